```python
import math
import jax, jax.numpy as jnp
from jax import lax
import numpy as np

D_MODEL = 1024
BATCH = 8
SEQ = 2048
DEPTH = 4

HEAD_DIM = 64
MOBA_HEADS = 8
MOBA_BLOCK = 256
MOBA_TOPK = 3
MOBA_QCHUNK = 16
DIFF_HEADS = 4
DIFF_HEAD_DIM = 64
DIL_HEADS = 8
DIL_CONFIGS = ((128, 1), (512, 4), (2048, 16))
DIL_BLOCK = 128
MLA_HEADS = 4
MLA_Q_RANK = 256
MLA_KV_RANK = 128
MLA_NOPE_DIM = 128
MLA_ROPE_DIM = 64
MLA_V_DIM = 128
ROPE_THETA = 10000.0
Q_BLOCK = 128
REL_BUCKETS = 32
REL_MAX_DIST = 1024
N_BIAS_HEADS = MOBA_HEADS + DIFF_HEADS + DIL_HEADS
D_FF = 2816
CONV_WIDTH = 3
EPS = 1e-6
NEG = -1e30
N_EVEN = (DEPTH + 1) // 2
N_ODD = DEPTH // 2
MOBA_W = MOBA_HEADS * HEAD_DIM
DIFF_QK_W = DIFF_HEADS * 2 * DIFF_HEAD_DIM
DIFF_V_W = DIFF_HEADS * 2 * DIFF_HEAD_DIM
EVEN_IN_SPLIT = (MOBA_W, MOBA_W, MOBA_W, DIFF_QK_W, DIFF_QK_W, DIFF_V_W)
EVEN_IN = sum(EVEN_IN_SPLIT)
EVEN_MIX = MOBA_W + DIFF_V_W
DIL_W = DIL_HEADS * HEAD_DIM
ODD_IN_SPLIT = (DIL_W, DIL_W, DIL_W, MLA_Q_RANK, MLA_KV_RANK, MLA_ROPE_DIM)
ODD_IN = sum(ODD_IN_SPLIT)
ODD_MIX = DIL_W + MLA_HEADS * MLA_V_DIM

kernel_name = 'hybrid_moba_diff_dilated_mla_convffn'


def rmsnorm(x, g):
    xf = x.astype(jnp.float32)
    y = xf * lax.rsqrt(jnp.mean(xf * xf, axis=-1, keepdims=True) + EPS)
    return (y * g.astype(jnp.float32)).astype(x.dtype)


def split_cols(x, sizes):
    outs, off = [], 0
    for s in sizes:
        outs.append(x[..., off:off + s])
        off += s
    return outs


def heads(x, n, d):
    b, s, _ = x.shape
    return x.reshape(b, s, n, d).transpose(0, 2, 1, 3)


def merge_heads(o):
    b, h, s, d = o.shape
    return o.transpose(0, 2, 1, 3).reshape(b, s, h * d)


def rel_bucket(dist):
    max_exact = REL_BUCKETS // 2
    n = jnp.maximum(dist, 0)
    nf = jnp.maximum(n, 1).astype(jnp.float32)
    large = max_exact + (jnp.log(nf / max_exact) / math.log(REL_MAX_DIST / max_exact)
                         * (REL_BUCKETS - max_exact)).astype(jnp.int32)
    large = jnp.minimum(large, REL_BUCKETS - 1)
    return jnp.where(n < max_exact, n, large)


def rope(x, pos):
    half = x.shape[-1] // 2
    freq = ROPE_THETA ** (-jnp.arange(half, dtype=jnp.float32) / half)
    ang = pos.astype(jnp.float32)[:, None] * freq[None, :]
    cos, sin = jnp.cos(ang), jnp.sin(ang)
    xf = x.astype(jnp.float32)
    x1, x2 = xf[..., :half], xf[..., half:]
    return jnp.concatenate([x1 * cos - x2 * sin, x1 * sin + x2 * cos], axis=-1).astype(x.dtype)


def moba_attention(q, k, v, bias_table):
    B, H, S, dh = q.shape
    blk = MOBA_BLOCK
    sp = -(-S // blk) * blk
    if sp != S:
        padw = ((0, 0), (0, 0), (0, sp - S), (0, 0))
        q, k, v = jnp.pad(q, padw), jnp.pad(k, padw), jnp.pad(v, padw)
    nb = sp // blk
    n_sel = min(MOBA_TOPK, nb - 1)
    scale = dh ** -0.5
    kb = k.reshape(B, H, nb, blk, dh)
    vb = v.reshape(B, H, nb, blk, dh)
    qblk = jnp.arange(sp) // blk
    table_t = bias_table.T
    head_ix = jnp.arange(H)[None, :, None, None, None]
    if n_sel > 0:
        kmean = jnp.mean(kb.astype(jnp.float32), axis=3)
        gate = jnp.einsum('bhsd,bhnd->bhsn', q.astype(jnp.float32), kmean)
        fully_past = jnp.arange(nb)[None, :] < qblk[:, None]
        gate = jnp.where(fully_past, gate, -jnp.inf)
        _, sel = lax.top_k(gate, n_sel)
        sel_ok = sel < qblk[:, None]
    gather_blocks = jax.vmap(jax.vmap(lambda xb, ix: xb[ix]))

    def chunk(c):
        start = c * MOBA_QCHUNK
        qc = lax.dynamic_slice_in_dim(q, start, MOBA_QCHUNK, axis=2)
        qpos = start + jnp.arange(MOBA_QCHUNK)
        own = start // blk
        k_own = lax.dynamic_index_in_dim(kb, own, axis=2, keepdims=False)
        v_own = lax.dynamic_index_in_dim(vb, own, axis=2, keepdims=False)
        dist_own = qpos[:, None] - (own * blk + jnp.arange(blk))[None, :]
        b_own = jnp.moveaxis(jnp.take(bias_table, rel_bucket(dist_own), axis=0), -1, 0)
        s_own = jnp.einsum('bhqd,bhkd->bhqk', qc, k_own).astype(jnp.float32) * scale + b_own
        s_own = jnp.where(dist_own >= 0, s_own, NEG)
        if n_sel == 0:
            p = jax.nn.softmax(s_own, axis=-1).astype(v.dtype)
            return jnp.einsum('bhqk,bhkd->bhqd', p, v_own)
        sel_c = lax.dynamic_slice_in_dim(sel, start, MOBA_QCHUNK, axis=2)
        ok_c = lax.dynamic_slice_in_dim(sel_ok, start, MOBA_QCHUNK, axis=2)
        flat = sel_c.reshape(B, H, MOBA_QCHUNK * n_sel)
        k_sel = gather_blocks(kb, flat).reshape(B, H, MOBA_QCHUNK, n_sel, blk, dh)
        v_sel = gather_blocks(vb, flat).reshape(B, H, MOBA_QCHUNK, n_sel, blk, dh)
        s_sel = jnp.einsum('bhqd,bhqnkd->bhqnk', qc, k_sel).astype(jnp.float32) * scale
        dist_sel = qpos[None, None, :, None, None] - (sel_c[..., None] * blk + jnp.arange(blk))
        s_sel = s_sel + table_t[head_ix, rel_bucket(dist_sel)]
        s_sel = jnp.where(ok_c[..., None], s_sel, NEG)
        s = jnp.concatenate([s_own, s_sel.reshape(B, H, MOBA_QCHUNK, n_sel * blk)], axis=-1)
        p = jax.nn.softmax(s, axis=-1).astype(v.dtype)
        p_own = p[..., :blk]
        p_sel = p[..., blk:].reshape(B, H, MOBA_QCHUNK, n_sel, blk)
        return (jnp.einsum('bhqk,bhkd->bhqd', p_own, v_own)
                + jnp.einsum('bhqnk,bhqnkd->bhqd', p_sel, v_sel))

    o = lax.map(chunk, jnp.arange(sp // MOBA_QCHUNK))
    o = jnp.moveaxis(o, 0, 2).reshape(B, H, sp, dh)
    return o[:, :, :S]


def diff_attention(q, k, v, lam, subln_g, lam_init, bias_table):
    B, H, _, S, dh = q.shape
    scale = dh ** -0.5
    kpos = jnp.arange(S)

    def block(i):
        qb = lax.dynamic_slice_in_dim(q, i * Q_BLOCK, Q_BLOCK, axis=3)
        qpos = i * Q_BLOCK + jnp.arange(Q_BLOCK)
        dist = qpos[:, None] - kpos[None, :]
        bias = jnp.moveaxis(jnp.take(bias_table, rel_bucket(dist), axis=0), -1, 0)
        s = jnp.einsum('bhcqd,bhckd->bhcqk', qb, k).astype(jnp.float32) * scale
        s = jnp.where(dist >= 0, s + bias[None, :, None], NEG)
        p = jax.nn.softmax(s, axis=-1)
        a = (p[:, :, 0] - lam * p[:, :, 1]).astype(v.dtype)
        return jnp.einsum('bhqk,bhkd->bhqd', a, v)

    o = lax.map(block, jnp.arange(S // Q_BLOCK))
    o = jnp.moveaxis(o, 0, 2).reshape(B, H, S, 2 * dh)
    return rmsnorm(o, subln_g) * (1.0 - lam_init)


def dilated_attention(q, k, v, bias_table):
    B, H, S, dh = q.shape
    blk = DIL_BLOCK
    scale = dh ** -0.5
    qi = jnp.arange(blk)
    kj = jnp.arange(2 * blk) - blk
    step = qi[:, None] - kj[None, :]
    outs, lses = [], []
    for window, dil in DIL_CONFIGS:
        span = window // dil
        L = S // dil
        lp = -(-L // blk) * blk
        nblk = lp // blk

        def strided(x):
            x = x.reshape(B, H, L, dil, dh).transpose(0, 1, 3, 2, 4)
            x = jnp.pad(x, ((0, 0), (0, 0), (0, 0), (0, lp - L), (0, 0)))
            return x.reshape(B, H, dil, nblk, blk, dh)

        def band(x):
            prev = jnp.pad(x, ((0, 0), (0, 0), (0, 0), (1, 0), (0, 0), (0, 0)))[:, :, :, :-1]
            return jnp.concatenate([prev, x], axis=4)

        qb = strided(q)
        kband = band(strided(k))
        vband = band(strided(v))
        s = jnp.einsum('bhrnqd,bhrnkd->bhrnqk', qb, kband).astype(jnp.float32) * scale
        bias = jnp.moveaxis(jnp.take(bias_table, rel_bucket(step * dil), axis=0), -1, 0)
        key_exists = (jnp.arange(nblk)[:, None] * blk + kj[None, :]) >= 0
        mask = ((step >= 0) & (step <= span))[None] & key_exists[:, None, :]
        s = jnp.where(mask, s + bias[None, :, None, None], NEG)
        m = jnp.max(s, axis=-1, keepdims=True)
        e = jnp.exp(s - m)
        den = jnp.sum(e, axis=-1, keepdims=True)
        o = jnp.einsum('bhrnqk,bhrnkd->bhrnqd', (e / den).astype(v.dtype), vband)
        lse = (m + jnp.log(den))[..., 0]
        o = o.reshape(B, H, dil, lp, dh)[:, :, :, :L].transpose(0, 1, 3, 2, 4).reshape(B, H, S, dh)
        lse = lse.reshape(B, H, dil, lp)[:, :, :, :L].transpose(0, 1, 3, 2).reshape(B, H, S)
        outs.append(o)
        lses.append(lse)
    wts = jax.nn.softmax(jnp.stack(lses, axis=0), axis=0)
    return jnp.einsum('gbhs,gbhsd->bhsd', wts.astype(v.dtype), jnp.stack(outs, axis=0))


def mla_attention(q_nope, q_rope, k_nope, k_rope, v):
    B, H, S, _ = q_nope.shape
    scale = (MLA_NOPE_DIM + MLA_ROPE_DIM) ** -0.5
    kpos = jnp.arange(S)

    def block(i):
        qn = lax.dynamic_slice_in_dim(q_nope, i * Q_BLOCK, Q_BLOCK, axis=2)
        qr = lax.dynamic_slice_in_dim(q_rope, i * Q_BLOCK, Q_BLOCK, axis=2)
        qpos = i * Q_BLOCK + jnp.arange(Q_BLOCK)
        s = (jnp.einsum('bhqd,bhkd->bhqk', qn, k_nope)
             + jnp.einsum('bhqd,bkd->bhqk', qr, k_rope)).astype(jnp.float32) * scale
        s = jnp.where(qpos[:, None] >= kpos[None, :], s, NEG)
        p = jax.nn.softmax(s, axis=-1).astype(v.dtype)
        return jnp.einsum('bhqk,bhkd->bhqd', p, v)

    o = lax.map(block, jnp.arange(S // Q_BLOCK))
    return jnp.moveaxis(o, 0, 2).reshape(B, H, S, MLA_V_DIM)


def even_mixer(h, w_in, lam_params, subln_g, w_out, rel_table, lam_init):
    B, S, _ = h.shape
    qa, ka, va, qd, kd, vd = split_cols(h @ w_in, EVEN_IN_SPLIT)
    o_a = moba_attention(heads(qa, MOBA_HEADS, HEAD_DIM), heads(ka, MOBA_HEADS, HEAD_DIM),
                         heads(va, MOBA_HEADS, HEAD_DIM), rel_table[:, :MOBA_HEADS])
    qd = qd.reshape(B, S, DIFF_HEADS, 2, DIFF_HEAD_DIM).transpose(0, 2, 3, 1, 4)
    kd = kd.reshape(B, S, DIFF_HEADS, 2, DIFF_HEAD_DIM).transpose(0, 2, 3, 1, 4)
    vd = heads(vd, DIFF_HEADS, 2 * DIFF_HEAD_DIM)
    lp = lam_params.astype(jnp.float32)
    lam = jnp.exp(jnp.sum(lp[0] * lp[1])) - jnp.exp(jnp.sum(lp[2] * lp[3])) + lam_init
    o_b = diff_attention(qd, kd, vd, lam, subln_g, lam_init,
                         rel_table[:, MOBA_HEADS:MOBA_HEADS + DIFF_HEADS])
    return jnp.concatenate([merge_heads(o_a), merge_heads(o_b)], axis=-1) @ w_out


def odd_mixer(h, w_in, q_norm, w_uq, kv_norm, w_ukv, w_out, rel_table):
    B, S, _ = h.shape
    qc, kc, vc, cq, ckv, kr = split_cols(h @ w_in, ODD_IN_SPLIT)
    o_c = dilated_attention(heads(qc, DIL_HEADS, HEAD_DIM), heads(kc, DIL_HEADS, HEAD_DIM),
                            heads(vc, DIL_HEADS, HEAD_DIM), rel_table[:, MOBA_HEADS + DIFF_HEADS:])
    pos = jnp.arange(S)
    q = heads(rmsnorm(cq, q_norm) @ w_uq, MLA_HEADS, MLA_NOPE_DIM + MLA_ROPE_DIM)
    q_nope, q_rope = q[..., :MLA_NOPE_DIM], rope(q[..., MLA_NOPE_DIM:], pos)
    kv = heads(rmsnorm(ckv, kv_norm) @ w_ukv, MLA_HEADS, MLA_NOPE_DIM + MLA_V_DIM)
    k_nope, v = kv[..., :MLA_NOPE_DIM], kv[..., MLA_NOPE_DIM:]
    o_d = mla_attention(q_nope, q_rope, k_nope, rope(kr, pos), v)
    return jnp.concatenate([merge_heads(o_c), merge_heads(o_d)], axis=-1) @ w_out


def conv_ffn(h, w_in, conv_w, conv_b, w_out):
    u, g = jnp.split(h @ w_in, 2, axis=-1)
    g = lax.conv_general_dilated(g, conv_w[:, None, :], window_strides=(1,),
                                 padding=[(CONV_WIDTH - 1, 0)],
                                 dimension_numbers=('NWC', 'WIO', 'NWC'),
                                 feature_group_count=D_FF) + conv_b
    return (jax.nn.gelu(g, approximate=False) * u) @ w_out


def setup_inputs(seed: int = 0) -> dict:
    key = jax.random.key(seed)
    ks = jax.random.split(key, 20)
    f32 = jnp.float32

    def w(k, shape, fan_in):
        return jax.random.normal(k, shape, f32) * (fan_in ** -0.5)

    def gain(k, shape):
        return 1.0 + 0.02 * jax.random.normal(k, shape, f32)

    return {
        'x': jax.random.normal(ks[0], (BATCH, SEQ, D_MODEL), f32),
        'rel_bias': 0.2 * jax.random.normal(ks[1], (REL_BUCKETS, N_BIAS_HEADS), f32),
        'even_norm1': gain(ks[2], (N_EVEN, D_MODEL)),
        'even_w_in': w(ks[3], (N_EVEN, D_MODEL, EVEN_IN), D_MODEL),
        'diff_lambda': 0.1 * jax.random.normal(ks[4], (N_EVEN, 4, DIFF_HEAD_DIM), f32),
        'diff_subln': gain(ks[5], (N_EVEN, 2 * DIFF_HEAD_DIM)),
        'even_w_out': w(ks[6], (N_EVEN, EVEN_MIX, D_MODEL), EVEN_MIX),
        'odd_norm1': gain(ks[7], (N_ODD, D_MODEL)),
        'odd_w_in': w(ks[8], (N_ODD, D_MODEL, ODD_IN), D_MODEL),
        'mla_q_norm': gain(ks[9], (N_ODD, MLA_Q_RANK)),
        'mla_w_uq': w(ks[10], (N_ODD, MLA_Q_RANK, MLA_HEADS * (MLA_NOPE_DIM + MLA_ROPE_DIM)), MLA_Q_RANK),
        'mla_kv_norm': gain(ks[11], (N_ODD, MLA_KV_RANK)),
        'mla_w_ukv': w(ks[12], (N_ODD, MLA_KV_RANK, MLA_HEADS * (MLA_NOPE_DIM + MLA_V_DIM)), MLA_KV_RANK),
        'odd_w_out': w(ks[13], (N_ODD, ODD_MIX, D_MODEL), ODD_MIX),
        'ffn_norm': gain(ks[14], (DEPTH, D_MODEL)),
        'ffn_w_in': w(ks[15], (DEPTH, D_MODEL, 2 * D_FF), D_MODEL),
        'ffn_conv_w': w(ks[16], (DEPTH, CONV_WIDTH, D_FF), CONV_WIDTH),
        'ffn_conv_b': 0.02 * jax.random.normal(ks[17], (DEPTH, D_FF), f32),
        'ffn_w_out': w(ks[18], (DEPTH, D_FF, D_MODEL), D_FF),
        'final_norm': gain(ks[19], (D_MODEL,)),
    }


def reference(x, rel_bias, even_norm1, even_w_in, diff_lambda, diff_subln, even_w_out,
              odd_norm1, odd_w_in, mla_q_norm, mla_w_uq, mla_kv_norm, mla_w_ukv, odd_w_out,
              ffn_norm, ffn_w_in, ffn_conv_w, ffn_conv_b, ffn_w_out, final_norm):
    h = x
    for layer in range(DEPTH):
        i = layer // 2
        if layer % 2 == 0:
            lam_init = 0.8 - 0.6 * math.exp(-0.3 * layer)
            h = h + even_mixer(rmsnorm(h, even_norm1[i]), even_w_in[i], diff_lambda[i],
                               diff_subln[i], even_w_out[i], rel_bias, lam_init)
        else:
            h = h + odd_mixer(rmsnorm(h, odd_norm1[i]), odd_w_in[i], mla_q_norm[i], mla_w_uq[i],
                              mla_kv_norm[i], mla_w_ukv[i], odd_w_out[i], rel_bias)
        h = h + conv_ffn(rmsnorm(h, ffn_norm[layer]), ffn_w_in[layer], ffn_conv_w[layer],
                         ffn_conv_b[layer], ffn_w_out[layer])
    return rmsnorm(h, final_norm)
```

```python
import functools
import math

import numpy as np
import jax
import jax.numpy as jnp
from jax import lax
from jax.experimental import pallas as pl
from jax.experimental.pallas import tpu as pltpu

D_MODEL = 1024
DEPTH = 4
HEAD_DIM = 64
MOBA_HEADS = 8
MOBA_BLOCK = 256
MOBA_TOPK = 3
DIFF_HEADS = 4
DIL_HEADS = 8
DIL_CONFIGS = ((128, 1), (512, 4), (2048, 16))
MLA_HEADS = 4
MLA_Q_RANK = 256
MLA_KV_RANK = 128
MLA_NOPE_DIM = 128
MLA_ROPE_DIM = 64
MLA_V_DIM = 128
ROPE_THETA = 10000.0
REL_BUCKETS = 32
REL_MAX_DIST = 1024
N_BIAS_HEADS = MOBA_HEADS + DIFF_HEADS + DIL_HEADS
D_FF = 2816
EPS = 1e-6
NEG = -1e30

LANES = 128
TQ = 256
TK = 256
FF_CHUNK = 256
VMEM_LIMIT = 56 * 1024 * 1024

F32 = jnp.float32
BF16 = jnp.bfloat16


def _params(n_axes, vmem=None):
    return pltpu.CompilerParams(dimension_semantics=("arbitrary",) * n_axes,
                                vmem_limit_bytes=vmem)


def _rms(x, g):
    ms = jnp.mean(x * x, axis=-1, keepdims=True)
    return x * lax.rsqrt(ms + EPS) * g


def _norm_matmul_kernel(x_ref, g_ref, w_ref, o_ref, xn_ref):
    @pl.when(pl.program_id(1) == 0)
    def _():
        xn_ref[...] = _rms(x_ref[...], g_ref[...]).astype(BF16)

    o_ref[...] = jnp.dot(xn_ref[...], w_ref[...],
                         preferred_element_type=F32).astype(o_ref.dtype)


def norm_matmul(x, g, w, out_dtype, tm=1024, tn=512):
    m, d = x.shape
    n = w.shape[1]
    tn = min(tn, n)
    return pl.pallas_call(
        _norm_matmul_kernel,
        grid=(m // tm, n // tn),
        in_specs=[pl.BlockSpec((tm, d), lambda i, j: (i, 0)),
                  pl.BlockSpec((1, d), lambda i, j: (0, 0)),
                  pl.BlockSpec((d, tn), lambda i, j: (0, j))],
        out_specs=pl.BlockSpec((tm, tn), lambda i, j: (i, j)),
        out_shape=jax.ShapeDtypeStruct((m, n), out_dtype),
        scratch_shapes=[pltpu.VMEM((tm, d), BF16)],
        compiler_params=_params(2, VMEM_LIMIT),
        name="norm_matmul",
    )(x, g.reshape(1, d), w)


def _proj_res_kernel(*refs, n_in, final_norm):
    a_refs = refs[:n_in]
    w_refs = refs[n_in:2 * n_in]
    res_ref = refs[2 * n_in]
    o_ref = refs[-1]
    acc = res_ref[...]
    for a_ref, w_ref in zip(a_refs, w_refs):
        acc = acc + jnp.dot(a_ref[...], w_ref[...], preferred_element_type=F32)
    if final_norm:
        acc = _rms(acc, refs[2 * n_in + 1][...])
    o_ref[...] = acc


def proj_residual(acts, ws, res, final_g=None, tm=512):
    m, d = res.shape
    n_in = len(acts)
    in_specs = [pl.BlockSpec((tm, a.shape[1]), lambda i: (i, 0)) for a in acts]
    in_specs += [pl.BlockSpec(w.shape, lambda i: (0, 0)) for w in ws]
    in_specs += [pl.BlockSpec((tm, d), lambda i: (i, 0))]
    args = list(acts) + list(ws) + [res]
    if final_g is not None:
        in_specs += [pl.BlockSpec((1, d), lambda i: (0, 0))]
        args += [final_g.reshape(1, d)]
    return pl.pallas_call(
        functools.partial(_proj_res_kernel, n_in=n_in, final_norm=final_g is not None),
        grid=(m // tm,),
        in_specs=in_specs,
        out_specs=pl.BlockSpec((tm, d), lambda i: (i, 0)),
        out_shape=jax.ShapeDtypeStruct((m, d), F32),
        compiler_params=_params(1, VMEM_LIMIT),
        name="proj_residual",
    )(*args)


def _ffn_in_kernel(x_ref, g_ref, wu_ref, wg_ref, cw_ref, cb_ref, o_ref, xn_ref):
    @pl.when(pl.program_id(1) == 0)
    def _():
        xn_ref[...] = _rms(x_ref[0], g_ref[...]).astype(BF16)

    xn = xn_ref[...]
    u = jnp.dot(xn, wu_ref[...], preferred_element_type=F32)
    gt = jnp.dot(xn, wg_ref[...], preferred_element_type=F32)
    row = lax.broadcasted_iota(jnp.int32, gt.shape, 0)
    g1 = jnp.where(row >= 1, pltpu.roll(gt, 1, 0), 0.0)
    g2 = jnp.where(row >= 2, pltpu.roll(gt, 2, 0), 0.0)
    cw = cw_ref[...]
    z = cw[2:3] * gt + cw[1:2] * g1 + cw[0:1] * g2 + cb_ref[...]
    gelu = 0.5 * z * (1.0 + lax.erf(z * math.sqrt(0.5)))
    o_ref[0] = (gelu * u).astype(BF16)


def ffn_in(h3, g, w_in, conv_w, conv_b):
    b, s, d = h3.shape
    nc = D_FF // FF_CHUNK
    return pl.pallas_call(
        _ffn_in_kernel,
        grid=(b, nc),
        in_specs=[pl.BlockSpec((1, s, d), lambda i, c: (i, 0, 0)),
                  pl.BlockSpec((1, d), lambda i, c: (0, 0)),
                  pl.BlockSpec((d, FF_CHUNK), lambda i, c: (0, c)),
                  pl.BlockSpec((d, FF_CHUNK), lambda i, c: (0, c + nc)),
                  pl.BlockSpec((3, FF_CHUNK), lambda i, c: (0, c)),
                  pl.BlockSpec((1, FF_CHUNK), lambda i, c: (0, c))],
        out_specs=pl.BlockSpec((1, s, FF_CHUNK), lambda i, c: (i, 0, c)),
        out_shape=jax.ShapeDtypeStruct((b, s, D_FF), BF16),
        scratch_shapes=[pltpu.VMEM((s, d), BF16)],
        compiler_params=_params(2, VMEM_LIMIT),
        name="ffn_in",
    )(h3, g.reshape(1, d), w_in, w_in, conv_w, conv_b.reshape(1, D_FF))


def _bucket_of_distance(s):
    max_exact = REL_BUCKETS // 2
    n_large = REL_BUCKETS - max_exact
    thresholds = []
    for k in range(1, n_large):
        t = max_exact * (REL_MAX_DIST / max_exact) ** (k / n_large)
        ti = int(round(t))
        thresholds.append(ti if abs(t - ti) < 1e-9 else int(math.ceil(t)))
    d = np.arange(s)
    large = max_exact + sum((d >= t).astype(np.int64) for t in thresholds)
    return np.where(d < max_exact, d, np.minimum(large, REL_BUCKETS - 1)).astype(np.int32)


def _dilated_log_multiplicity(s):
    d = np.arange(s)
    count = np.zeros(s, np.int64)
    for window, dil in DIL_CONFIGS:
        count += ((d % dil == 0) & (d // dil <= window // dil)).astype(np.int64)
    return np.where(count > 0, np.log(np.maximum(count, 1)), NEG).astype(np.float32)


def _bias_strip_kernel(u_ref, o_ref, *, s):
    for blk in range(s // TQ):
        start = s - TQ * blk
        win = u_ref[0, :, start:start + 2 * TK]
        tile = jnp.broadcast_to(win, (TQ, 2 * TK))
        tile = pltpu.roll(tile, TK, 1, stride=1, stride_axis=0)
        o_ref[0, blk * TQ:(blk + 1) * TQ, :] = tile[:, :TK]


def bias_strips(rel_bias, s):
    bucket = _bucket_of_distance(s)
    per_dist = jnp.take(rel_bias.T.astype(F32), jnp.asarray(bucket), axis=1)
    logmult = jnp.asarray(_dilated_log_multiplicity(s))
    dil0 = MOBA_HEADS + DIFF_HEADS
    covered = logmult > 0.5 * NEG
    dil_rows = jnp.where(covered[None, :], per_dist[dil0:] + logmult[None, :], NEG)
    per_dist = jnp.concatenate([per_dist[:dil0], dil_rows], axis=0)
    nh = per_dist.shape[0]
    neg = jnp.full((nh, TK), NEG, F32)
    u = jnp.concatenate([per_dist[:, :1], neg, per_dist[:, :0:-1]], axis=1)
    u = jnp.concatenate([u, u[:, :TK]], axis=1).reshape(nh, 1, s + 2 * TK)
    return pl.pallas_call(
        functools.partial(_bias_strip_kernel, s=s),
        grid=(nh,),
        in_specs=[pl.BlockSpec((1, 1, s + 2 * TK), lambda h: (h, 0, 0))],
        out_specs=pl.BlockSpec((1, s, TK), lambda h: (h, 0, 0)),
        out_shape=jax.ShapeDtypeStruct((nh, s, TK), F32),
        compiler_params=_params(1),
        name="bias_strips",
    )(u)


def _flash_kernel(*refs, mode, moba, lam_init, s):
    if mode == "diff":
        q_ref, k_ref, v_ref, b_ref, lam_ref, sg_ref, o_ref = refs
    elif moba:
        q_ref, k_ref, v_ref, b_ref, o_ref, kaug_ref, kmp_ref = refs
    else:
        q_ref, k_ref, v_ref, b_ref, o_ref = refs
    i = pl.program_id(2)
    nblk = s // TK
    q = q_ref[0]
    lane = lax.broadcasted_iota(jnp.int32, (TQ, LANES), 1)
    halves = (lane < HEAD_DIM, lane >= HEAD_DIM)
    nt = (((1,), (1,)), ((), ()))

    if moba:
        @pl.when(i == 0)
        def _():
            k_all = k_ref[0]
            rowblk = lax.broadcasted_iota(jnp.int32, (s, LANES), 0) // TK
            lane_s = lax.broadcasted_iota(jnp.int32, (s, LANES), 1)
            kmean = jnp.mean(k_all.astype(F32).reshape(nblk, TK, LANES), axis=1)
            for c in range(2):
                base = HEAD_DIM if c == 0 else 0
                in_half = (lane_s < HEAD_DIM) if c == 0 else (lane_s >= HEAD_DIM)
                onehot = jnp.where(lane_s - base == rowblk, 1.0, 0.0).astype(BF16)
                kaug_ref[c] = jnp.where(in_half, k_all, onehot)
                kmp_ref[c] = jnp.zeros((LANES, LANES), F32)
                kmp_ref[c, base:base + nblk, :] = kmean

    qs = []
    for c in range(2):
        qc = jnp.where(halves[c], q, jnp.zeros_like(q))
        if moba:
            base = HEAD_DIM if c == 0 else 0
            gate = lax.dot_general(qc.astype(F32), kmp_ref[c], nt,
                                   precision=lax.Precision.HIGHEST,
                                   preferred_element_type=F32)
            blk_of_lane = lane - base
            rank = jnp.zeros((TQ, LANES), jnp.int32)
            for n in range(nblk):
                col = gate[:, base + n:base + n + 1]
                beats = (col > gate) | ((col == gate) & (n < blk_of_lane))
                rank = rank + jnp.where(beats & (n < i), 1, 0)
            keep = ((rank < MOBA_TOPK) & (blk_of_lane < i)) | (blk_of_lane == i)
            in_gate = (blk_of_lane >= 0) & (blk_of_lane < nblk)
            selneg = jnp.where(in_gate & jnp.logical_not(keep), NEG, 0.0).astype(BF16)
            qc = jnp.where(halves[c], q, selneg)
        qs.append(qc)

    def k_tile(c, start):
        if moba:
            return kaug_ref[c, pl.ds(start, TK), :]
        return k_ref[0, pl.ds(start, TK), :]

    def scores(c, start, delta_start):
        sc = lax.dot_general(qs[c], k_tile(c, start), nt, preferred_element_type=F32)
        bsel = c if mode == "pair" else 0
        return sc + b_ref[bsel, pl.ds(delta_start, TQ), :]

    d0 = pl.multiple_of(i * TK, TK)
    vd = v_ref[0, pl.ds(d0, TK), :]
    init = []
    for c in range(2):
        sc = scores(c, d0, 0)
        m = jnp.max(sc, axis=1, keepdims=True)
        p = jnp.exp(sc - m)
        l = jnp.sum(p, axis=1, keepdims=True)
        acc = jnp.dot(p.astype(BF16), vd, preferred_element_type=F32)
        init += [m, l, acc]

    def body(j, carry):
        start = pl.multiple_of(j * TK, TK)
        dstart = pl.multiple_of((i - j) * TQ, TQ)
        vj = v_ref[0, pl.ds(start, TK), :]
        out = []
        for c in range(2):
            m, l, acc = carry[3 * c:3 * c + 3]
            sc = scores(c, start, dstart)
            m_new = jnp.maximum(m, jnp.max(sc, axis=1, keepdims=True))
            alpha = jnp.exp(m - m_new)
            p = jnp.exp(sc - m_new)
            l = alpha * l + jnp.sum(p, axis=1, keepdims=True)
            acc = alpha * acc + jnp.dot(p.astype(BF16), vj, preferred_element_type=F32)
            out += [m_new, l, acc]
        return tuple(out)

    fin = lax.fori_loop(0, i, body, tuple(init))
    o0 = fin[2] / fin[1]
    o1 = fin[5] / fin[4]
    if mode == "pair":
        o_ref[0] = jnp.where(halves[0], o0, o1).astype(o_ref.dtype)
    else:
        lp = lam_ref[...]
        lam = (jnp.exp(jnp.sum(lp[0:1] * lp[1:2], axis=1, keepdims=True))
               - jnp.exp(jnp.sum(lp[2:3] * lp[3:4], axis=1, keepdims=True)) + lam_init)
        o = o0 - lam * o1
        o_ref[0] = (_rms(o, sg_ref[...]) * (1.0 - lam_init)).astype(o_ref.dtype)


def flash_attention(qkv, col_q, col_k, col_v, n_blocks, strips, head0, mode, moba=False,
                    lam_params=None, subln_g=None, lam_init=0.0):
    b, s, _ = qkv.shape
    nq = s // TQ
    nb = 2 if mode == "pair" else 1
    in_specs = [pl.BlockSpec((1, TQ, LANES), lambda p, bb, i: (bb, i, col_q + p)),
                pl.BlockSpec((1, s, LANES), lambda p, bb, i: (bb, 0, col_k + p)),
                pl.BlockSpec((1, s, LANES), lambda p, bb, i: (bb, 0, col_v + p)),
                pl.BlockSpec((nb, s, TK), lambda p, bb, i: (head0 // nb + p, 0, 0))]
    args = [qkv, qkv, qkv, strips]
    scratch = []
    if mode == "diff":
        in_specs += [pl.BlockSpec((4, HEAD_DIM), lambda p, bb, i: (0, 0)),
                     pl.BlockSpec((1, LANES), lambda p, bb, i: (0, 0))]
        args += [lam_params.astype(F32), subln_g.reshape(1, LANES)]
    if moba:
        scratch = [pltpu.VMEM((2, s, LANES), BF16), pltpu.VMEM((2, LANES, LANES), F32)]
    return pl.pallas_call(
        functools.partial(_flash_kernel, mode=mode, moba=moba, lam_init=lam_init, s=s),
        grid=(n_blocks, b, nq),
        in_specs=in_specs,
        out_specs=pl.BlockSpec((1, TQ, LANES), lambda p, bb, i: (bb, i, p)),
        out_shape=jax.ShapeDtypeStruct((b, s, n_blocks * LANES), BF16),
        scratch_shapes=scratch,
        compiler_params=_params(3, VMEM_LIMIT),
        name="flash_" + mode + ("_moba" if moba else ""),
    )(*args)


def _mla_up_kernel(lat_ref, qg_ref, kvg_ref, wa_ref, wb_ref, wkv_ref, cq_ref, sq_ref,
                   ck_ref, sk_ref, q_ref, kv_ref, kr_ref):
    lat = lat_ref[...]
    cqn = _rms(lat[:, :MLA_Q_RANK], qg_ref[...]).astype(BF16)
    ckvn = _rms(lat[:, MLA_Q_RANK:MLA_Q_RANK + MLA_KV_RANK], kvg_ref[...]).astype(BF16)
    qa = jnp.dot(cqn, wa_ref[...], preferred_element_type=F32)
    qb = jnp.dot(cqn, wb_ref[...], preferred_element_type=F32)
    cq = cq_ref[...]
    sq = sq_ref[...]
    hw = 2 * LANES
    for h in range(MLA_HEADS):
        q_ref[:, h * hw:(h + 1) * hw] = (qa[:, h * hw:(h + 1) * hw] * cq
                                         + qb[:, h * hw:(h + 1) * hw] * sq).astype(BF16)
    kv_ref[...] = jnp.dot(ckvn, wkv_ref[...], preferred_element_type=F32).astype(BF16)
    x = lat[:, MLA_Q_RANK + MLA_KV_RANK:]
    kr_ref[...] = (x * ck_ref[...] + pltpu.roll(x, MLA_ROPE_DIM, 1) * sk_ref[...]).astype(BF16)


def mla_up(latent, q_norm, kv_norm, wa, wb, wkv, rope_tabs, s, tm=1024):
    m = latent.shape[0]
    cq, sq, ck, sk = rope_tabs
    per_seq = s // tm
    tab = lambda w: pl.BlockSpec((tm, w), lambda i: (i % per_seq, 0))
    full = lambda a: pl.BlockSpec(a.shape, lambda i: (0, 0))
    hw = 2 * LANES
    return pl.pallas_call(
        _mla_up_kernel,
        grid=(m // tm,),
        in_specs=[pl.BlockSpec((tm, latent.shape[1]), lambda i: (i, 0)),
                  pl.BlockSpec((1, MLA_Q_RANK), lambda i: (0, 0)),
                  pl.BlockSpec((1, MLA_KV_RANK), lambda i: (0, 0)),
                  full(wa), full(wb), full(wkv), tab(hw), tab(hw), tab(LANES), tab(LANES)],
        out_specs=[pl.BlockSpec((tm, MLA_HEADS * hw), lambda i: (i, 0)),
                   pl.BlockSpec((tm, MLA_HEADS * hw), lambda i: (i, 0)),
                   pl.BlockSpec((tm, LANES), lambda i: (i, 0))],
        out_shape=[jax.ShapeDtypeStruct((m, MLA_HEADS * hw), BF16),
                   jax.ShapeDtypeStruct((m, MLA_HEADS * hw), BF16),
                   jax.ShapeDtypeStruct((m, LANES), BF16)],
        compiler_params=_params(1, VMEM_LIMIT),
        name="mla_up",
    )(latent, q_norm.reshape(1, -1), kv_norm.reshape(1, -1), wa, wb, wkv, cq, sq, ck, sk)


def _mla_flash_kernel(qn_ref, qr_ref, kn_ref, kr_ref, v_ref, o_ref):
    i = pl.program_id(2)
    scale = (MLA_NOPE_DIM + MLA_ROPE_DIM) ** -0.5
    nt = (((1,), (1,)), ((), ()))
    qn = qn_ref[0]
    qr = qr_ref[0]

    def scores(start):
        sc = lax.dot_general(qn, kn_ref[0, pl.ds(start, TK), :], nt, preferred_element_type=F32)
        sc = sc + lax.dot_general(qr, kr_ref[0, pl.ds(start, TK), :], nt,
                                  preferred_element_type=F32)
        return sc * scale

    d0 = pl.multiple_of(i * TK, TK)
    row = lax.broadcasted_iota(jnp.int32, (TQ, TK), 0)
    col = lax.broadcasted_iota(jnp.int32, (TQ, TK), 1)
    sc = jnp.where(row >= col, scores(d0), NEG)
    m = jnp.max(sc, axis=1, keepdims=True)
    p = jnp.exp(sc - m)
    l = jnp.sum(p, axis=1, keepdims=True)
    acc = jnp.dot(p.astype(BF16), v_ref[0, pl.ds(d0, TK), :], preferred_element_type=F32)

    def body(j, carry):
        m, l, acc = carry
        start = pl.multiple_of(j * TK, TK)
        sc = scores(start)
        m_new = jnp.maximum(m, jnp.max(sc, axis=1, keepdims=True))
        alpha = jnp.exp(m - m_new)
        p = jnp.exp(sc - m_new)
        l = alpha * l + jnp.sum(p, axis=1, keepdims=True)
        acc = alpha * acc + jnp.dot(p.astype(BF16), v_ref[0, pl.ds(start, TK), :],
                                    preferred_element_type=F32)
        return m_new, l, acc

    m, l, acc = lax.fori_loop(0, i, body, (m, l, acc))
    o_ref[0] = (acc / l).astype(o_ref.dtype)


def mla_attention(q, kv, kr):
    b, s, _ = q.shape
    nq = s // TQ
    return pl.pallas_call(
        _mla_flash_kernel,
        grid=(MLA_HEADS, b, nq),
        in_specs=[pl.BlockSpec((1, TQ, LANES), lambda h, bb, i: (bb, i, 2 * h)),
                  pl.BlockSpec((1, TQ, LANES), lambda h, bb, i: (bb, i, 2 * h + 1)),
                  pl.BlockSpec((1, s, LANES), lambda h, bb, i: (bb, 0, 2 * h)),
                  pl.BlockSpec((1, s, LANES), lambda h, bb, i: (bb, 0, 0)),
                  pl.BlockSpec((1, s, LANES), lambda h, bb, i: (bb, 0, 2 * h + 1))],
        out_specs=pl.BlockSpec((1, TQ, LANES), lambda h, bb, i: (bb, i, h)),
        out_shape=jax.ShapeDtypeStruct((b, s, MLA_HEADS * LANES), BF16),
        compiler_params=_params(3, VMEM_LIMIT),
        name="mla_flash",
    )(q, q, kv, kr, kv)


def _rope_tables(s):
    half = MLA_ROPE_DIM // 2
    freq = ROPE_THETA ** (-jnp.arange(half, dtype=F32) / half)
    ang = jnp.arange(s, dtype=F32)[:, None] * freq[None, :]
    cos, sin = jnp.cos(ang), jnp.sin(ang)
    cos2 = jnp.concatenate([cos, cos], axis=1)
    sin2 = jnp.concatenate([-sin, sin], axis=1)
    z64 = jnp.zeros((s, MLA_ROPE_DIM), F32)
    cq = jnp.concatenate([jnp.ones((s, MLA_NOPE_DIM), F32), cos2, z64], axis=1)
    sq = jnp.concatenate([jnp.zeros((s, MLA_NOPE_DIM), F32), sin2, z64], axis=1)
    ck = jnp.concatenate([cos2, z64], axis=1)
    sk = jnp.concatenate([sin2, z64], axis=1)
    return cq, sq, ck, sk


def _swap_halves(w):
    half = w.shape[-1] // 2
    return jnp.concatenate([w[..., half:], w[..., :half]], axis=-1)


def _mla_q_weights(w_uq):
    r = w_uq.shape[0]
    w = w_uq.reshape(r, MLA_HEADS, MLA_NOPE_DIM + MLA_ROPE_DIM)
    nope, rope = w[..., :MLA_NOPE_DIM], w[..., MLA_NOPE_DIM:]
    z64 = jnp.zeros((r, MLA_HEADS, MLA_ROPE_DIM), w.dtype)
    z128 = jnp.zeros((r, MLA_HEADS, MLA_NOPE_DIM), w.dtype)
    wa = jnp.concatenate([nope, rope, z64], axis=-1).reshape(r, -1)
    wb = jnp.concatenate([z128, _swap_halves(rope), z64], axis=-1).reshape(r, -1)
    return wa.astype(BF16), wb.astype(BF16)


def kernel(x, rel_bias, even_norm1, even_w_in, diff_lambda, diff_subln, even_w_out,
           odd_norm1, odd_w_in, mla_q_norm, mla_w_uq, mla_kv_norm, mla_w_ukv, odd_w_out,
           ffn_norm, ffn_w_in, ffn_conv_w, ffn_conv_b, ffn_w_out, final_norm):
    b, s, d = x.shape
    m = b * s
    strips = bias_strips(rel_bias, s)
    rope_tabs = _rope_tables(s)
    qscale = HEAD_DIM ** -0.5
    h = x.reshape(m, d)
    for layer in range(DEPTH):
        li = layer // 2
        if layer % 2 == 0:
            lam_init = 0.8 - 0.6 * math.exp(-0.3 * layer)
            w = even_w_in[li]
            mw, dw = MOBA_HEADS * HEAD_DIM, DIFF_HEADS * 2 * HEAD_DIM
            w = jnp.concatenate([w[:, :mw] * qscale, w[:, mw:3 * mw],
                                 w[:, 3 * mw:3 * mw + dw] * qscale, w[:, 3 * mw + dw:]],
                                axis=1).astype(BF16)
            qkv = norm_matmul(h, even_norm1[li], w, BF16).reshape(b, s, -1)
            nb = mw // LANES
            o_a = flash_attention(qkv, 0, nb, 2 * nb, nb, strips, 0, "pair", moba=True)
            o_b = flash_attention(qkv, 3 * nb, 4 * nb, 5 * nb, DIFF_HEADS, strips, MOBA_HEADS,
                                  "diff", lam_params=diff_lambda[li], subln_g=diff_subln[li],
                                  lam_init=lam_init)
            wo = even_w_out[li].astype(BF16)
        else:
            w = odd_w_in[li]
            dw = DIL_HEADS * HEAD_DIM
            lat0 = 3 * dw
            w_dil = jnp.concatenate([w[:, :dw] * qscale, w[:, dw:lat0]], axis=1).astype(BF16)
            kr_cols = w[:, lat0 + MLA_Q_RANK + MLA_KV_RANK:]
            w_lat = jnp.concatenate([w[:, lat0:], _swap_halves(kr_cols)], axis=1).astype(BF16)
            qkv = norm_matmul(h, odd_norm1[li], w_dil, BF16).reshape(b, s, -1)
            latent = norm_matmul(h, odd_norm1[li], w_lat, F32)
            nb = dw // LANES
            o_a = flash_attention(qkv, 0, nb, 2 * nb, nb, strips, MOBA_HEADS + DIFF_HEADS, "pair")
            wa, wb = _mla_q_weights(mla_w_uq[li])
            q_m, kv_m, kr_m = mla_up(latent, mla_q_norm[li], mla_kv_norm[li], wa, wb,
                                     mla_w_ukv[li].astype(BF16), rope_tabs, s)
            o_b = mla_attention(q_m.reshape(b, s, -1), kv_m.reshape(b, s, -1),
                                kr_m.reshape(b, s, -1))
            wo = odd_w_out[li].astype(BF16)
        wa_w = o_a.shape[-1]
        h = proj_residual([o_a.reshape(m, -1), o_b.reshape(m, -1)],
                          [wo[:wa_w], wo[wa_w:]], h)
        act = ffn_in(h.reshape(b, s, d), ffn_norm[layer], ffn_w_in[layer].astype(BF16),
                     ffn_conv_w[layer], ffn_conv_b[layer])
        h = proj_residual([act.reshape(m, D_FF)], [ffn_w_out[layer].astype(BF16)], h,
                          final_g=final_norm if layer == DEPTH - 1 else None)
    return h.reshape(b, s, d)
```

```python
import functools
import math

import numpy as np
import jax
import jax.numpy as jnp
from jax import lax
from jax.experimental import pallas as pl
from jax.experimental.pallas import tpu as pltpu

D_MODEL = 1024
DEPTH = 4
HEAD_DIM = 64
MOBA_HEADS = 8
MOBA_BLOCK = 256
MOBA_TOPK = 3
DIFF_HEADS = 4
DIL_HEADS = 8
DIL_CONFIGS = ((128, 1), (512, 4), (2048, 16))
MLA_HEADS = 4
MLA_Q_RANK = 256
MLA_KV_RANK = 128
MLA_NOPE_DIM = 128
MLA_ROPE_DIM = 64
MLA_V_DIM = 128
ROPE_THETA = 10000.0
REL_BUCKETS = 32
REL_MAX_DIST = 1024
N_BIAS_HEADS = MOBA_HEADS + DIFF_HEADS + DIL_HEADS
D_FF = 2816
EPS = 1e-6
NEG = -1e30

LANES = 128
TQ = 256
TK = 256
FF_CHUNK = 256
VMEM_LIMIT = 56 * 1024 * 1024

F32 = jnp.float32
BF16 = jnp.bfloat16


def _params(n_axes, vmem=None):
    return pltpu.CompilerParams(dimension_semantics=("arbitrary",) * n_axes,
                                vmem_limit_bytes=vmem)


def _rms(x, g):
    ms = jnp.mean(x * x, axis=-1, keepdims=True)
    return x * lax.rsqrt(ms + EPS) * g


def _norm_matmul_kernel(x_ref, g_ref, w_ref, o_ref, xn_ref):
    @pl.when(pl.program_id(1) == 0)
    def _():
        xn_ref[...] = _rms(x_ref[...], g_ref[...]).astype(BF16)

    o_ref[...] = jnp.dot(xn_ref[...], w_ref[...],
                         preferred_element_type=F32).astype(o_ref.dtype)


def norm_matmul(x, g, w, out_dtype, tm=1024, tn=512):
    m, d = x.shape
    n = w.shape[1]
    tn = min(tn, n)
    return pl.pallas_call(
        _norm_matmul_kernel,
        grid=(m // tm, n // tn),
        in_specs=[pl.BlockSpec((tm, d), lambda i, j: (i, 0)),
                  pl.BlockSpec((1, d), lambda i, j: (0, 0)),
                  pl.BlockSpec((d, tn), lambda i, j: (0, j))],
        out_specs=pl.BlockSpec((tm, tn), lambda i, j: (i, j)),
        out_shape=jax.ShapeDtypeStruct((m, n), out_dtype),
        scratch_shapes=[pltpu.VMEM((tm, d), BF16)],
        compiler_params=_params(2, VMEM_LIMIT),
        name="norm_matmul",
    )(x, g.reshape(1, d), w)


def _proj_res_kernel(*refs, n_in, final_norm):
    a_refs = refs[:n_in]
    w_refs = refs[n_in:2 * n_in]
    res_ref = refs[2 * n_in]
    o_ref = refs[-1]
    acc = res_ref[...]
    for a_ref, w_ref in zip(a_refs, w_refs):
        acc = acc + jnp.dot(a_ref[...], w_ref[...], preferred_element_type=F32)
    if final_norm:
        acc = _rms(acc, refs[2 * n_in + 1][...])
    o_ref[...] = acc


def proj_residual(acts, ws, res, final_g=None, tm=512):
    m, d = res.shape
    n_in = len(acts)
    in_specs = [pl.BlockSpec((tm, a.shape[1]), lambda i: (i, 0)) for a in acts]
    in_specs += [pl.BlockSpec(w.shape, lambda i: (0, 0)) for w in ws]
    in_specs += [pl.BlockSpec((tm, d), lambda i: (i, 0))]
    args = list(acts) + list(ws) + [res]
    if final_g is not None:
        in_specs += [pl.BlockSpec((1, d), lambda i: (0, 0))]
        args += [final_g.reshape(1, d)]
    return pl.pallas_call(
        functools.partial(_proj_res_kernel, n_in=n_in, final_norm=final_g is not None),
        grid=(m // tm,),
        in_specs=in_specs,
        out_specs=pl.BlockSpec((tm, d), lambda i: (i, 0)),
        out_shape=jax.ShapeDtypeStruct((m, d), F32),
        compiler_params=_params(1, VMEM_LIMIT),
        name="proj_residual",
    )(*args)


def _ffn_in_kernel(x_ref, g_ref, wu_ref, wg_ref, cw_ref, cb_ref, o_ref, xn_ref):
    @pl.when(pl.program_id(1) == 0)
    def _():
        xn_ref[...] = _rms(x_ref[0], g_ref[...]).astype(BF16)

    xn = xn_ref[...]
    u = jnp.dot(xn, wu_ref[...], preferred_element_type=F32)
    gt = jnp.dot(xn, wg_ref[...], preferred_element_type=F32)
    row = lax.broadcasted_iota(jnp.int32, gt.shape, 0)
    g1 = jnp.where(row >= 1, pltpu.roll(gt, 1, 0), 0.0)
    g2 = jnp.where(row >= 2, pltpu.roll(gt, 2, 0), 0.0)
    cw = cw_ref[...]
    z = cw[2:3] * gt + cw[1:2] * g1 + cw[0:1] * g2 + cb_ref[...]
    gelu = 0.5 * z * (1.0 + lax.erf(z * math.sqrt(0.5)))
    o_ref[0] = (gelu * u).astype(BF16)


def ffn_in(h3, g, w_in, conv_w, conv_b):
    b, s, d = h3.shape
    nc = D_FF // FF_CHUNK
    return pl.pallas_call(
        _ffn_in_kernel,
        grid=(b, nc),
        in_specs=[pl.BlockSpec((1, s, d), lambda i, c: (i, 0, 0)),
                  pl.BlockSpec((1, d), lambda i, c: (0, 0)),
                  pl.BlockSpec((d, FF_CHUNK), lambda i, c: (0, c)),
                  pl.BlockSpec((d, FF_CHUNK), lambda i, c: (0, c + nc)),
                  pl.BlockSpec((3, FF_CHUNK), lambda i, c: (0, c)),
                  pl.BlockSpec((1, FF_CHUNK), lambda i, c: (0, c))],
        out_specs=pl.BlockSpec((1, s, FF_CHUNK), lambda i, c: (i, 0, c)),
        out_shape=jax.ShapeDtypeStruct((b, s, D_FF), BF16),
        scratch_shapes=[pltpu.VMEM((s, d), BF16)],
        compiler_params=_params(2, VMEM_LIMIT),
        name="ffn_in",
    )(h3, g.reshape(1, d), w_in, w_in, conv_w, conv_b.reshape(1, D_FF))


def _bucket_of_distance(s):
    max_exact = REL_BUCKETS // 2
    n_large = REL_BUCKETS - max_exact
    thresholds = []
    for k in range(1, n_large):
        t = max_exact * (REL_MAX_DIST / max_exact) ** (k / n_large)
        ti = int(round(t))
        thresholds.append(ti if abs(t - ti) < 1e-9 else int(math.ceil(t)))
    d = np.arange(s)
    large = max_exact + sum((d >= t).astype(np.int64) for t in thresholds)
    return np.where(d < max_exact, d, np.minimum(large, REL_BUCKETS - 1)).astype(np.int32)


def _dilated_log_multiplicity(s):
    d = np.arange(s)
    count = np.zeros(s, np.int64)
    for window, dil in DIL_CONFIGS:
        count += ((d % dil == 0) & (d // dil <= window // dil)).astype(np.int64)
    return np.where(count > 0, np.log(np.maximum(count, 1)), NEG).astype(np.float32)


def _bias_strip_kernel(w_ref, o_ref, *, s):
    tile = jnp.broadcast_to(w_ref[0], (TQ, s + TQ))
    tile = pltpu.roll(tile, 0, 1, stride=1, stride_axis=0)
    o_ref[0] = tile[:, TQ:]


def bias_strips(rel_bias, s):
    bucket = _bucket_of_distance(s)
    per_dist = jnp.take(rel_bias.T.astype(F32), jnp.asarray(bucket), axis=1)
    logmult = jnp.asarray(_dilated_log_multiplicity(s))
    dil0 = MOBA_HEADS + DIFF_HEADS
    covered = logmult > 0.5 * NEG
    dil_rows = jnp.where(covered[None, :], per_dist[dil0:] + logmult[None, :], NEG)
    per_dist = jnp.concatenate([per_dist[:dil0], dil_rows], axis=0)
    nh = per_dist.shape[0]
    w = jnp.concatenate([jnp.full((nh, 1), NEG, F32), per_dist[:, ::-1],
                         jnp.full((nh, TQ - 1), NEG, F32)], axis=1).reshape(nh, 1, s + TQ)
    return pl.pallas_call(
        functools.partial(_bias_strip_kernel, s=s),
        grid=(nh,),
        in_specs=[pl.BlockSpec((1, 1, s + TQ), lambda h: (h, 0, 0))],
        out_specs=pl.BlockSpec((1, TQ, s), lambda h: (h, 0, 0)),
        out_shape=jax.ShapeDtypeStruct((nh, TQ, s), F32),
        compiler_params=_params(1),
        name="bias_strips",
    )(w)


def _attn_kernel(*refs, mode, moba, lam_init, s):
    if mode == "diff":
        q_ref, k_ref, v_ref, b_ref, lam_ref, sg_ref, o_ref = refs
    elif moba:
        q_ref, k_ref, v_ref, b_ref, o_ref, kaug_ref, kmp_ref = refs
    else:
        q_ref, k_ref, v_ref, b_ref, o_ref = refs
    nblk = s // TK
    lane = lax.broadcasted_iota(jnp.int32, (TQ, LANES), 1)
    halves = (lane < HEAD_DIM, lane >= HEAD_DIM)
    nt = (((1,), (1,)), ((), ()))

    if moba:
        k_all = k_ref[0]
        rowblk = lax.broadcasted_iota(jnp.int32, (s, LANES), 0) // TK
        lane_s = lax.broadcasted_iota(jnp.int32, (s, LANES), 1)
        kmean = jnp.mean(k_all.astype(F32).reshape(nblk, TK, LANES), axis=1)
        for c in range(2):
            base = HEAD_DIM if c == 0 else 0
            in_half = (lane_s < HEAD_DIM) if c == 0 else (lane_s >= HEAD_DIM)
            onehot = jnp.where(lane_s - base == rowblk, 1.0, 0.0).astype(BF16)
            kaug_ref[c] = jnp.where(in_half, k_all, onehot)
            kmp_ref[c] = jnp.zeros((LANES, LANES), F32)
            kmp_ref[c, base:base + nblk, :] = kmean

    if mode == "diff":
        lp = lam_ref[...]
        lam = (jnp.exp(jnp.sum(lp[0:1] * lp[1:2], axis=1, keepdims=True))
               - jnp.exp(jnp.sum(lp[2:3] * lp[3:4], axis=1, keepdims=True)) + lam_init)

    for i in range(nblk):
        nk = (i + 1) * TK
        q = q_ref[0, i * TQ:(i + 1) * TQ, :]
        outs = []
        for c in range(2):
            qc = jnp.where(halves[c], q, jnp.zeros_like(q))
            if moba and i > 0:
                base = HEAD_DIM if c == 0 else 0
                gate = lax.dot_general(qc.astype(F32), kmp_ref[c], nt,
                                       precision=lax.Precision.HIGHEST,
                                       preferred_element_type=F32)
                blk_of_lane = lane - base
                rank = jnp.zeros((TQ, LANES), jnp.int32)
                for n in range(i):
                    col = gate[:, base + n:base + n + 1]
                    wins_tie = jnp.where(n < blk_of_lane, 1, 0)
                    rank = rank + jnp.where(col > gate, 1, jnp.where(col == gate, wins_tie, 0))
                in_past = jnp.where(blk_of_lane >= 0, jnp.where(blk_of_lane < i, 1, 0), 0)
                drop = jnp.where(rank >= MOBA_TOPK, in_past, 0)
                qc = jnp.where(halves[c], q, jnp.where(drop > 0, NEG, 0.0).astype(BF16))
            kc = kaug_ref[c, 0:nk, :] if moba else k_ref[0, 0:nk, :]
            sc = lax.dot_general(qc, kc, nt, preferred_element_type=F32)
            sc = sc + b_ref[c if mode == "pair" else 0, :, s - nk:]
            m = jnp.max(sc, axis=1, keepdims=True)
            p = jnp.exp(sc - m)
            l = jnp.sum(p, axis=1, keepdims=True)
            o = jnp.dot(p.astype(BF16), v_ref[0, 0:nk, :], preferred_element_type=F32)
            outs.append(o / l)
        if mode == "pair":
            res = jnp.where(halves[0], outs[0], outs[1])
        else:
            res = _rms(outs[0] - lam * outs[1], sg_ref[...]) * (1.0 - lam_init)
        o_ref[0, i * TQ:(i + 1) * TQ, :] = res.astype(o_ref.dtype)


def attention(qkv, col_q, col_k, col_v, n_blocks, strips, head0, mode, moba=False,
              lam_params=None, subln_g=None, lam_init=0.0):
    b, s, _ = qkv.shape
    nb = 2 if mode == "pair" else 1
    seq = lambda col: pl.BlockSpec((1, s, LANES), lambda p, bb: (bb, 0, col + p))
    in_specs = [seq(col_q), seq(col_k), seq(col_v),
                pl.BlockSpec((nb, TQ, s), lambda p, bb: (head0 // nb + p, 0, 0))]
    args = [qkv, qkv, qkv, strips]
    scratch = []
    if mode == "diff":
        in_specs += [pl.BlockSpec((4, HEAD_DIM), lambda p, bb: (0, 0)),
                     pl.BlockSpec((1, LANES), lambda p, bb: (0, 0))]
        args += [lam_params.astype(F32), subln_g.reshape(1, LANES)]
    if moba:
        scratch = [pltpu.VMEM((2, s, LANES), BF16), pltpu.VMEM((2, LANES, LANES), F32)]
    return pl.pallas_call(
        functools.partial(_attn_kernel, mode=mode, moba=moba, lam_init=lam_init, s=s),
        grid=(n_blocks, b),
        in_specs=in_specs,
        out_specs=pl.BlockSpec((1, s, LANES), lambda p, bb: (bb, 0, p)),
        out_shape=jax.ShapeDtypeStruct((b, s, n_blocks * LANES), BF16),
        scratch_shapes=scratch,
        compiler_params=_params(2, VMEM_LIMIT),
        name="attn_" + mode + ("_moba" if moba else ""),
    )(*args)


def _mla_up_kernel(lat_ref, qg_ref, kvg_ref, wa_ref, wb_ref, wk_ref, wv_ref, cq_ref, sq_ref,
                   ck_ref, sk_ref, q_ref, k_ref, v_ref):
    lat = lat_ref[...]
    cqn = _rms(lat[:, :MLA_Q_RANK], qg_ref[...]).astype(BF16)
    ckvn = _rms(lat[:, MLA_Q_RANK:MLA_Q_RANK + MLA_KV_RANK], kvg_ref[...]).astype(BF16)
    qa = jnp.dot(cqn, wa_ref[...], preferred_element_type=F32)
    qb = jnp.dot(cqn, wb_ref[...], preferred_element_type=F32)
    kn = jnp.dot(ckvn, wk_ref[...], preferred_element_type=F32).astype(BF16)
    v_ref[...] = jnp.dot(ckvn, wv_ref[...], preferred_element_type=F32).astype(BF16)
    x = lat[:, MLA_Q_RANK + MLA_KV_RANK:]
    kr = (x * ck_ref[...] + pltpu.roll(x, MLA_ROPE_DIM, 1) * sk_ref[...]).astype(BF16)
    cq = cq_ref[...]
    sq = sq_ref[...]
    hw = 2 * LANES
    for h in range(MLA_HEADS):
        q_ref[:, h * hw:(h + 1) * hw] = (qa[:, h * hw:(h + 1) * hw] * cq
                                         + qb[:, h * hw:(h + 1) * hw] * sq).astype(BF16)
        k_ref[:, h * hw:h * hw + LANES] = kn[:, h * LANES:(h + 1) * LANES]
        k_ref[:, h * hw + LANES:(h + 1) * hw] = kr


def mla_up(latent, q_norm, kv_norm, wa, wb, wk, wv, rope_tabs, s, tm=1024):
    m = latent.shape[0]
    cq, sq, ck, sk = rope_tabs
    per_seq = s // tm
    tab = lambda w: pl.BlockSpec((tm, w), lambda i: (i % per_seq, 0))
    full = lambda a: pl.BlockSpec(a.shape, lambda i: (0, 0))
    hw = 2 * LANES
    rows = lambda w: pl.BlockSpec((tm, w), lambda i: (i, 0))
    return pl.pallas_call(
        _mla_up_kernel,
        grid=(m // tm,),
        in_specs=[rows(latent.shape[1]),
                  pl.BlockSpec((1, MLA_Q_RANK), lambda i: (0, 0)),
                  pl.BlockSpec((1, MLA_KV_RANK), lambda i: (0, 0)),
                  full(wa), full(wb), full(wk), full(wv), tab(hw), tab(hw), tab(LANES), tab(LANES)],
        out_specs=[rows(MLA_HEADS * hw), rows(MLA_HEADS * hw), rows(MLA_HEADS * LANES)],
        out_shape=[jax.ShapeDtypeStruct((m, MLA_HEADS * hw), BF16),
                   jax.ShapeDtypeStruct((m, MLA_HEADS * hw), BF16),
                   jax.ShapeDtypeStruct((m, MLA_HEADS * LANES), BF16)],
        compiler_params=_params(1, VMEM_LIMIT),
        name="mla_up",
    )(latent, q_norm.reshape(1, -1), kv_norm.reshape(1, -1), wa, wb, wk, wv, cq, sq, ck, sk)


def _mla_attn_kernel(q_ref, k_ref, v_ref, o_ref, *, s):
    scale = (MLA_NOPE_DIM + MLA_ROPE_DIM) ** -0.5
    nt = (((1,), (1,)), ((), ()))
    row = lax.broadcasted_iota(jnp.int32, (TQ, TK), 0)
    col = lax.broadcasted_iota(jnp.int32, (TQ, TK), 1)
    causal = row >= col
    for i in range(s // TQ):
        nk = (i + 1) * TK
        q = q_ref[0, i * TQ:(i + 1) * TQ, :]
        sc = lax.dot_general(q, k_ref[0, 0:nk, :], nt, preferred_element_type=F32) * scale
        diag = jnp.where(causal, sc[:, nk - TK:], NEG)
        sc = diag if i == 0 else jnp.concatenate([sc[:, :nk - TK], diag], axis=1)
        m = jnp.max(sc, axis=1, keepdims=True)
        p = jnp.exp(sc - m)
        l = jnp.sum(p, axis=1, keepdims=True)
        o = jnp.dot(p.astype(BF16), v_ref[0, 0:nk, :], preferred_element_type=F32)
        o_ref[0, i * TQ:(i + 1) * TQ, :] = (o / l).astype(o_ref.dtype)


def mla_attention(q, k, v):
    b, s, _ = q.shape
    hw = 2 * LANES
    return pl.pallas_call(
        functools.partial(_mla_attn_kernel, s=s),
        grid=(MLA_HEADS, b),
        in_specs=[pl.BlockSpec((1, s, hw), lambda h, bb: (bb, 0, h)),
                  pl.BlockSpec((1, s, hw), lambda h, bb: (bb, 0, h)),
                  pl.BlockSpec((1, s, LANES), lambda h, bb: (bb, 0, h))],
        out_specs=pl.BlockSpec((1, s, LANES), lambda h, bb: (bb, 0, h)),
        out_shape=jax.ShapeDtypeStruct((b, s, MLA_HEADS * LANES), BF16),
        compiler_params=_params(2, VMEM_LIMIT),
        name="mla_attn",
    )(q, k, v)


def _rope_tables(s):
    half = MLA_ROPE_DIM // 2
    freq = ROPE_THETA ** (-jnp.arange(half, dtype=F32) / half)
    ang = jnp.arange(s, dtype=F32)[:, None] * freq[None, :]
    cos, sin = jnp.cos(ang), jnp.sin(ang)
    cos2 = jnp.concatenate([cos, cos], axis=1)
    sin2 = jnp.concatenate([-sin, sin], axis=1)
    z64 = jnp.zeros((s, MLA_ROPE_DIM), F32)
    cq = jnp.concatenate([jnp.ones((s, MLA_NOPE_DIM), F32), cos2, z64], axis=1)
    sq = jnp.concatenate([jnp.zeros((s, MLA_NOPE_DIM), F32), sin2, z64], axis=1)
    ck = jnp.concatenate([cos2, z64], axis=1)
    sk = jnp.concatenate([sin2, z64], axis=1)
    return cq, sq, ck, sk


def _swap_halves(w):
    half = w.shape[-1] // 2
    return jnp.concatenate([w[..., half:], w[..., :half]], axis=-1)


def _mla_q_weights(w_uq):
    r = w_uq.shape[0]
    w = w_uq.reshape(r, MLA_HEADS, MLA_NOPE_DIM + MLA_ROPE_DIM)
    nope, rope = w[..., :MLA_NOPE_DIM], w[..., MLA_NOPE_DIM:]
    z64 = jnp.zeros((r, MLA_HEADS, MLA_ROPE_DIM), w.dtype)
    z128 = jnp.zeros((r, MLA_HEADS, MLA_NOPE_DIM), w.dtype)
    wa = jnp.concatenate([nope, rope, z64], axis=-1).reshape(r, -1)
    wb = jnp.concatenate([z128, _swap_halves(rope), z64], axis=-1).reshape(r, -1)
    return wa.astype(BF16), wb.astype(BF16)


def kernel(x, rel_bias, even_norm1, even_w_in, diff_lambda, diff_subln, even_w_out,
           odd_norm1, odd_w_in, mla_q_norm, mla_w_uq, mla_kv_norm, mla_w_ukv, odd_w_out,
           ffn_norm, ffn_w_in, ffn_conv_w, ffn_conv_b, ffn_w_out, final_norm):
    b, s, d = x.shape
    m = b * s
    strips = bias_strips(rel_bias, s)
    rope_tabs = _rope_tables(s)
    qscale = HEAD_DIM ** -0.5
    h = x.reshape(m, d)
    for layer in range(DEPTH):
        li = layer // 2
        if layer % 2 == 0:
            lam_init = 0.8 - 0.6 * math.exp(-0.3 * layer)
            w = even_w_in[li]
            mw, dw = MOBA_HEADS * HEAD_DIM, DIFF_HEADS * 2 * HEAD_DIM
            w = jnp.concatenate([w[:, :mw] * qscale, w[:, mw:3 * mw],
                                 w[:, 3 * mw:3 * mw + dw] * qscale, w[:, 3 * mw + dw:]],
                                axis=1).astype(BF16)
            qkv = norm_matmul(h, even_norm1[li], w, BF16).reshape(b, s, -1)
            nb = mw // LANES
            o_a = attention(qkv, 0, nb, 2 * nb, nb, strips, 0, "pair", moba=True)
            o_b = attention(qkv, 3 * nb, 4 * nb, 5 * nb, DIFF_HEADS, strips, MOBA_HEADS, "diff",
                            lam_params=diff_lambda[li], subln_g=diff_subln[li], lam_init=lam_init)
            wo = even_w_out[li].astype(BF16)
        else:
            w = odd_w_in[li]
            dw = DIL_HEADS * HEAD_DIM
            lat0 = 3 * dw
            w_dil = jnp.concatenate([w[:, :dw] * qscale, w[:, dw:lat0]], axis=1).astype(BF16)
            kr_cols = w[:, lat0 + MLA_Q_RANK + MLA_KV_RANK:]
            w_lat = jnp.concatenate([w[:, lat0:], _swap_halves(kr_cols)], axis=1).astype(BF16)
            qkv = norm_matmul(h, odd_norm1[li], w_dil, BF16).reshape(b, s, -1)
            latent = norm_matmul(h, odd_norm1[li], w_lat, F32)
            nb = dw // LANES
            o_a = attention(qkv, 0, nb, 2 * nb, nb, strips, MOBA_HEADS + DIFF_HEADS, "pair")
            wa, wb = _mla_q_weights(mla_w_uq[li])
            wkv = mla_w_ukv[li].reshape(MLA_KV_RANK, MLA_HEADS, MLA_NOPE_DIM + MLA_V_DIM)
            wk = wkv[..., :MLA_NOPE_DIM].reshape(MLA_KV_RANK, -1).astype(BF16)
            wv = wkv[..., MLA_NOPE_DIM:].reshape(MLA_KV_RANK, -1).astype(BF16)
            q_m, k_m, v_m = mla_up(latent, mla_q_norm[li], mla_kv_norm[li], wa, wb, wk, wv,
                                   rope_tabs, s)
            o_b = mla_attention(q_m.reshape(b, s, -1), k_m.reshape(b, s, -1),
                                v_m.reshape(b, s, -1))
            wo = odd_w_out[li].astype(BF16)
        wa_w = o_a.shape[-1]
        h = proj_residual([o_a.reshape(m, -1), o_b.reshape(m, -1)],
                          [wo[:wa_w], wo[wa_w:]], h)
        act = ffn_in(h.reshape(b, s, d), ffn_norm[layer], ffn_w_in[layer].astype(BF16),
                     ffn_conv_w[layer], ffn_conv_b[layer])
        h = proj_residual([act.reshape(m, D_FF)], [ffn_w_out[layer].astype(BF16)], h,
                          final_g=final_norm if layer == DEPTH - 1 else None)
    return h.reshape(b, s, d)
```

```python
import functools
import math

import numpy as np
import jax
import jax.numpy as jnp
from jax import lax
from jax.experimental import pallas as pl
from jax.experimental.pallas import tpu as pltpu

D_MODEL = 1024
DEPTH = 4
HEAD_DIM = 64
MOBA_HEADS = 8
MOBA_BLOCK = 256
MOBA_TOPK = 3
DIFF_HEADS = 4
DIL_HEADS = 8
DIL_CONFIGS = ((128, 1), (512, 4), (2048, 16))
MLA_HEADS = 4
MLA_Q_RANK = 256
MLA_KV_RANK = 128
MLA_NOPE_DIM = 128
MLA_ROPE_DIM = 64
MLA_V_DIM = 128
ROPE_THETA = 10000.0
REL_BUCKETS = 32
REL_MAX_DIST = 1024
N_BIAS_HEADS = MOBA_HEADS + DIFF_HEADS + DIL_HEADS
D_FF = 2816
EPS = 1e-6
NEG = -1e30

LANES = 128
TQ = 256
TK = 256
FF_CHUNK = 256
FF_ROWS = 512
VMEM_LIMIT = 56 * 1024 * 1024

F32 = jnp.float32
BF16 = jnp.bfloat16


def _params(n_axes, vmem=None):
    return pltpu.CompilerParams(dimension_semantics=("arbitrary",) * n_axes,
                                vmem_limit_bytes=vmem)


def _rms(x, g):
    ms = jnp.mean(x * x, axis=-1, keepdims=True)
    return x * lax.rsqrt(ms + EPS) * g


def _in_proj_kernel(x_ref, g_ref, w_ref, *o_refs, row_chunk):
    tm = x_ref.shape[0]
    g = g_ref[...]
    for r in range(tm // row_chunk):
        rows = slice(r * row_chunk, (r + 1) * row_chunk)
        xn = _rms(x_ref[rows, :], g).astype(BF16)
        acc = jnp.dot(xn, w_ref[...], preferred_element_type=F32)
        col = 0
        for o_ref in o_refs:
            width = o_ref.shape[1]
            o_ref[rows, :] = acc[:, col:col + width].astype(o_ref.dtype)
            col += width


def in_proj(x, g, w, out_widths, out_dtypes, tm=1024, row_chunk=512):
    m, d = x.shape
    return pl.pallas_call(
        functools.partial(_in_proj_kernel, row_chunk=row_chunk),
        grid=(m // tm,),
        in_specs=[pl.BlockSpec((tm, d), lambda i: (i, 0)),
                  pl.BlockSpec((1, d), lambda i: (0, 0)),
                  pl.BlockSpec(w.shape, lambda i: (0, 0))],
        out_specs=[pl.BlockSpec((tm, n), lambda i: (i, 0)) for n in out_widths],
        out_shape=[jax.ShapeDtypeStruct((m, n), dt) for n, dt in zip(out_widths, out_dtypes)],
        compiler_params=_params(1, VMEM_LIMIT),
        name="in_proj",
    )(x, g.reshape(1, d), w)


def _proj_res_kernel(*refs, n_in, final_norm):
    a_refs = refs[:n_in]
    w_refs = refs[n_in:2 * n_in]
    res_ref = refs[2 * n_in]
    o_ref = refs[-1]
    acc = res_ref[...]
    for a_ref, w_ref in zip(a_refs, w_refs):
        acc = acc + jnp.dot(a_ref[...], w_ref[...], preferred_element_type=F32)
    if final_norm:
        acc = _rms(acc, refs[2 * n_in + 1][...])
    o_ref[...] = acc


def proj_residual(acts, ws, res, final_g=None, tm=512):
    m, d = res.shape
    n_in = len(acts)
    in_specs = [pl.BlockSpec((tm, a.shape[1]), lambda i: (i, 0)) for a in acts]
    in_specs += [pl.BlockSpec(w.shape, lambda i: (0, 0)) for w in ws]
    in_specs += [pl.BlockSpec((tm, d), lambda i: (i, 0))]
    args = list(acts) + list(ws) + [res]
    if final_g is not None:
        in_specs += [pl.BlockSpec((1, d), lambda i: (0, 0))]
        args += [final_g.reshape(1, d)]
    return pl.pallas_call(
        functools.partial(_proj_res_kernel, n_in=n_in, final_norm=final_g is not None),
        grid=(m // tm,),
        in_specs=in_specs,
        out_specs=pl.BlockSpec((tm, d), lambda i: (i, 0)),
        out_shape=jax.ShapeDtypeStruct((m, d), F32),
        compiler_params=_params(1, VMEM_LIMIT),
        name="proj_residual",
    )(*args)


def _ffn_in_kernel(x_ref, g_ref, wu_ref, wg_ref, cw_ref, cb_ref, o_ref, xn_ref):
    @pl.when(pl.program_id(1) == 0)
    def _():
        xn_ref[...] = _rms(x_ref[0], g_ref[...]).astype(BF16)

    s = xn_ref.shape[0]
    cw = cw_ref[...]
    cb = cb_ref[...]
    sub = lax.broadcasted_iota(jnp.int32, (8, FF_CHUNK), 0)
    prev1 = prev2 = jnp.zeros((8, FF_CHUNK), F32)
    for r in range(s // FF_ROWS):
        xn = xn_ref[r * FF_ROWS:(r + 1) * FF_ROWS, :]
        u = jnp.dot(xn, wu_ref[...], preferred_element_type=F32)
        gt = jnp.dot(xn, wg_ref[...], preferred_element_type=F32)
        r1 = pltpu.roll(gt, 1, 0)
        r2 = pltpu.roll(gt, 2, 0)
        g1 = jnp.concatenate([jnp.where(sub >= 1, r1[:8], prev1), r1[8:]], axis=0)
        g2 = jnp.concatenate([jnp.where(sub >= 2, r2[:8], prev2), r2[8:]], axis=0)
        prev1, prev2 = r1[:8], r2[:8]
        z = cw[2:3] * gt + cw[1:2] * g1 + cw[0:1] * g2 + cb
        gelu = 0.5 * z * (1.0 + lax.erf(z * math.sqrt(0.5)))
        o_ref[0, r * FF_ROWS:(r + 1) * FF_ROWS, :] = (gelu * u).astype(BF16)


def ffn_in(h3, g, w_in, conv_w, conv_b):
    b, s, d = h3.shape
    nc = D_FF // FF_CHUNK
    return pl.pallas_call(
        _ffn_in_kernel,
        grid=(b, nc),
        in_specs=[pl.BlockSpec((1, s, d), lambda i, c: (i, 0, 0)),
                  pl.BlockSpec((1, d), lambda i, c: (0, 0)),
                  pl.BlockSpec((d, FF_CHUNK), lambda i, c: (0, c)),
                  pl.BlockSpec((d, FF_CHUNK), lambda i, c: (0, c + nc)),
                  pl.BlockSpec((3, FF_CHUNK), lambda i, c: (0, c)),
                  pl.BlockSpec((1, FF_CHUNK), lambda i, c: (0, c))],
        out_specs=pl.BlockSpec((1, s, FF_CHUNK), lambda i, c: (i, 0, c)),
        out_shape=jax.ShapeDtypeStruct((b, s, D_FF), BF16),
        scratch_shapes=[pltpu.VMEM((s, d), BF16)],
        compiler_params=_params(2, VMEM_LIMIT),
        name="ffn_in",
    )(h3, g.reshape(1, d), w_in, w_in, conv_w, conv_b.reshape(1, D_FF))


def _bucket_of_distance(s):
    max_exact = REL_BUCKETS // 2
    n_large = REL_BUCKETS - max_exact
    thresholds = []
    for k in range(1, n_large):
        t = max_exact * (REL_MAX_DIST / max_exact) ** (k / n_large)
        ti = int(round(t))
        thresholds.append(ti if abs(t - ti) < 1e-9 else int(math.ceil(t)))
    d = np.arange(s)
    large = max_exact + sum((d >= t).astype(np.int64) for t in thresholds)
    return np.where(d < max_exact, d, np.minimum(large, REL_BUCKETS - 1)).astype(np.int32)


def _dilated_log_multiplicity(s):
    d = np.arange(s)
    count = np.zeros(s, np.int64)
    for window, dil in DIL_CONFIGS:
        count += ((d % dil == 0) & (d // dil <= window // dil)).astype(np.int64)
    return np.where(count > 0, np.log(np.maximum(count, 1)), NEG).astype(np.float32)


def _bias_strip_kernel(w_ref, o_ref, *, s):
    tile = jnp.broadcast_to(w_ref[0], (TQ, s + TQ))
    tile = pltpu.roll(tile, 0, 1, stride=1, stride_axis=0)
    o_ref[0] = tile[:, TQ:]


def bias_strips(rel_bias, s):
    bucket = _bucket_of_distance(s)
    per_dist = jnp.take(rel_bias.T.astype(F32), jnp.asarray(bucket), axis=1)
    logmult = jnp.asarray(_dilated_log_multiplicity(s))
    dil0 = MOBA_HEADS + DIFF_HEADS
    covered = logmult > 0.5 * NEG
    dil_rows = jnp.where(covered[None, :], per_dist[dil0:] + logmult[None, :], NEG)
    per_dist = jnp.concatenate([per_dist[:dil0], dil_rows], axis=0)
    nh = per_dist.shape[0]
    w = jnp.concatenate([jnp.full((nh, 1), NEG, F32), per_dist[:, ::-1],
                         jnp.full((nh, TQ - 1), NEG, F32)], axis=1).reshape(nh, 1, s + TQ)
    return pl.pallas_call(
        functools.partial(_bias_strip_kernel, s=s),
        grid=(nh,),
        in_specs=[pl.BlockSpec((1, 1, s + TQ), lambda h: (h, 0, 0))],
        out_specs=pl.BlockSpec((1, TQ, s), lambda h: (h, 0, 0)),
        out_shape=jax.ShapeDtypeStruct((nh, TQ, s), F32),
        compiler_params=_params(1),
        name="bias_strips",
    )(w)


def _attn_kernel(*refs, mode, moba, lam_init, s):
    if mode == "diff":
        q_ref, k_ref, v_ref, b_ref, lam_ref, sg_ref, o_ref = refs
    elif moba:
        q_ref, k_ref, v_ref, b_ref, o_ref, kaug_ref, drop_ref = refs
    else:
        q_ref, k_ref, v_ref, b_ref, o_ref = refs
    nblk = s // TK
    lane = lax.broadcasted_iota(jnp.int32, (TQ, LANES), 1)
    halves = (lane < HEAD_DIM, lane >= HEAD_DIM)
    nt = (((1,), (1,)), ((), ()))

    if moba:
        k_all = k_ref[0]
        q_all = q_ref[0]
        rowblk = lax.broadcasted_iota(jnp.int32, (s, LANES), 0) // TK
        lane_s = lax.broadcasted_iota(jnp.int32, (s, LANES), 1)
        kmean = jnp.mean(k_all.astype(F32).reshape(nblk, TK, LANES), axis=1)
        k_hi = kmean.astype(BF16)
        rem = kmean - k_hi.astype(F32)
        k_mid = rem.astype(BF16)
        k_lo = (rem - k_mid.astype(F32)).astype(BF16)
        kmean3 = jnp.concatenate([k_hi, k_mid, k_lo], axis=1)
        qblk = lax.broadcasted_iota(jnp.int32, (nblk, s), 1) // TQ
        blk = lax.broadcasted_iota(jnp.int32, (nblk, s), 0)
        for c in range(2):
            base = HEAD_DIM if c == 0 else 0
            in_half = (lane_s < HEAD_DIM) if c == 0 else (lane_s >= HEAD_DIM)
            onehot = jnp.where(lane_s - base == rowblk, 1.0, 0.0).astype(BF16)
            kaug_ref[c] = jnp.where(in_half, k_all, onehot)
            qh = jnp.where(in_half, q_all, jnp.zeros_like(q_all))
            gate = lax.dot_general(kmean3, jnp.concatenate([qh, qh, qh], axis=1), nt,
                                   preferred_element_type=F32)
            rank = jnp.zeros((nblk, s), jnp.int32)
            for n in range(nblk - 1):
                gn = gate[n:n + 1, :]
                beats = jnp.where(gn > gate, 1, jnp.where(gn == gate, jnp.where(n < blk, 1, 0), 0))
                rank = rank + jnp.where(n < qblk, beats, 0)
            drop = jnp.where(blk < qblk, jnp.where(rank >= MOBA_TOPK, NEG, 0.0), 0.0)
            drop_ref[c] = jnp.zeros((LANES, s), F32)
            drop_ref[c, base:base + nblk, :] = drop

    if mode == "diff":
        lp = lam_ref[...]
        lam = (jnp.exp(jnp.sum(lp[0:1] * lp[1:2], axis=1, keepdims=True))
               - jnp.exp(jnp.sum(lp[2:3] * lp[3:4], axis=1, keepdims=True)) + lam_init)

    for i in range(nblk):
        nk = (i + 1) * TK
        q = q_ref[0, i * TQ:(i + 1) * TQ, :]
        outs = []
        for c in range(2):
            qc = jnp.where(halves[c], q, jnp.zeros_like(q))
            if moba and i > 0:
                sel = drop_ref[c, :, i * TQ:(i + 1) * TQ].T
                qc = jnp.where(halves[c], q, sel.astype(BF16))
            kc = kaug_ref[c, 0:nk, :] if moba else k_ref[0, 0:nk, :]
            sc = lax.dot_general(qc, kc, nt, preferred_element_type=F32)
            sc = sc + b_ref[c if mode == "pair" else 0, :, s - nk:]
            m = jnp.max(sc, axis=1, keepdims=True)
            p = jnp.exp(sc - m)
            l = jnp.sum(p, axis=1, keepdims=True)
            o = jnp.dot(p.astype(BF16), v_ref[0, 0:nk, :], preferred_element_type=F32)
            outs.append(o / l)
        if mode == "pair":
            res = jnp.where(halves[0], outs[0], outs[1])
        else:
            res = _rms(outs[0] - lam * outs[1], sg_ref[...]) * (1.0 - lam_init)
        o_ref[0, i * TQ:(i + 1) * TQ, :] = res.astype(o_ref.dtype)


def attention(qkv, col_q, col_k, col_v, n_blocks, strips, head0, mode, moba=False,
              lam_params=None, subln_g=None, lam_init=0.0):
    b, s, _ = qkv.shape
    nb = 2 if mode == "pair" else 1
    seq = lambda col: pl.BlockSpec((1, s, LANES), lambda p, bb: (bb, 0, col + p))
    in_specs = [seq(col_q), seq(col_k), seq(col_v),
                pl.BlockSpec((nb, TQ, s), lambda p, bb: (head0 // nb + p, 0, 0))]
    args = [qkv, qkv, qkv, strips]
    scratch = []
    if mode == "diff":
        in_specs += [pl.BlockSpec((4, HEAD_DIM), lambda p, bb: (0, 0)),
                     pl.BlockSpec((1, LANES), lambda p, bb: (0, 0))]
        args += [lam_params.astype(F32), subln_g.reshape(1, LANES)]
    if moba:
        scratch = [pltpu.VMEM((2, s, LANES), BF16), pltpu.VMEM((2, LANES, s), F32)]
    return pl.pallas_call(
        functools.partial(_attn_kernel, mode=mode, moba=moba, lam_init=lam_init, s=s),
        grid=(n_blocks, b),
        in_specs=in_specs,
        out_specs=pl.BlockSpec((1, s, LANES), lambda p, bb: (bb, 0, p)),
        out_shape=jax.ShapeDtypeStruct((b, s, n_blocks * LANES), BF16),
        scratch_shapes=scratch,
        compiler_params=_params(2, VMEM_LIMIT),
        name="attn_" + mode + ("_moba" if moba else ""),
    )(*args)


def _mla_up_kernel(lat_ref, qg_ref, kvg_ref, wa_ref, wb_ref, wk_ref, wv_ref, cq_ref, sq_ref,
                   ck_ref, sk_ref, q_ref, k_ref, v_ref):
    lat = lat_ref[...]
    cqn = _rms(lat[:, :MLA_Q_RANK], qg_ref[...]).astype(BF16)
    ckvn = _rms(lat[:, MLA_Q_RANK:MLA_Q_RANK + MLA_KV_RANK], kvg_ref[...]).astype(BF16)
    qa = jnp.dot(cqn, wa_ref[...], preferred_element_type=F32)
    qb = jnp.dot(cqn, wb_ref[...], preferred_element_type=F32)
    kn = jnp.dot(ckvn, wk_ref[...], preferred_element_type=F32).astype(BF16)
    v_ref[...] = jnp.dot(ckvn, wv_ref[...], preferred_element_type=F32).astype(BF16)
    x = lat[:, MLA_Q_RANK + MLA_KV_RANK:]
    kr = (x * ck_ref[...] + pltpu.roll(x, MLA_ROPE_DIM, 1) * sk_ref[...]).astype(BF16)
    cq = cq_ref[...]
    sq = sq_ref[...]
    hw = 2 * LANES
    for h in range(MLA_HEADS):
        q_ref[:, h * hw:(h + 1) * hw] = (qa[:, h * hw:(h + 1) * hw] * cq
                                         + qb[:, h * hw:(h + 1) * hw] * sq).astype(BF16)
        k_ref[:, h * hw:h * hw + LANES] = kn[:, h * LANES:(h + 1) * LANES]
        k_ref[:, h * hw + LANES:(h + 1) * hw] = kr


def mla_up(latent, q_norm, kv_norm, wa, wb, wk, wv, rope_tabs, s, tm=1024):
    m = latent.shape[0]
    cq, sq, ck, sk = rope_tabs
    per_seq = s // tm
    tab = lambda w: pl.BlockSpec((tm, w), lambda i: (i % per_seq, 0))
    full = lambda a: pl.BlockSpec(a.shape, lambda i: (0, 0))
    hw = 2 * LANES
    rows = lambda w: pl.BlockSpec((tm, w), lambda i: (i, 0))
    return pl.pallas_call(
        _mla_up_kernel,
        grid=(m // tm,),
        in_specs=[rows(latent.shape[1]),
                  pl.BlockSpec((1, MLA_Q_RANK), lambda i: (0, 0)),
                  pl.BlockSpec((1, MLA_KV_RANK), lambda i: (0, 0)),
                  full(wa), full(wb), full(wk), full(wv), tab(hw), tab(hw), tab(LANES), tab(LANES)],
        out_specs=[rows(MLA_HEADS * hw), rows(MLA_HEADS * hw), rows(MLA_HEADS * LANES)],
        out_shape=[jax.ShapeDtypeStruct((m, MLA_HEADS * hw), BF16),
                   jax.ShapeDtypeStruct((m, MLA_HEADS * hw), BF16),
                   jax.ShapeDtypeStruct((m, MLA_HEADS * LANES), BF16)],
        compiler_params=_params(1, VMEM_LIMIT),
        name="mla_up",
    )(latent, q_norm.reshape(1, -1), kv_norm.reshape(1, -1), wa, wb, wk, wv, cq, sq, ck, sk)


def _mla_attn_kernel(q_ref, k_ref, v_ref, o_ref, *, s):
    scale = (MLA_NOPE_DIM + MLA_ROPE_DIM) ** -0.5
    nt = (((1,), (1,)), ((), ()))
    row = lax.broadcasted_iota(jnp.int32, (TQ, TK), 0)
    col = lax.broadcasted_iota(jnp.int32, (TQ, TK), 1)
    causal = row >= col
    for i in range(s // TQ):
        nk = (i + 1) * TK
        q = q_ref[0, i * TQ:(i + 1) * TQ, :]
        sc = lax.dot_general(q, k_ref[0, 0:nk, :], nt, preferred_element_type=F32) * scale
        diag = jnp.where(causal, sc[:, nk - TK:], NEG)
        sc = diag if i == 0 else jnp.concatenate([sc[:, :nk - TK], diag], axis=1)
        m = jnp.max(sc, axis=1, keepdims=True)
        p = jnp.exp(sc - m)
        l = jnp.sum(p, axis=1, keepdims=True)
        o = jnp.dot(p.astype(BF16), v_ref[0, 0:nk, :], preferred_element_type=F32)
        o_ref[0, i * TQ:(i + 1) * TQ, :] = (o / l).astype(o_ref.dtype)


def mla_attention(q, k, v):
    b, s, _ = q.shape
    hw = 2 * LANES
    return pl.pallas_call(
        functools.partial(_mla_attn_kernel, s=s),
        grid=(MLA_HEADS, b),
        in_specs=[pl.BlockSpec((1, s, hw), lambda h, bb: (bb, 0, h)),
                  pl.BlockSpec((1, s, hw), lambda h, bb: (bb, 0, h)),
                  pl.BlockSpec((1, s, LANES), lambda h, bb: (bb, 0, h))],
        out_specs=pl.BlockSpec((1, s, LANES), lambda h, bb: (bb, 0, h)),
        out_shape=jax.ShapeDtypeStruct((b, s, MLA_HEADS * LANES), BF16),
        compiler_params=_params(2, VMEM_LIMIT),
        name="mla_attn",
    )(q, k, v)


def _rope_tables(s):
    half = MLA_ROPE_DIM // 2
    freq = ROPE_THETA ** (-jnp.arange(half, dtype=F32) / half)
    ang = jnp.arange(s, dtype=F32)[:, None] * freq[None, :]
    cos, sin = jnp.cos(ang), jnp.sin(ang)
    cos2 = jnp.concatenate([cos, cos], axis=1)
    sin2 = jnp.concatenate([-sin, sin], axis=1)
    z64 = jnp.zeros((s, MLA_ROPE_DIM), F32)
    cq = jnp.concatenate([jnp.ones((s, MLA_NOPE_DIM), F32), cos2, z64], axis=1)
    sq = jnp.concatenate([jnp.zeros((s, MLA_NOPE_DIM), F32), sin2, z64], axis=1)
    ck = jnp.concatenate([cos2, z64], axis=1)
    sk = jnp.concatenate([sin2, z64], axis=1)
    return cq, sq, ck, sk


def _swap_halves(w):
    half = w.shape[-1] // 2
    return jnp.concatenate([w[..., half:], w[..., :half]], axis=-1)


def _mla_q_weights(w_uq):
    r = w_uq.shape[0]
    w = w_uq.reshape(r, MLA_HEADS, MLA_NOPE_DIM + MLA_ROPE_DIM)
    nope, rope = w[..., :MLA_NOPE_DIM], w[..., MLA_NOPE_DIM:]
    z64 = jnp.zeros((r, MLA_HEADS, MLA_ROPE_DIM), w.dtype)
    z128 = jnp.zeros((r, MLA_HEADS, MLA_NOPE_DIM), w.dtype)
    wa = jnp.concatenate([nope, rope, z64], axis=-1).reshape(r, -1)
    wb = jnp.concatenate([z128, _swap_halves(rope), z64], axis=-1).reshape(r, -1)
    return wa.astype(BF16), wb.astype(BF16)


def kernel(x, rel_bias, even_norm1, even_w_in, diff_lambda, diff_subln, even_w_out,
           odd_norm1, odd_w_in, mla_q_norm, mla_w_uq, mla_kv_norm, mla_w_ukv, odd_w_out,
           ffn_norm, ffn_w_in, ffn_conv_w, ffn_conv_b, ffn_w_out, final_norm):
    b, s, d = x.shape
    m = b * s
    strips = bias_strips(rel_bias, s)
    rope_tabs = _rope_tables(s)
    qscale = HEAD_DIM ** -0.5
    h = x.reshape(m, d)
    for layer in range(DEPTH):
        li = layer // 2
        if layer % 2 == 0:
            lam_init = 0.8 - 0.6 * math.exp(-0.3 * layer)
            w = even_w_in[li]
            mw, dw = MOBA_HEADS * HEAD_DIM, DIFF_HEADS * 2 * HEAD_DIM
            w = jnp.concatenate([w[:, :mw] * qscale, w[:, mw:3 * mw],
                                 w[:, 3 * mw:3 * mw + dw] * qscale, w[:, 3 * mw + dw:]],
                                axis=1).astype(BF16)
            (qkv,) = in_proj(h, even_norm1[li], w, [w.shape[1]], [BF16])
            qkv = qkv.reshape(b, s, -1)
            nb = mw // LANES
            o_a = attention(qkv, 0, nb, 2 * nb, nb, strips, 0, "pair", moba=True)
            o_b = attention(qkv, 3 * nb, 4 * nb, 5 * nb, DIFF_HEADS, strips, MOBA_HEADS, "diff",
                            lam_params=diff_lambda[li], subln_g=diff_subln[li], lam_init=lam_init)
            wo = even_w_out[li].astype(BF16)
        else:
            w = odd_w_in[li]
            dw = DIL_HEADS * HEAD_DIM
            lat0 = 3 * dw
            w_dil = jnp.concatenate([w[:, :dw] * qscale, w[:, dw:lat0]], axis=1).astype(BF16)
            kr_cols = w[:, lat0 + MLA_Q_RANK + MLA_KV_RANK:]
            w_lat = jnp.concatenate([w[:, lat0:], _swap_halves(kr_cols)], axis=1).astype(BF16)
            qkv, latent = in_proj(h, odd_norm1[li], jnp.concatenate([w_dil, w_lat], axis=1),
                                  [w_dil.shape[1], w_lat.shape[1]], [BF16, F32])
            qkv = qkv.reshape(b, s, -1)
            nb = dw // LANES
            o_a = attention(qkv, 0, nb, 2 * nb, nb, strips, MOBA_HEADS + DIFF_HEADS, "pair")
            wa, wb = _mla_q_weights(mla_w_uq[li])
            wkv = mla_w_ukv[li].reshape(MLA_KV_RANK, MLA_HEADS, MLA_NOPE_DIM + MLA_V_DIM)
            wk = wkv[..., :MLA_NOPE_DIM].reshape(MLA_KV_RANK, -1).astype(BF16)
            wv = wkv[..., MLA_NOPE_DIM:].reshape(MLA_KV_RANK, -1).astype(BF16)
            q_m, k_m, v_m = mla_up(latent, mla_q_norm[li], mla_kv_norm[li], wa, wb, wk, wv,
                                   rope_tabs, s)
            o_b = mla_attention(q_m.reshape(b, s, -1), k_m.reshape(b, s, -1),
                                v_m.reshape(b, s, -1))
            wo = odd_w_out[li].astype(BF16)
        wa_w = o_a.shape[-1]
        h = proj_residual([o_a.reshape(m, -1), o_b.reshape(m, -1)],
                          [wo[:wa_w], wo[wa_w:]], h)
        act = ffn_in(h.reshape(b, s, d), ffn_norm[layer], ffn_w_in[layer].astype(BF16),
                     ffn_conv_w[layer], ffn_conv_b[layer])
        h = proj_residual([act.reshape(m, D_FF)], [ffn_w_out[layer].astype(BF16)], h,
                          final_g=final_norm if layer == DEPTH - 1 else None)
    return h.reshape(b, s, d)
```

```python
import functools
import math

import numpy as np
import jax
import jax.numpy as jnp
from jax import lax
from jax.experimental import pallas as pl
from jax.experimental.pallas import tpu as pltpu

D_MODEL = 1024
DEPTH = 4
HEAD_DIM = 64
MOBA_HEADS = 8
MOBA_BLOCK = 256
MOBA_TOPK = 3
DIFF_HEADS = 4
DIL_HEADS = 8
DIL_CONFIGS = ((128, 1), (512, 4), (2048, 16))
MLA_HEADS = 4
MLA_Q_RANK = 256
MLA_KV_RANK = 128
MLA_NOPE_DIM = 128
MLA_ROPE_DIM = 64
MLA_V_DIM = 128
ROPE_THETA = 10000.0
REL_BUCKETS = 32
REL_MAX_DIST = 1024
N_BIAS_HEADS = MOBA_HEADS + DIFF_HEADS + DIL_HEADS
D_FF = 2816
EPS = 1e-6
NEG = -1e30

LANES = 128
TQ = 256
TK = 256
FF_CHUNK = 256
FF_ROWS = 512
VMEM_LIMIT = 56 * 1024 * 1024

F32 = jnp.float32
BF16 = jnp.bfloat16


def _params(n_axes, vmem=None):
    return pltpu.CompilerParams(dimension_semantics=("arbitrary",) * n_axes,
                                vmem_limit_bytes=vmem)


def _rms(x, g):
    ms = jnp.mean(x * x, axis=-1, keepdims=True)
    return x * lax.rsqrt(ms + EPS) * g


def _in_proj_kernel(x_ref, g_ref, w_ref, *o_refs, row_chunk):
    tm = x_ref.shape[0]
    g = g_ref[...]
    for r in range(tm // row_chunk):
        rows = slice(r * row_chunk, (r + 1) * row_chunk)
        xn = _rms(x_ref[rows, :], g).astype(BF16)
        acc = jnp.dot(xn, w_ref[...], preferred_element_type=F32)
        col = 0
        for o_ref in o_refs:
            width = o_ref.shape[1]
            o_ref[rows, :] = acc[:, col:col + width].astype(o_ref.dtype)
            col += width


def in_proj(x, g, w, out_widths, out_dtypes, tm=1024, row_chunk=512):
    m, d = x.shape
    return pl.pallas_call(
        functools.partial(_in_proj_kernel, row_chunk=row_chunk),
        grid=(m // tm,),
        in_specs=[pl.BlockSpec((tm, d), lambda i: (i, 0)),
                  pl.BlockSpec((1, d), lambda i: (0, 0)),
                  pl.BlockSpec(w.shape, lambda i: (0, 0))],
        out_specs=[pl.BlockSpec((tm, n), lambda i: (i, 0)) for n in out_widths],
        out_shape=[jax.ShapeDtypeStruct((m, n), dt) for n, dt in zip(out_widths, out_dtypes)],
        compiler_params=_params(1, VMEM_LIMIT),
        name="in_proj",
    )(x, g.reshape(1, d), w)


def _proj_res_kernel(*refs, n_in, final_norm):
    a_refs = refs[:n_in]
    w_refs = refs[n_in:2 * n_in]
    res_ref = refs[2 * n_in]
    o_ref = refs[-1]
    acc = res_ref[...]
    for a_ref, w_ref in zip(a_refs, w_refs):
        acc = acc + jnp.dot(a_ref[...], w_ref[...], preferred_element_type=F32)
    if final_norm:
        acc = _rms(acc, refs[2 * n_in + 1][...])
    o_ref[...] = acc


def proj_residual(acts, ws, res, final_g=None, tm=512):
    m, d = res.shape
    n_in = len(acts)
    in_specs = [pl.BlockSpec((tm, a.shape[1]), lambda i: (i, 0)) for a in acts]
    in_specs += [pl.BlockSpec(w.shape, lambda i: (0, 0)) for w in ws]
    in_specs += [pl.BlockSpec((tm, d), lambda i: (i, 0))]
    args = list(acts) + list(ws) + [res]
    if final_g is not None:
        in_specs += [pl.BlockSpec((1, d), lambda i: (0, 0))]
        args += [final_g.reshape(1, d)]
    return pl.pallas_call(
        functools.partial(_proj_res_kernel, n_in=n_in, final_norm=final_g is not None),
        grid=(m // tm,),
        in_specs=in_specs,
        out_specs=pl.BlockSpec((tm, d), lambda i: (i, 0)),
        out_shape=jax.ShapeDtypeStruct((m, d), F32),
        compiler_params=_params(1, VMEM_LIMIT),
        name="proj_residual",
    )(*args)


def _ffn_in_kernel(x_ref, g_ref, wu_ref, wg_ref, cw_ref, cb_ref, o_ref, xn_ref):
    @pl.when(pl.program_id(1) == 0)
    def _():
        xn_ref[...] = _rms(x_ref[0], g_ref[...]).astype(BF16)

    s = xn_ref.shape[0]
    cw = cw_ref[...]
    cb = cb_ref[...]
    sub = lax.broadcasted_iota(jnp.int32, (8, FF_CHUNK), 0)
    prev1 = prev2 = jnp.zeros((8, FF_CHUNK), F32)
    for r in range(s // FF_ROWS):
        xn = xn_ref[r * FF_ROWS:(r + 1) * FF_ROWS, :]
        u = jnp.dot(xn, wu_ref[...], preferred_element_type=F32)
        gt = jnp.dot(xn, wg_ref[...], preferred_element_type=F32)
        r1 = pltpu.roll(gt, 1, 0)
        r2 = pltpu.roll(gt, 2, 0)
        g1 = jnp.concatenate([jnp.where(sub >= 1, r1[:8], prev1), r1[8:]], axis=0)
        g2 = jnp.concatenate([jnp.where(sub >= 2, r2[:8], prev2), r2[8:]], axis=0)
        prev1, prev2 = r1[:8], r2[:8]
        z = cw[2:3] * gt + cw[1:2] * g1 + cw[0:1] * g2 + cb
        gelu = 0.5 * z * (1.0 + lax.erf(z * math.sqrt(0.5)))
        o_ref[0, r * FF_ROWS:(r + 1) * FF_ROWS, :] = (gelu * u).astype(BF16)


def ffn_in(h3, g, w_in, conv_w, conv_b):
    b, s, d = h3.shape
    nc = D_FF // FF_CHUNK
    return pl.pallas_call(
        _ffn_in_kernel,
        grid=(b, nc),
        in_specs=[pl.BlockSpec((1, s, d), lambda i, c: (i, 0, 0)),
                  pl.BlockSpec((1, d), lambda i, c: (0, 0)),
                  pl.BlockSpec((d, FF_CHUNK), lambda i, c: (0, c)),
                  pl.BlockSpec((d, FF_CHUNK), lambda i, c: (0, c + nc)),
                  pl.BlockSpec((3, FF_CHUNK), lambda i, c: (0, c)),
                  pl.BlockSpec((1, FF_CHUNK), lambda i, c: (0, c))],
        out_specs=pl.BlockSpec((1, s, FF_CHUNK), lambda i, c: (i, 0, c)),
        out_shape=jax.ShapeDtypeStruct((b, s, D_FF), BF16),
        scratch_shapes=[pltpu.VMEM((s, d), BF16)],
        compiler_params=_params(2, VMEM_LIMIT),
        name="ffn_in",
    )(h3, g.reshape(1, d), w_in, w_in, conv_w, conv_b.reshape(1, D_FF))


def _bucket_of_distance(s):
    max_exact = REL_BUCKETS // 2
    n_large = REL_BUCKETS - max_exact
    thresholds = []
    for k in range(1, n_large):
        t = max_exact * (REL_MAX_DIST / max_exact) ** (k / n_large)
        ti = int(round(t))
        thresholds.append(ti if abs(t - ti) < 1e-9 else int(math.ceil(t)))
    d = np.arange(s)
    large = max_exact + sum((d >= t).astype(np.int64) for t in thresholds)
    return np.where(d < max_exact, d, np.minimum(large, REL_BUCKETS - 1)).astype(np.int32)


def _dilated_log_multiplicity(s):
    d = np.arange(s)
    count = np.zeros(s, np.int64)
    for window, dil in DIL_CONFIGS:
        count += ((d % dil == 0) & (d // dil <= window // dil)).astype(np.int64)
    return np.where(count > 0, np.log(np.maximum(count, 1)), NEG).astype(np.float32)


def _bias_strip_kernel(w_ref, o_ref, *, s):
    tile = jnp.broadcast_to(w_ref[0], (TQ, s + TQ))
    tile = pltpu.roll(tile, 0, 1, stride=1, stride_axis=0)
    o_ref[0] = tile[:, TQ:]


def bias_strips(rel_bias, s):
    bucket = _bucket_of_distance(s)
    per_dist = jnp.take(rel_bias.T.astype(F32), jnp.asarray(bucket), axis=1)
    logmult = jnp.asarray(_dilated_log_multiplicity(s))
    dil0 = MOBA_HEADS + DIFF_HEADS
    covered = logmult > 0.5 * NEG
    dil_rows = jnp.where(covered[None, :], per_dist[dil0:] + logmult[None, :], NEG)
    per_dist = jnp.concatenate([per_dist[:dil0], dil_rows], axis=0)
    nh = per_dist.shape[0]
    w = jnp.concatenate([jnp.full((nh, 1), NEG, F32), per_dist[:, ::-1],
                         jnp.full((nh, TQ - 1), NEG, F32)], axis=1).reshape(nh, 1, s + TQ)
    return pl.pallas_call(
        functools.partial(_bias_strip_kernel, s=s),
        grid=(nh,),
        in_specs=[pl.BlockSpec((1, 1, s + TQ), lambda h: (h, 0, 0))],
        out_specs=pl.BlockSpec((1, TQ, s), lambda h: (h, 0, 0)),
        out_shape=jax.ShapeDtypeStruct((nh, TQ, s), F32),
        compiler_params=_params(1),
        name="bias_strips",
    )(w)


def _store_transposed(kt_ref, idx, k):
    s = k.shape[0]
    for j in range(s // TK):
        kt_ref[idx, :, j * TK:(j + 1) * TK] = k[j * TK:(j + 1) * TK, :].T


def _attn_kernel(*refs, mode, moba, lam_init, s):
    if mode == "diff":
        q_ref, k_ref, v_ref, b_ref, lam_ref, sg_ref, o_ref, kt_ref, sc_ref, p_ref = refs
    elif moba:
        q_ref, k_ref, v_ref, b_ref, o_ref, kt_ref, sc_ref, p_ref, drop_ref = refs
    else:
        q_ref, k_ref, v_ref, b_ref, o_ref, kt_ref, sc_ref, p_ref = refs
    nblk = s // TK
    lane = lax.broadcasted_iota(jnp.int32, (TQ, LANES), 1)
    halves = (lane < HEAD_DIM, lane >= HEAD_DIM)
    nt = (((1,), (1,)), ((), ()))

    if moba:
        k_all = k_ref[0]
        q_all = q_ref[0]
        rowblk = lax.broadcasted_iota(jnp.int32, (s, LANES), 0) // TK
        lane_s = lax.broadcasted_iota(jnp.int32, (s, LANES), 1)
        kmean = jnp.mean(k_all.astype(F32).reshape(nblk, TK, LANES), axis=1)
        k_hi = kmean.astype(BF16)
        rem = kmean - k_hi.astype(F32)
        k_mid = rem.astype(BF16)
        k_lo = (rem - k_mid.astype(F32)).astype(BF16)
        kmean3 = jnp.concatenate([k_hi, k_mid, k_lo], axis=1)
        qblk = lax.broadcasted_iota(jnp.int32, (nblk, s), 1) // TQ
        blk = lax.broadcasted_iota(jnp.int32, (nblk, s), 0)
        for c in range(2):
            base = HEAD_DIM if c == 0 else 0
            in_half = (lane_s < HEAD_DIM) if c == 0 else (lane_s >= HEAD_DIM)
            onehot = jnp.where(lane_s - base == rowblk, 1.0, 0.0).astype(BF16)
            _store_transposed(kt_ref, c, jnp.where(in_half, k_all, onehot))
            qh = jnp.where(in_half, q_all, jnp.zeros_like(q_all))
            gate = lax.dot_general(kmean3, jnp.concatenate([qh, qh, qh], axis=1), nt,
                                   preferred_element_type=F32)
            rank = jnp.zeros((nblk, s), jnp.int32)
            for n in range(nblk - 1):
                gn = gate[n:n + 1, :]
                beats = jnp.where(gn > gate, 1, jnp.where(gn == gate, jnp.where(n < blk, 1, 0), 0))
                rank = rank + jnp.where(n < qblk, beats, 0)
            drop = jnp.where(blk < qblk, jnp.where(rank >= MOBA_TOPK, NEG, 0.0), 0.0)
            drop_ref[c] = jnp.zeros((LANES, s), F32)
            drop_ref[c, base:base + nblk, :] = drop

    else:
        _store_transposed(kt_ref, 0, k_ref[0])

    if mode == "diff":
        lp = lam_ref[...]
        lam = (jnp.exp(jnp.sum(lp[0:1] * lp[1:2], axis=1, keepdims=True))
               - jnp.exp(jnp.sum(lp[2:3] * lp[3:4], axis=1, keepdims=True)) + lam_init)

    units = [(i, c) for i in range(nblk) for c in range(2)]

    def scores(u):
        i, c = units[u]
        nk = (i + 1) * TK
        q = q_ref[0, i * TQ:(i + 1) * TQ, :]
        qc = jnp.where(halves[c], q, jnp.zeros_like(q))
        if moba and i > 0:
            sel = drop_ref[c, :, i * TQ:(i + 1) * TQ].T
            qc = jnp.where(halves[c], q, sel.astype(BF16))
        sc_ref[u % 2, :, 0:nk] = (jnp.dot(qc, kt_ref[c if moba else 0, :, 0:nk],
                                          preferred_element_type=F32)
                                  + b_ref[c if mode == "pair" else 0, :, s - nk:])

    def softmax(u):
        nk = (units[u][0] + 1) * TK
        m = jnp.max(sc_ref[u % 2, :, 0:nk], axis=1, keepdims=True)
        p = jnp.exp(sc_ref[u % 2, :, 0:nk] - m)
        p_ref[u % 2, :, 0:nk] = p.astype(BF16)
        return jnp.sum(p, axis=1, keepdims=True)

    outs = []

    def weighted_values(u, l):
        i, c = units[u]
        nk = (i + 1) * TK
        o = jnp.dot(p_ref[u % 2, :, 0:nk], v_ref[0, 0:nk, :], preferred_element_type=F32)
        outs.append(o / l)
        if c == 1:
            if mode == "pair":
                res = jnp.where(halves[0], outs[0], outs[1])
            else:
                res = _rms(outs[0] - lam * outs[1], sg_ref[...]) * (1.0 - lam_init)
            o_ref[0, i * TQ:(i + 1) * TQ, :] = res.astype(o_ref.dtype)
            outs.clear()

    scores(0)
    l_prev = None
    for u in range(len(units)):
        if u + 1 < len(units):
            scores(u + 1)
        l = softmax(u)
        if u > 0:
            weighted_values(u - 1, l_prev)
        l_prev = l
    weighted_values(len(units) - 1, l_prev)


def attention(qkv, col_q, col_k, col_v, n_blocks, strips, head0, mode, moba=False,
              lam_params=None, subln_g=None, lam_init=0.0):
    b, s, _ = qkv.shape
    nb = 2 if mode == "pair" else 1
    seq = lambda col: pl.BlockSpec((1, s, LANES), lambda p, bb: (bb, 0, col + p))
    in_specs = [seq(col_q), seq(col_k), seq(col_v),
                pl.BlockSpec((nb, TQ, s), lambda p, bb: (head0 // nb + p, 0, 0))]
    args = [qkv, qkv, qkv, strips]
    if mode == "diff":
        in_specs += [pl.BlockSpec((4, HEAD_DIM), lambda p, bb: (0, 0)),
                     pl.BlockSpec((1, LANES), lambda p, bb: (0, 0))]
        args += [lam_params.astype(F32), subln_g.reshape(1, LANES)]
    scratch = [pltpu.VMEM((2 if moba else 1, LANES, s), BF16),
               pltpu.VMEM((2, TQ, s), F32), pltpu.VMEM((2, TQ, s), BF16)]
    if moba:
        scratch += [pltpu.VMEM((2, LANES, s), F32)]
    return pl.pallas_call(
        functools.partial(_attn_kernel, mode=mode, moba=moba, lam_init=lam_init, s=s),
        grid=(n_blocks, b),
        in_specs=in_specs,
        out_specs=pl.BlockSpec((1, s, LANES), lambda p, bb: (bb, 0, p)),
        out_shape=jax.ShapeDtypeStruct((b, s, n_blocks * LANES), BF16),
        scratch_shapes=scratch,
        compiler_params=_params(2, VMEM_LIMIT),
        name="attn_" + mode + ("_moba" if moba else ""),
    )(*args)


def _mla_up_kernel(lat_ref, qg_ref, kvg_ref, wa_ref, wb_ref, wk_ref, wv_ref, cq_ref, sq_ref,
                   ck_ref, sk_ref, q_ref, k_ref, v_ref):
    lat = lat_ref[...]
    cqn = _rms(lat[:, :MLA_Q_RANK], qg_ref[...]).astype(BF16)
    ckvn = _rms(lat[:, MLA_Q_RANK:MLA_Q_RANK + MLA_KV_RANK], kvg_ref[...]).astype(BF16)
    qa = jnp.dot(cqn, wa_ref[...], preferred_element_type=F32)
    qb = jnp.dot(cqn, wb_ref[...], preferred_element_type=F32)
    kn = jnp.dot(ckvn, wk_ref[...], preferred_element_type=F32).astype(BF16)
    v_ref[...] = jnp.dot(ckvn, wv_ref[...], preferred_element_type=F32).astype(BF16)
    x = lat[:, MLA_Q_RANK + MLA_KV_RANK:]
    kr = (x * ck_ref[...] + pltpu.roll(x, MLA_ROPE_DIM, 1) * sk_ref[...]).astype(BF16)
    cq = cq_ref[...]
    sq = sq_ref[...]
    hw = 2 * LANES
    for h in range(MLA_HEADS):
        q_ref[:, h * hw:(h + 1) * hw] = (qa[:, h * hw:(h + 1) * hw] * cq
                                         + qb[:, h * hw:(h + 1) * hw] * sq).astype(BF16)
        k_ref[:, h * hw:h * hw + LANES] = kn[:, h * LANES:(h + 1) * LANES]
        k_ref[:, h * hw + LANES:(h + 1) * hw] = kr


def mla_up(latent, q_norm, kv_norm, wa, wb, wk, wv, rope_tabs, s, tm=1024):
    m = latent.shape[0]
    cq, sq, ck, sk = rope_tabs
    per_seq = s // tm
    tab = lambda w: pl.BlockSpec((tm, w), lambda i: (i % per_seq, 0))
    full = lambda a: pl.BlockSpec(a.shape, lambda i: (0, 0))
    hw = 2 * LANES
    rows = lambda w: pl.BlockSpec((tm, w), lambda i: (i, 0))
    return pl.pallas_call(
        _mla_up_kernel,
        grid=(m // tm,),
        in_specs=[rows(latent.shape[1]),
                  pl.BlockSpec((1, MLA_Q_RANK), lambda i: (0, 0)),
                  pl.BlockSpec((1, MLA_KV_RANK), lambda i: (0, 0)),
                  full(wa), full(wb), full(wk), full(wv), tab(hw), tab(hw), tab(LANES), tab(LANES)],
        out_specs=[rows(MLA_HEADS * hw), rows(MLA_HEADS * hw), rows(MLA_HEADS * LANES)],
        out_shape=[jax.ShapeDtypeStruct((m, MLA_HEADS * hw), BF16),
                   jax.ShapeDtypeStruct((m, MLA_HEADS * hw), BF16),
                   jax.ShapeDtypeStruct((m, MLA_HEADS * LANES), BF16)],
        compiler_params=_params(1, VMEM_LIMIT),
        name="mla_up",
    )(latent, q_norm.reshape(1, -1), kv_norm.reshape(1, -1), wa, wb, wk, wv, cq, sq, ck, sk)


def _mla_attn_kernel(q_ref, k_ref, v_ref, o_ref, kt_ref, sc_ref, p_ref, *, s):
    scale = (MLA_NOPE_DIM + MLA_ROPE_DIM) ** -0.5
    _store_transposed(kt_ref, 0, k_ref[0])
    row = lax.broadcasted_iota(jnp.int32, (TQ, TK), 0)
    col = lax.broadcasted_iota(jnp.int32, (TQ, TK), 1)
    causal = row >= col
    nblk = s // TQ

    def scores(i):
        nk = (i + 1) * TK
        q = q_ref[0, i * TQ:(i + 1) * TQ, :]
        sc = jnp.dot(q, kt_ref[0, :, 0:nk], preferred_element_type=F32) * scale
        if i > 0:
            sc_ref[i % 2, :, 0:nk - TK] = sc[:, :nk - TK]
        sc_ref[i % 2, :, nk - TK:nk] = jnp.where(causal, sc[:, nk - TK:], NEG)

    def softmax(i):
        nk = (i + 1) * TK
        m = jnp.max(sc_ref[i % 2, :, 0:nk], axis=1, keepdims=True)
        p = jnp.exp(sc_ref[i % 2, :, 0:nk] - m)
        p_ref[i % 2, :, 0:nk] = p.astype(BF16)
        return jnp.sum(p, axis=1, keepdims=True)

    def weighted_values(i, l):
        nk = (i + 1) * TK
        o = jnp.dot(p_ref[i % 2, :, 0:nk], v_ref[0, 0:nk, :], preferred_element_type=F32)
        o_ref[0, i * TQ:(i + 1) * TQ, :] = (o / l).astype(o_ref.dtype)

    scores(0)
    l_prev = None
    for i in range(nblk):
        if i + 1 < nblk:
            scores(i + 1)
        l = softmax(i)
        if i > 0:
            weighted_values(i - 1, l_prev)
        l_prev = l
    weighted_values(nblk - 1, l_prev)


def mla_attention(q, k, v):
    b, s, _ = q.shape
    hw = 2 * LANES
    return pl.pallas_call(
        functools.partial(_mla_attn_kernel, s=s),
        grid=(MLA_HEADS, b),
        in_specs=[pl.BlockSpec((1, s, hw), lambda h, bb: (bb, 0, h)),
                  pl.BlockSpec((1, s, hw), lambda h, bb: (bb, 0, h)),
                  pl.BlockSpec((1, s, LANES), lambda h, bb: (bb, 0, h))],
        out_specs=pl.BlockSpec((1, s, LANES), lambda h, bb: (bb, 0, h)),
        out_shape=jax.ShapeDtypeStruct((b, s, MLA_HEADS * LANES), BF16),
        scratch_shapes=[pltpu.VMEM((1, hw, s), BF16),
                        pltpu.VMEM((2, TQ, s), F32), pltpu.VMEM((2, TQ, s), BF16)],
        compiler_params=_params(2, VMEM_LIMIT),
        name="mla_attn",
    )(q, k, v)


def _rope_tables(s):
    half = MLA_ROPE_DIM // 2
    freq = ROPE_THETA ** (-jnp.arange(half, dtype=F32) / half)
    ang = jnp.arange(s, dtype=F32)[:, None] * freq[None, :]
    cos, sin = jnp.cos(ang), jnp.sin(ang)
    cos2 = jnp.concatenate([cos, cos], axis=1)
    sin2 = jnp.concatenate([-sin, sin], axis=1)
    z64 = jnp.zeros((s, MLA_ROPE_DIM), F32)
    cq = jnp.concatenate([jnp.ones((s, MLA_NOPE_DIM), F32), cos2, z64], axis=1)
    sq = jnp.concatenate([jnp.zeros((s, MLA_NOPE_DIM), F32), sin2, z64], axis=1)
    ck = jnp.concatenate([cos2, z64], axis=1)
    sk = jnp.concatenate([sin2, z64], axis=1)
    return cq, sq, ck, sk


def _swap_halves(w):
    half = w.shape[-1] // 2
    return jnp.concatenate([w[..., half:], w[..., :half]], axis=-1)


def _mla_q_weights(w_uq):
    r = w_uq.shape[0]
    w = w_uq.reshape(r, MLA_HEADS, MLA_NOPE_DIM + MLA_ROPE_DIM)
    nope, rope = w[..., :MLA_NOPE_DIM], w[..., MLA_NOPE_DIM:]
    z64 = jnp.zeros((r, MLA_HEADS, MLA_ROPE_DIM), w.dtype)
    z128 = jnp.zeros((r, MLA_HEADS, MLA_NOPE_DIM), w.dtype)
    wa = jnp.concatenate([nope, rope, z64], axis=-1).reshape(r, -1)
    wb = jnp.concatenate([z128, _swap_halves(rope), z64], axis=-1).reshape(r, -1)
    return wa.astype(BF16), wb.astype(BF16)


def kernel(x, rel_bias, even_norm1, even_w_in, diff_lambda, diff_subln, even_w_out,
           odd_norm1, odd_w_in, mla_q_norm, mla_w_uq, mla_kv_norm, mla_w_ukv, odd_w_out,
           ffn_norm, ffn_w_in, ffn_conv_w, ffn_conv_b, ffn_w_out, final_norm):
    b, s, d = x.shape
    m = b * s
    strips = bias_strips(rel_bias, s)
    rope_tabs = _rope_tables(s)
    qscale = HEAD_DIM ** -0.5
    h = x.reshape(m, d)
    for layer in range(DEPTH):
        li = layer // 2
        if layer % 2 == 0:
            lam_init = 0.8 - 0.6 * math.exp(-0.3 * layer)
            w = even_w_in[li]
            mw, dw = MOBA_HEADS * HEAD_DIM, DIFF_HEADS * 2 * HEAD_DIM
            w = jnp.concatenate([w[:, :mw] * qscale, w[:, mw:3 * mw],
                                 w[:, 3 * mw:3 * mw + dw] * qscale, w[:, 3 * mw + dw:]],
                                axis=1).astype(BF16)
            (qkv,) = in_proj(h, even_norm1[li], w, [w.shape[1]], [BF16])
            qkv = qkv.reshape(b, s, -1)
            nb = mw // LANES
            o_a = attention(qkv, 0, nb, 2 * nb, nb, strips, 0, "pair", moba=True)
            o_b = attention(qkv, 3 * nb, 4 * nb, 5 * nb, DIFF_HEADS, strips, MOBA_HEADS, "diff",
                            lam_params=diff_lambda[li], subln_g=diff_subln[li], lam_init=lam_init)
            wo = even_w_out[li].astype(BF16)
        else:
            w = odd_w_in[li]
            dw = DIL_HEADS * HEAD_DIM
            lat0 = 3 * dw
            w_dil = jnp.concatenate([w[:, :dw] * qscale, w[:, dw:lat0]], axis=1).astype(BF16)
            kr_cols = w[:, lat0 + MLA_Q_RANK + MLA_KV_RANK:]
            w_lat = jnp.concatenate([w[:, lat0:], _swap_halves(kr_cols)], axis=1).astype(BF16)
            qkv, latent = in_proj(h, odd_norm1[li], jnp.concatenate([w_dil, w_lat], axis=1),
                                  [w_dil.shape[1], w_lat.shape[1]], [BF16, F32])
            qkv = qkv.reshape(b, s, -1)
            nb = dw // LANES
            o_a = attention(qkv, 0, nb, 2 * nb, nb, strips, MOBA_HEADS + DIFF_HEADS, "pair")
            wa, wb = _mla_q_weights(mla_w_uq[li])
            wkv = mla_w_ukv[li].reshape(MLA_KV_RANK, MLA_HEADS, MLA_NOPE_DIM + MLA_V_DIM)
            wk = wkv[..., :MLA_NOPE_DIM].reshape(MLA_KV_RANK, -1).astype(BF16)
            wv = wkv[..., MLA_NOPE_DIM:].reshape(MLA_KV_RANK, -1).astype(BF16)
            q_m, k_m, v_m = mla_up(latent, mla_q_norm[li], mla_kv_norm[li], wa, wb, wk, wv,
                                   rope_tabs, s)
            o_b = mla_attention(q_m.reshape(b, s, -1), k_m.reshape(b, s, -1),
                                v_m.reshape(b, s, -1))
            wo = odd_w_out[li].astype(BF16)
        wa_w = o_a.shape[-1]
        h = proj_residual([o_a.reshape(m, -1), o_b.reshape(m, -1)],
                          [wo[:wa_w], wo[wa_w:]], h)
        act = ffn_in(h.reshape(b, s, d), ffn_norm[layer], ffn_w_in[layer].astype(BF16),
                     ffn_conv_w[layer], ffn_conv_b[layer])
        h = proj_residual([act.reshape(m, D_FF)], [ffn_w_out[layer].astype(BF16)], h,
                          final_g=final_norm if layer == DEPTH - 1 else None)
    return h.reshape(b, s, d)
```

```python
import functools
import math

import numpy as np
import jax
import jax.numpy as jnp
from jax import lax
from jax.experimental import pallas as pl
from jax.experimental.pallas import tpu as pltpu

D_MODEL = 1024
DEPTH = 4
HEAD_DIM = 64
MOBA_HEADS = 8
MOBA_BLOCK = 256
MOBA_TOPK = 3
DIFF_HEADS = 4
DIL_HEADS = 8
DIL_CONFIGS = ((128, 1), (512, 4), (2048, 16))
MLA_HEADS = 4
MLA_Q_RANK = 256
MLA_KV_RANK = 128
MLA_NOPE_DIM = 128
MLA_ROPE_DIM = 64
MLA_V_DIM = 128
ROPE_THETA = 10000.0
REL_BUCKETS = 32
REL_MAX_DIST = 1024
N_BIAS_HEADS = MOBA_HEADS + DIFF_HEADS + DIL_HEADS
D_FF = 2816
EPS = 1e-6
NEG = -1e30
LOG2E = math.log2(math.e)

LANES = 128
TQ = 256
TK = 256
FF_CHUNK = 256
FF_ROWS = 512
VMEM_LIMIT = 56 * 1024 * 1024

F32 = jnp.float32
BF16 = jnp.bfloat16


def _params(n_axes, vmem=None):
    return pltpu.CompilerParams(dimension_semantics=("arbitrary",) * n_axes,
                                vmem_limit_bytes=vmem)


def _rms(x, g):
    ms = jnp.mean(x * x, axis=-1, keepdims=True)
    return x * lax.rsqrt(ms + EPS) * g


def _in_proj_kernel(x_ref, g_ref, w_ref, *o_refs, row_chunk):
    tm = x_ref.shape[0]
    g = g_ref[...]
    for r in range(tm // row_chunk):
        rows = slice(r * row_chunk, (r + 1) * row_chunk)
        xn = _rms(x_ref[rows, :], g).astype(BF16)
        acc = jnp.dot(xn, w_ref[...], preferred_element_type=F32)
        col = 0
        for o_ref in o_refs:
            width = o_ref.shape[1]
            o_ref[rows, :] = acc[:, col:col + width].astype(o_ref.dtype)
            col += width


def in_proj(x, g, w, out_widths, out_dtypes, tm=1024, row_chunk=512):
    m, d = x.shape
    return pl.pallas_call(
        functools.partial(_in_proj_kernel, row_chunk=row_chunk),
        grid=(m // tm,),
        in_specs=[pl.BlockSpec((tm, d), lambda i: (i, 0)),
                  pl.BlockSpec((1, d), lambda i: (0, 0)),
                  pl.BlockSpec(w.shape, lambda i: (0, 0))],
        out_specs=[pl.BlockSpec((tm, n), lambda i: (i, 0)) for n in out_widths],
        out_shape=[jax.ShapeDtypeStruct((m, n), dt) for n, dt in zip(out_widths, out_dtypes)],
        compiler_params=_params(1, VMEM_LIMIT),
        name="in_proj",
    )(x, g.reshape(1, d), w)


def _proj_res_kernel(*refs, n_in, final_norm):
    a_refs = refs[:n_in]
    w_refs = refs[n_in:2 * n_in]
    res_ref = refs[2 * n_in]
    o_ref = refs[-1]
    acc = res_ref[...]
    for a_ref, w_ref in zip(a_refs, w_refs):
        acc = acc + jnp.dot(a_ref[...], w_ref[...], preferred_element_type=F32)
    if final_norm:
        acc = _rms(acc, refs[2 * n_in + 1][...])
    o_ref[...] = acc


def proj_residual(acts, ws, res, final_g=None, tm=512):
    m, d = res.shape
    n_in = len(acts)
    in_specs = [pl.BlockSpec((tm, a.shape[1]), lambda i: (i, 0)) for a in acts]
    in_specs += [pl.BlockSpec(w.shape, lambda i: (0, 0)) for w in ws]
    in_specs += [pl.BlockSpec((tm, d), lambda i: (i, 0))]
    args = list(acts) + list(ws) + [res]
    if final_g is not None:
        in_specs += [pl.BlockSpec((1, d), lambda i: (0, 0))]
        args += [final_g.reshape(1, d)]
    return pl.pallas_call(
        functools.partial(_proj_res_kernel, n_in=n_in, final_norm=final_g is not None),
        grid=(m // tm,),
        in_specs=in_specs,
        out_specs=pl.BlockSpec((tm, d), lambda i: (i, 0)),
        out_shape=jax.ShapeDtypeStruct((m, d), F32),
        compiler_params=_params(1, VMEM_LIMIT),
        name="proj_residual",
    )(*args)


def _ffn_in_kernel(x_ref, g_ref, w_ref, cw_ref, cb_ref, o_ref, xn_ref):
    @pl.when(pl.program_id(1) == 0)
    def _():
        xn_ref[...] = _rms(x_ref[0], g_ref[...]).astype(BF16)

    s = xn_ref.shape[0]
    cw = cw_ref[...]
    cb = cb_ref[...]
    sub = lax.broadcasted_iota(jnp.int32, (8, FF_CHUNK), 0)
    prev1 = prev2 = jnp.zeros((8, FF_CHUNK), F32)
    for r in range(s // FF_ROWS):
        xn = xn_ref[r * FF_ROWS:(r + 1) * FF_ROWS, :]
        ug = jnp.dot(xn, w_ref[...], preferred_element_type=F32)
        u, gt = ug[:, :FF_CHUNK], ug[:, FF_CHUNK:]
        r1 = pltpu.roll(gt, 1, 0)
        r2 = pltpu.roll(gt, 2, 0)
        g1 = jnp.concatenate([jnp.where(sub >= 1, r1[:8], prev1), r1[8:]], axis=0)
        g2 = jnp.concatenate([jnp.where(sub >= 2, r2[:8], prev2), r2[8:]], axis=0)
        prev1, prev2 = r1[:8], r2[:8]
        z = cw[2:3] * gt + cw[1:2] * g1 + cw[0:1] * g2 + cb
        gelu = 0.5 * z * (1.0 + lax.erf(z * math.sqrt(0.5)))
        o_ref[0, r * FF_ROWS:(r + 1) * FF_ROWS, :] = (gelu * u).astype(BF16)


def ffn_in(h3, g, w_ug, conv_w, conv_b):
    b, s, d = h3.shape
    nc = D_FF // FF_CHUNK
    return pl.pallas_call(
        _ffn_in_kernel,
        grid=(b, nc),
        in_specs=[pl.BlockSpec((1, s, d), lambda i, c: (i, 0, 0)),
                  pl.BlockSpec((1, d), lambda i, c: (0, 0)),
                  pl.BlockSpec((d, 2 * FF_CHUNK), lambda i, c: (0, c)),
                  pl.BlockSpec((3, FF_CHUNK), lambda i, c: (0, c)),
                  pl.BlockSpec((1, FF_CHUNK), lambda i, c: (0, c))],
        out_specs=pl.BlockSpec((1, s, FF_CHUNK), lambda i, c: (i, 0, c)),
        out_shape=jax.ShapeDtypeStruct((b, s, D_FF), BF16),
        scratch_shapes=[pltpu.VMEM((s, d), BF16)],
        compiler_params=_params(2, VMEM_LIMIT),
        name="ffn_in",
    )(h3, g.reshape(1, d), w_ug, conv_w, conv_b.reshape(1, D_FF))


def _bucket_of_distance(s):
    max_exact = REL_BUCKETS // 2
    n_large = REL_BUCKETS - max_exact
    thresholds = []
    for k in range(1, n_large):
        t = max_exact * (REL_MAX_DIST / max_exact) ** (k / n_large)
        ti = int(round(t))
        thresholds.append(ti if abs(t - ti) < 1e-9 else int(math.ceil(t)))
    d = np.arange(s)
    large = max_exact + sum((d >= t).astype(np.int64) for t in thresholds)
    return np.where(d < max_exact, d, np.minimum(large, REL_BUCKETS - 1)).astype(np.int32)


def _dilated_log_multiplicity(s):
    d = np.arange(s)
    count = np.zeros(s, np.int64)
    for window, dil in DIL_CONFIGS:
        count += ((d % dil == 0) & (d // dil <= window // dil)).astype(np.int64)
    return np.where(count > 0, np.log(np.maximum(count, 1)), NEG).astype(np.float32)


def _bias_strip_kernel(w_ref, o_ref, *, s):
    tile = jnp.broadcast_to(w_ref[0], (TQ, s + TQ))
    tile = pltpu.roll(tile, 0, 1, stride=1, stride_axis=0)
    o_ref[0] = tile[:, TQ:]


def bias_strips(rel_bias, s):
    bucket = _bucket_of_distance(s)
    per_dist = jnp.take(rel_bias.T.astype(F32), jnp.asarray(bucket), axis=1)
    logmult = jnp.asarray(_dilated_log_multiplicity(s))
    dil0 = MOBA_HEADS + DIFF_HEADS
    covered = logmult > 0.5 * NEG
    dil_rows = jnp.where(covered[None, :], per_dist[dil0:] + logmult[None, :], NEG)
    per_dist = jnp.concatenate([per_dist[:dil0], dil_rows], axis=0) * LOG2E
    nh = per_dist.shape[0]
    w = jnp.concatenate([jnp.full((nh, 1), NEG, F32), per_dist[:, ::-1],
                         jnp.full((nh, TQ - 1), NEG, F32)], axis=1).reshape(nh, 1, s + TQ)
    return pl.pallas_call(
        functools.partial(_bias_strip_kernel, s=s),
        grid=(nh,),
        in_specs=[pl.BlockSpec((1, 1, s + TQ), lambda h: (h, 0, 0))],
        out_specs=pl.BlockSpec((1, TQ, s), lambda h: (h, 0, 0)),
        out_shape=jax.ShapeDtypeStruct((nh, TQ, s), F32),
        compiler_params=_params(1),
        name="bias_strips",
    )(w)


def _store_transposed(kt_ref, idx, k):
    s = k.shape[0]
    for j in range(s // TK):
        kt_ref[idx, :, j * TK:(j + 1) * TK] = k[j * TK:(j + 1) * TK, :].T


def _attn_kernel(*refs, mode, moba, lam_init, s):
    if mode == "diff":
        q_ref, k_ref, v_ref, b_ref, lam_ref, sg_ref, o_ref, kt_ref, sc_ref, p_ref = refs
    elif moba:
        q_ref, k_ref, v_ref, b_ref, o_ref, kt_ref, sc_ref, p_ref, vone_ref, drop_ref = refs
    else:
        q_ref, k_ref, v_ref, b_ref, o_ref, kt_ref, sc_ref, p_ref, vone_ref = refs
    nblk = s // TK
    lane = lax.broadcasted_iota(jnp.int32, (TQ, LANES), 1)
    halves = (lane < HEAD_DIM, lane >= HEAD_DIM)
    nt = (((1,), (1,)), ((), ()))

    if moba:
        k_all = k_ref[0]
        q_all = q_ref[0]
        rowblk = lax.broadcasted_iota(jnp.int32, (s, LANES), 0) // TK
        lane_s = lax.broadcasted_iota(jnp.int32, (s, LANES), 1)
        kmean = jnp.mean(k_all.astype(F32).reshape(nblk, TK, LANES), axis=1)
        k_hi = kmean.astype(BF16)
        rem = kmean - k_hi.astype(F32)
        k_mid = rem.astype(BF16)
        k_lo = (rem - k_mid.astype(F32)).astype(BF16)
        kmean3 = jnp.concatenate([k_hi, k_mid, k_lo], axis=1)
        qblk = lax.broadcasted_iota(jnp.int32, (nblk, s), 1) // TQ
        blk = lax.broadcasted_iota(jnp.int32, (nblk, s), 0)
        for c in range(2):
            base = HEAD_DIM if c == 0 else 0
            in_half = (lane_s < HEAD_DIM) if c == 0 else (lane_s >= HEAD_DIM)
            onehot = jnp.where(lane_s - base == rowblk, 1.0, 0.0).astype(BF16)
            _store_transposed(kt_ref, c, jnp.where(in_half, k_all, onehot))
            qh = jnp.where(in_half, q_all, jnp.zeros_like(q_all))
            gate = lax.dot_general(kmean3, jnp.concatenate([qh, qh, qh], axis=1), nt,
                                   preferred_element_type=F32)
            rank = jnp.zeros((nblk, s), jnp.int32)
            for n in range(nblk - 1):
                gn = gate[n:n + 1, :]
                beats = jnp.where(gn > gate, 1, jnp.where(gn == gate, jnp.where(n < blk, 1, 0), 0))
                rank = rank + jnp.where(n < qblk, beats, 0)
            drop = jnp.where(blk < qblk, jnp.where(rank >= MOBA_TOPK, NEG, 0.0), 0.0)
            drop_ref[c] = jnp.zeros((LANES, s), F32)
            drop_ref[c, base:base + nblk, :] = drop

    else:
        _store_transposed(kt_ref, 0, k_ref[0])

    if mode == "diff":
        lp = lam_ref[...]
        lam = (jnp.exp(jnp.sum(lp[0:1] * lp[1:2], axis=1, keepdims=True))
               - jnp.exp(jnp.sum(lp[2:3] * lp[3:4], axis=1, keepdims=True)) + lam_init)

    units = [(i, c) for i in range(nblk) for c in range(2)]

    if mode == "pair":
        v_all = v_ref[0]
        lane_s = lax.broadcasted_iota(jnp.int32, (s, LANES), 1)
        ones = jnp.ones_like(v_all)
        vone_ref[0] = jnp.where(lane_s < HEAD_DIM, v_all, ones)
        vone_ref[1] = jnp.where(lane_s >= HEAD_DIM, v_all, ones)

    def scores(u):
        i, c = units[u]
        nk = (i + 1) * TK
        q = q_ref[0, i * TQ:(i + 1) * TQ, :]
        qc = jnp.where(halves[c], q, jnp.zeros_like(q))
        if moba and i > 0:
            sel = drop_ref[c, :, i * TQ:(i + 1) * TQ].T
            qc = jnp.where(halves[c], q, sel.astype(BF16))
        sc_ref[u % 2, :, 0:nk] = (jnp.dot(qc, kt_ref[c if moba else 0, :, 0:nk],
                                          preferred_element_type=F32)
                                  + b_ref[c if mode == "pair" else 0, :, s - nk:])

    def softmax(u):
        nk = (units[u][0] + 1) * TK
        m = jnp.max(sc_ref[u % 2, :, 0:nk], axis=1, keepdims=True)
        p = jnp.exp2(sc_ref[u % 2, :, 0:nk] - m)
        p_ref[u % 2, :, 0:nk] = p.astype(BF16)
        return None if mode == "pair" else jnp.sum(p, axis=1, keepdims=True)

    outs = []

    def weighted_values(u, l):
        i, c = units[u]
        nk = (i + 1) * TK
        if mode == "pair":
            o = jnp.dot(p_ref[u % 2, :, 0:nk], vone_ref[c, 0:nk, :], preferred_element_type=F32)
            l = pltpu.roll(o, HEAD_DIM, 1)
        else:
            o = jnp.dot(p_ref[u % 2, :, 0:nk], v_ref[0, 0:nk, :], preferred_element_type=F32)
        outs.append(o / l)
        if c == 1:
            if mode == "pair":
                res = jnp.where(halves[0], outs[0], outs[1])
            else:
                res = _rms(outs[0] - lam * outs[1], sg_ref[...]) * (1.0 - lam_init)
            o_ref[0, i * TQ:(i + 1) * TQ, :] = res.astype(o_ref.dtype)
            outs.clear()

    scores(0)
    l_prev = None
    for u in range(len(units)):
        if u + 1 < len(units):
            scores(u + 1)
        l = softmax(u)
        if u > 0:
            weighted_values(u - 1, l_prev)
        l_prev = l
    weighted_values(len(units) - 1, l_prev)


def attention(qkv, col_q, col_k, col_v, n_blocks, strips, head0, mode, moba=False,
              lam_params=None, subln_g=None, lam_init=0.0):
    b, s, _ = qkv.shape
    nb = 2 if mode == "pair" else 1
    seq = lambda col: pl.BlockSpec((1, s, LANES), lambda p, bb: (bb, 0, col + p))
    in_specs = [seq(col_q), seq(col_k), seq(col_v),
                pl.BlockSpec((nb, TQ, s), lambda p, bb: (head0 // nb + p, 0, 0))]
    args = [qkv, qkv, qkv, strips]
    if mode == "diff":
        in_specs += [pl.BlockSpec((4, HEAD_DIM), lambda p, bb: (0, 0)),
                     pl.BlockSpec((1, LANES), lambda p, bb: (0, 0))]
        args += [lam_params.astype(F32), subln_g.reshape(1, LANES)]
    scratch = [pltpu.VMEM((2 if moba else 1, LANES, s), BF16),
               pltpu.VMEM((2, TQ, s), F32), pltpu.VMEM((2, TQ, s), BF16)]
    if mode == "pair":
        scratch += [pltpu.VMEM((2, s, LANES), BF16)]
    if moba:
        scratch += [pltpu.VMEM((2, LANES, s), F32)]
    return pl.pallas_call(
        functools.partial(_attn_kernel, mode=mode, moba=moba, lam_init=lam_init, s=s),
        grid=(n_blocks, b),
        in_specs=in_specs,
        out_specs=pl.BlockSpec((1, s, LANES), lambda p, bb: (bb, 0, p)),
        out_shape=jax.ShapeDtypeStruct((b, s, n_blocks * LANES), BF16),
        scratch_shapes=scratch,
        compiler_params=_params(2, VMEM_LIMIT),
        name="attn_" + mode + ("_moba" if moba else ""),
    )(*args)


def _mla_up_kernel(lat_ref, qg_ref, kvg_ref, wa_ref, wb_ref, wk_ref, wv_ref, cq_ref, sq_ref,
                   ck_ref, sk_ref, q_ref, k_ref, v_ref, *, per_seq):
    lat = lat_ref[...]
    tm = lat.shape[0]
    r0 = pl.multiple_of((pl.program_id(0) % per_seq) * tm, tm)
    cqn = _rms(lat[:, :MLA_Q_RANK], qg_ref[...]).astype(BF16)
    ckvn = _rms(lat[:, MLA_Q_RANK:MLA_Q_RANK + MLA_KV_RANK], kvg_ref[...]).astype(BF16)
    qa = jnp.dot(cqn, wa_ref[...], preferred_element_type=F32)
    qb = jnp.dot(cqn, wb_ref[...], preferred_element_type=F32)
    kn = jnp.dot(ckvn, wk_ref[...], preferred_element_type=F32).astype(BF16)
    v_ref[...] = jnp.dot(ckvn, wv_ref[...], preferred_element_type=F32).astype(BF16)
    x = lat[:, MLA_Q_RANK + MLA_KV_RANK:]
    kr = (x * ck_ref[pl.ds(r0, tm), :]
          + pltpu.roll(x, MLA_ROPE_DIM, 1) * sk_ref[pl.ds(r0, tm), :]).astype(BF16)
    cq = cq_ref[pl.ds(r0, tm), :]
    sq = sq_ref[pl.ds(r0, tm), :]
    hw = 2 * LANES
    for h in range(MLA_HEADS):
        q_ref[:, h * hw:(h + 1) * hw] = (qa[:, h * hw:(h + 1) * hw] * cq
                                         + qb[:, h * hw:(h + 1) * hw] * sq).astype(BF16)
        k_ref[:, h * hw:h * hw + LANES] = kn[:, h * LANES:(h + 1) * LANES]
        k_ref[:, h * hw + LANES:(h + 1) * hw] = kr


def mla_up(latent, q_norm, kv_norm, wa, wb, wk, wv, rope_tabs, s, tm=1024):
    m = latent.shape[0]
    cq, sq, ck, sk = rope_tabs
    per_seq = s // tm
    tab = lambda w: pl.BlockSpec((s, w), lambda i: (0, 0))
    full = lambda a: pl.BlockSpec(a.shape, lambda i: (0, 0))
    hw = 2 * LANES
    rows = lambda w: pl.BlockSpec((tm, w), lambda i: (i, 0))
    return pl.pallas_call(
        functools.partial(_mla_up_kernel, per_seq=per_seq),
        grid=(m // tm,),
        in_specs=[rows(latent.shape[1]),
                  pl.BlockSpec((1, MLA_Q_RANK), lambda i: (0, 0)),
                  pl.BlockSpec((1, MLA_KV_RANK), lambda i: (0, 0)),
                  full(wa), full(wb), full(wk), full(wv), tab(hw), tab(hw), tab(LANES), tab(LANES)],
        out_specs=[rows(MLA_HEADS * hw), rows(MLA_HEADS * hw), rows(MLA_HEADS * LANES)],
        out_shape=[jax.ShapeDtypeStruct((m, MLA_HEADS * hw), BF16),
                   jax.ShapeDtypeStruct((m, MLA_HEADS * hw), BF16),
                   jax.ShapeDtypeStruct((m, MLA_HEADS * LANES), BF16)],
        compiler_params=_params(1, VMEM_LIMIT),
        name="mla_up",
    )(latent, q_norm.reshape(1, -1), kv_norm.reshape(1, -1), wa, wb, wk, wv, cq, sq, ck, sk)


def _mla_attn_kernel(q_ref, k_ref, v_ref, o_ref, kt_ref, sc_ref, p_ref, *, s):
    scale = (MLA_NOPE_DIM + MLA_ROPE_DIM) ** -0.5 * LOG2E
    _store_transposed(kt_ref, 0, k_ref[0])
    row = lax.broadcasted_iota(jnp.int32, (TQ, TK), 0)
    col = lax.broadcasted_iota(jnp.int32, (TQ, TK), 1)
    causal = row >= col
    nblk = s // TQ

    def scores(i):
        nk = (i + 1) * TK
        q = q_ref[0, i * TQ:(i + 1) * TQ, :]
        sc = jnp.dot(q, kt_ref[0, :, 0:nk], preferred_element_type=F32) * scale
        if i > 0:
            sc_ref[i % 2, :, 0:nk - TK] = sc[:, :nk - TK]
        sc_ref[i % 2, :, nk - TK:nk] = jnp.where(causal, sc[:, nk - TK:], NEG)

    def softmax(i):
        nk = (i + 1) * TK
        m = jnp.max(sc_ref[i % 2, :, 0:nk], axis=1, keepdims=True)
        p = jnp.exp2(sc_ref[i % 2, :, 0:nk] - m)
        p_ref[i % 2, :, 0:nk] = p.astype(BF16)
        return jnp.sum(p, axis=1, keepdims=True)

    def weighted_values(i, l):
        nk = (i + 1) * TK
        o = jnp.dot(p_ref[i % 2, :, 0:nk], v_ref[0, 0:nk, :], preferred_element_type=F32)
        o_ref[0, i * TQ:(i + 1) * TQ, :] = (o / l).astype(o_ref.dtype)

    scores(0)
    l_prev = None
    for i in range(nblk):
        if i + 1 < nblk:
            scores(i + 1)
        l = softmax(i)
        if i > 0:
            weighted_values(i - 1, l_prev)
        l_prev = l
    weighted_values(nblk - 1, l_prev)


def mla_attention(q, k, v):
    b, s, _ = q.shape
    hw = 2 * LANES
    return pl.pallas_call(
        functools.partial(_mla_attn_kernel, s=s),
        grid=(MLA_HEADS, b),
        in_specs=[pl.BlockSpec((1, s, hw), lambda h, bb: (bb, 0, h)),
                  pl.BlockSpec((1, s, hw), lambda h, bb: (bb, 0, h)),
                  pl.BlockSpec((1, s, LANES), lambda h, bb: (bb, 0, h))],
        out_specs=pl.BlockSpec((1, s, LANES), lambda h, bb: (bb, 0, h)),
        out_shape=jax.ShapeDtypeStruct((b, s, MLA_HEADS * LANES), BF16),
        scratch_shapes=[pltpu.VMEM((1, hw, s), BF16),
                        pltpu.VMEM((2, TQ, s), F32), pltpu.VMEM((2, TQ, s), BF16)],
        compiler_params=_params(2, VMEM_LIMIT),
        name="mla_attn",
    )(q, k, v)


def _rope_tables(s):
    half = MLA_ROPE_DIM // 2
    freq = ROPE_THETA ** (-jnp.arange(half, dtype=F32) / half)
    ang = jnp.arange(s, dtype=F32)[:, None] * freq[None, :]
    cos, sin = jnp.cos(ang), jnp.sin(ang)
    cos2 = jnp.concatenate([cos, cos], axis=1)
    sin2 = jnp.concatenate([-sin, sin], axis=1)
    z64 = jnp.zeros((s, MLA_ROPE_DIM), F32)
    cq = jnp.concatenate([jnp.ones((s, MLA_NOPE_DIM), F32), cos2, z64], axis=1)
    sq = jnp.concatenate([jnp.zeros((s, MLA_NOPE_DIM), F32), sin2, z64], axis=1)
    ck = jnp.concatenate([cos2, z64], axis=1)
    sk = jnp.concatenate([sin2, z64], axis=1)
    return cq, sq, ck, sk


def _interleave_ug(w_in):
    d = w_in.shape[0]
    w = w_in.reshape(d, 2, D_FF // FF_CHUNK, FF_CHUNK).transpose(0, 2, 1, 3)
    return w.reshape(d, 2 * D_FF).astype(BF16)


def _swap_halves(w):
    half = w.shape[-1] // 2
    return jnp.concatenate([w[..., half:], w[..., :half]], axis=-1)


def _mla_q_weights(w_uq):
    r = w_uq.shape[0]
    w = w_uq.reshape(r, MLA_HEADS, MLA_NOPE_DIM + MLA_ROPE_DIM)
    nope, rope = w[..., :MLA_NOPE_DIM], w[..., MLA_NOPE_DIM:]
    z64 = jnp.zeros((r, MLA_HEADS, MLA_ROPE_DIM), w.dtype)
    z128 = jnp.zeros((r, MLA_HEADS, MLA_NOPE_DIM), w.dtype)
    wa = jnp.concatenate([nope, rope, z64], axis=-1).reshape(r, -1)
    wb = jnp.concatenate([z128, _swap_halves(rope), z64], axis=-1).reshape(r, -1)
    return wa.astype(BF16), wb.astype(BF16)


def kernel(x, rel_bias, even_norm1, even_w_in, diff_lambda, diff_subln, even_w_out,
           odd_norm1, odd_w_in, mla_q_norm, mla_w_uq, mla_kv_norm, mla_w_ukv, odd_w_out,
           ffn_norm, ffn_w_in, ffn_conv_w, ffn_conv_b, ffn_w_out, final_norm):
    b, s, d = x.shape
    m = b * s
    strips = bias_strips(rel_bias, s)
    rope_tabs = _rope_tables(s)
    qscale = HEAD_DIM ** -0.5 * LOG2E
    h = x.reshape(m, d)
    for layer in range(DEPTH):
        li = layer // 2
        if layer % 2 == 0:
            lam_init = 0.8 - 0.6 * math.exp(-0.3 * layer)
            w = even_w_in[li]
            mw, dw = MOBA_HEADS * HEAD_DIM, DIFF_HEADS * 2 * HEAD_DIM
            w = jnp.concatenate([w[:, :mw] * qscale, w[:, mw:3 * mw],
                                 w[:, 3 * mw:3 * mw + dw] * qscale, w[:, 3 * mw + dw:]],
                                axis=1).astype(BF16)
            (qkv,) = in_proj(h, even_norm1[li], w, [w.shape[1]], [BF16])
            qkv = qkv.reshape(b, s, -1)
            nb = mw // LANES
            o_a = attention(qkv, 0, nb, 2 * nb, nb, strips, 0, "pair", moba=True)
            o_b = attention(qkv, 3 * nb, 4 * nb, 5 * nb, DIFF_HEADS, strips, MOBA_HEADS, "diff",
                            lam_params=diff_lambda[li], subln_g=diff_subln[li], lam_init=lam_init)
            wo = even_w_out[li].astype(BF16)
        else:
            w = odd_w_in[li]
            dw = DIL_HEADS * HEAD_DIM
            lat0 = 3 * dw
            w_dil = jnp.concatenate([w[:, :dw] * qscale, w[:, dw:lat0]], axis=1).astype(BF16)
            kr_cols = w[:, lat0 + MLA_Q_RANK + MLA_KV_RANK:]
            w_lat = jnp.concatenate([w[:, lat0:], _swap_halves(kr_cols)], axis=1).astype(BF16)
            qkv, latent = in_proj(h, odd_norm1[li], jnp.concatenate([w_dil, w_lat], axis=1),
                                  [w_dil.shape[1], w_lat.shape[1]], [BF16, F32])
            qkv = qkv.reshape(b, s, -1)
            nb = dw // LANES
            o_a = attention(qkv, 0, nb, 2 * nb, nb, strips, MOBA_HEADS + DIFF_HEADS, "pair")
            wa, wb = _mla_q_weights(mla_w_uq[li])
            wkv = mla_w_ukv[li].reshape(MLA_KV_RANK, MLA_HEADS, MLA_NOPE_DIM + MLA_V_DIM)
            wk = wkv[..., :MLA_NOPE_DIM].reshape(MLA_KV_RANK, -1).astype(BF16)
            wv = wkv[..., MLA_NOPE_DIM:].reshape(MLA_KV_RANK, -1).astype(BF16)
            q_m, k_m, v_m = mla_up(latent, mla_q_norm[li], mla_kv_norm[li], wa, wb, wk, wv,
                                   rope_tabs, s)
            o_b = mla_attention(q_m.reshape(b, s, -1), k_m.reshape(b, s, -1),
                                v_m.reshape(b, s, -1))
            wo = odd_w_out[li].astype(BF16)
        wa_w = o_a.shape[-1]
        h = proj_residual([o_a.reshape(m, -1), o_b.reshape(m, -1)],
                          [wo[:wa_w], wo[wa_w:]], h)
        act = ffn_in(h.reshape(b, s, d), ffn_norm[layer], _interleave_ug(ffn_w_in[layer]),
                     ffn_conv_w[layer], ffn_conv_b[layer])
        h = proj_residual([act.reshape(m, D_FF)], [ffn_w_out[layer].astype(BF16)], h,
                          final_g=final_norm if layer == DEPTH - 1 else None)
    return h.reshape(b, s, d)
```

```python
import functools
import math

import numpy as np
import jax
import jax.numpy as jnp
from jax import lax
from jax.experimental import pallas as pl
from jax.experimental.pallas import tpu as pltpu

D_MODEL = 1024
DEPTH = 4
HEAD_DIM = 64
MOBA_HEADS = 8
MOBA_BLOCK = 256
MOBA_TOPK = 3
DIFF_HEADS = 4
DIL_HEADS = 8
DIL_CONFIGS = ((128, 1), (512, 4), (2048, 16))
MLA_HEADS = 4
MLA_Q_RANK = 256
MLA_KV_RANK = 128
MLA_NOPE_DIM = 128
MLA_ROPE_DIM = 64
MLA_V_DIM = 128
ROPE_THETA = 10000.0
REL_BUCKETS = 32
REL_MAX_DIST = 1024
N_BIAS_HEADS = MOBA_HEADS + DIFF_HEADS + DIL_HEADS
D_FF = 2816
EPS = 1e-6
NEG = -1e30
LOG2E = math.log2(math.e)

LANES = 128
TQ = 256
TK = 256
FF_CHUNK = 256
FF_ROWS = 512
FF_TAIL_ROWS = 128
VMEM_LIMIT = 56 * 1024 * 1024

F32 = jnp.float32
BF16 = jnp.bfloat16


def _params(n_axes, vmem=None):
    return pltpu.CompilerParams(dimension_semantics=("arbitrary",) * n_axes,
                                vmem_limit_bytes=vmem)


def _rms(x, g):
    ms = jnp.mean(x * x, axis=-1, keepdims=True)
    return x * lax.rsqrt(ms + EPS) * g


def _in_proj_kernel(x_ref, g_ref, w_ref, *o_refs, row_chunk):
    tm = x_ref.shape[0]
    g = g_ref[...]
    for r in range(tm // row_chunk):
        rows = slice(r * row_chunk, (r + 1) * row_chunk)
        xn = _rms(x_ref[rows, :], g).astype(BF16)
        acc = jnp.dot(xn, w_ref[...], preferred_element_type=F32)
        col = 0
        for o_ref in o_refs:
            width = o_ref.shape[1]
            o_ref[rows, :] = acc[:, col:col + width].astype(o_ref.dtype)
            col += width


def in_proj(x, g, w, out_widths, out_dtypes, tm=1024, row_chunk=512):
    m, d = x.shape
    return pl.pallas_call(
        functools.partial(_in_proj_kernel, row_chunk=row_chunk),
        grid=(m // tm,),
        in_specs=[pl.BlockSpec((tm, d), lambda i: (i, 0)),
                  pl.BlockSpec((1, d), lambda i: (0, 0)),
                  pl.BlockSpec(w.shape, lambda i: (0, 0))],
        out_specs=[pl.BlockSpec((tm, n), lambda i: (i, 0)) for n in out_widths],
        out_shape=[jax.ShapeDtypeStruct((m, n), dt) for n, dt in zip(out_widths, out_dtypes)],
        compiler_params=_params(1, VMEM_LIMIT),
        name="in_proj",
    )(x, g.reshape(1, d), w)


def _proj_res_kernel(*refs, n_in, final_norm):
    a_refs = refs[:n_in]
    w_refs = refs[n_in:2 * n_in]
    res_ref = refs[2 * n_in]
    o_ref = refs[-1]
    acc = res_ref[...]
    for a_ref, w_ref in zip(a_refs, w_refs):
        acc = acc + jnp.dot(a_ref[...], w_ref[...], preferred_element_type=F32)
    if final_norm:
        acc = _rms(acc, refs[2 * n_in + 1][...])
    o_ref[...] = acc


def proj_residual(acts, ws, res, final_g=None, tm=512):
    m, d = res.shape
    n_in = len(acts)
    in_specs = [pl.BlockSpec((tm, a.shape[1]), lambda i: (i, 0)) for a in acts]
    in_specs += [pl.BlockSpec(w.shape, lambda i: (0, 0)) for w in ws]
    in_specs += [pl.BlockSpec((tm, d), lambda i: (i, 0))]
    args = list(acts) + list(ws) + [res]
    if final_g is not None:
        in_specs += [pl.BlockSpec((1, d), lambda i: (0, 0))]
        args += [final_g.reshape(1, d)]
    return pl.pallas_call(
        functools.partial(_proj_res_kernel, n_in=n_in, final_norm=final_g is not None),
        grid=(m // tm,),
        in_specs=in_specs,
        out_specs=pl.BlockSpec((tm, d), lambda i: (i, 0)),
        out_shape=jax.ShapeDtypeStruct((m, d), F32),
        compiler_params=_params(1, VMEM_LIMIT),
        name="proj_residual",
    )(*args)


def _row_chunks(s):
    chunks, start, rows = [], 0, FF_ROWS
    while start < s:
        while rows > FF_TAIL_ROWS and s - start - rows < rows // 2:
            rows //= 2
        rows = min(rows, s - start)
        chunks.append((start, rows))
        start += rows
    return chunks


def _ffn_in_kernel(x_ref, g_ref, wu_ref, wg_ref, cw_ref, cb_ref, o_ref, xn_ref):
    @pl.when(pl.program_id(1) == 0)
    def _():
        xn_ref[...] = _rms(x_ref[0], g_ref[...]).astype(BF16)

    s = xn_ref.shape[0]
    cw = cw_ref[...]
    cb = cb_ref[...]
    sub = lax.broadcasted_iota(jnp.int32, (8, FF_CHUNK), 0)
    prev1 = prev2 = jnp.zeros((8, FF_CHUNK), F32)
    for start, rows in _row_chunks(s):
        xn = xn_ref[start:start + rows, :]
        u = jnp.dot(xn, wu_ref[...], preferred_element_type=F32)
        gt = jnp.dot(xn, wg_ref[...], preferred_element_type=F32)
        r1 = pltpu.roll(gt, 1, 0)
        r2 = pltpu.roll(gt, 2, 0)
        g1 = jnp.concatenate([jnp.where(sub >= 1, r1[:8], prev1), r1[8:]], axis=0)
        g2 = jnp.concatenate([jnp.where(sub >= 2, r2[:8], prev2), r2[8:]], axis=0)
        prev1, prev2 = r1[:8], r2[:8]
        z = cw[2:3] * gt + cw[1:2] * g1 + cw[0:1] * g2 + cb
        gelu = 0.5 * z * (1.0 + lax.erf(z * math.sqrt(0.5)))
        o_ref[0, start:start + rows, :] = (gelu * u).astype(BF16)


def ffn_in(h3, g, w_in, conv_w, conv_b):
    b, s, d = h3.shape
    nc = D_FF // FF_CHUNK
    return pl.pallas_call(
        _ffn_in_kernel,
        grid=(b, nc),
        in_specs=[pl.BlockSpec((1, s, d), lambda i, c: (i, 0, 0)),
                  pl.BlockSpec((1, d), lambda i, c: (0, 0)),
                  pl.BlockSpec((d, FF_CHUNK), lambda i, c: (0, c)),
                  pl.BlockSpec((d, FF_CHUNK), lambda i, c: (0, c + nc)),
                  pl.BlockSpec((3, FF_CHUNK), lambda i, c: (0, c)),
                  pl.BlockSpec((1, FF_CHUNK), lambda i, c: (0, c))],
        out_specs=pl.BlockSpec((1, s, FF_CHUNK), lambda i, c: (i, 0, c)),
        out_shape=jax.ShapeDtypeStruct((b, s, D_FF), BF16),
        scratch_shapes=[pltpu.VMEM((s, d), BF16)],
        compiler_params=_params(2, VMEM_LIMIT),
        name="ffn_in",
    )(h3, g.reshape(1, d), w_in, w_in, conv_w, conv_b.reshape(1, D_FF))


def _bucket_of_distance(s):
    max_exact = REL_BUCKETS // 2
    n_large = REL_BUCKETS - max_exact
    thresholds = []
    for k in range(1, n_large):
        t = max_exact * (REL_MAX_DIST / max_exact) ** (k / n_large)
        ti = int(round(t))
        thresholds.append(ti if abs(t - ti) < 1e-9 else int(math.ceil(t)))
    d = np.arange(s)
    large = max_exact + sum((d >= t).astype(np.int64) for t in thresholds)
    return np.where(d < max_exact, d, np.minimum(large, REL_BUCKETS - 1)).astype(np.int32)


def _dilated_log_multiplicity(s):
    d = np.arange(s)
    count = np.zeros(s, np.int64)
    for window, dil in DIL_CONFIGS:
        count += ((d % dil == 0) & (d // dil <= window // dil)).astype(np.int64)
    return np.where(count > 0, np.log(np.maximum(count, 1)), NEG).astype(np.float32)


def _bias_strip_kernel(w_ref, o_ref, *, s):
    tile = jnp.broadcast_to(w_ref[0], (TQ, s + TQ))
    tile = pltpu.roll(tile, 0, 1, stride=1, stride_axis=0)
    o_ref[0] = tile[:, TQ:]


def bias_strips(rel_bias, s):
    bucket = _bucket_of_distance(s)
    per_dist = jnp.take(rel_bias.T.astype(F32), jnp.asarray(bucket), axis=1)
    logmult = jnp.asarray(_dilated_log_multiplicity(s))
    dil0 = MOBA_HEADS + DIFF_HEADS
    covered = logmult > 0.5 * NEG
    dil_rows = jnp.where(covered[None, :], per_dist[dil0:] + logmult[None, :], NEG)
    per_dist = jnp.concatenate([per_dist[:dil0], dil_rows], axis=0) * LOG2E
    nh = per_dist.shape[0]
    w = jnp.concatenate([jnp.full((nh, 1), NEG, F32), per_dist[:, ::-1],
                         jnp.full((nh, TQ - 1), NEG, F32)], axis=1).reshape(nh, 1, s + TQ)
    return pl.pallas_call(
        functools.partial(_bias_strip_kernel, s=s),
        grid=(nh,),
        in_specs=[pl.BlockSpec((1, 1, s + TQ), lambda h: (h, 0, 0))],
        out_specs=pl.BlockSpec((1, TQ, s), lambda h: (h, 0, 0)),
        out_shape=jax.ShapeDtypeStruct((nh, TQ, s), F32),
        compiler_params=_params(1),
        name="bias_strips",
    )(w)


def _store_transposed(kt_ref, idx, k):
    s = k.shape[0]
    for j in range(s // TK):
        kt_ref[idx, :, j * TK:(j + 1) * TK] = k[j * TK:(j + 1) * TK, :].T


def _attn_kernel(*refs, mode, moba, lam_init, s):
    if mode == "diff":
        q_ref, k_ref, v_ref, b_ref, lam_ref, sg_ref, o_ref, kt_ref, sc_ref, p_ref = refs
    elif moba:
        q_ref, k_ref, v_ref, b_ref, o_ref, kt_ref, sc_ref, p_ref, vone_ref, drop_ref = refs
    else:
        q_ref, k_ref, v_ref, b_ref, o_ref, kt_ref, sc_ref, p_ref, vone_ref = refs
    nblk = s // TK
    lane = lax.broadcasted_iota(jnp.int32, (TQ, LANES), 1)
    halves = (lane < HEAD_DIM, lane >= HEAD_DIM)
    nt = (((1,), (1,)), ((), ()))

    if moba:
        k_all = k_ref[0]
        q_all = q_ref[0]
        rowblk = lax.broadcasted_iota(jnp.int32, (s, LANES), 0) // TK
        lane_s = lax.broadcasted_iota(jnp.int32, (s, LANES), 1)
        kmean = jnp.mean(k_all.astype(F32).reshape(nblk, TK, LANES), axis=1)
        k_hi = kmean.astype(BF16)
        rem = kmean - k_hi.astype(F32)
        k_mid = rem.astype(BF16)
        k_lo = (rem - k_mid.astype(F32)).astype(BF16)
        kmean3 = jnp.concatenate([k_hi, k_mid, k_lo], axis=1)
        qblk = lax.broadcasted_iota(jnp.int32, (nblk, s), 1) // TQ
        blk = lax.broadcasted_iota(jnp.int32, (nblk, s), 0)
        for c in range(2):
            base = HEAD_DIM if c == 0 else 0
            in_half = (lane_s < HEAD_DIM) if c == 0 else (lane_s >= HEAD_DIM)
            onehot = jnp.where(lane_s - base == rowblk, 1.0, 0.0).astype(BF16)
            _store_transposed(kt_ref, c, jnp.where(in_half, k_all, onehot))
            qh = jnp.where(in_half, q_all, jnp.zeros_like(q_all))
            gate = lax.dot_general(kmean3, jnp.concatenate([qh, qh, qh], axis=1), nt,
                                   preferred_element_type=F32)
            rank = jnp.zeros((nblk, s), jnp.int32)
            for n in range(nblk - 1):
                gn = gate[n:n + 1, :]
                beats = jnp.where(gn > gate, 1, jnp.where(gn == gate, jnp.where(n < blk, 1, 0), 0))
                rank = rank + jnp.where(n < qblk, beats, 0)
            drop = jnp.where(blk < qblk, jnp.where(rank >= MOBA_TOPK, NEG, 0.0), 0.0)
            drop_ref[c] = jnp.zeros((LANES, s), F32)
            drop_ref[c, base:base + nblk, :] = drop

    else:
        _store_transposed(kt_ref, 0, k_ref[0])

    if mode == "diff":
        lp = lam_ref[...]
        lam = (jnp.exp(jnp.sum(lp[0:1] * lp[1:2], axis=1, keepdims=True))
               - jnp.exp(jnp.sum(lp[2:3] * lp[3:4], axis=1, keepdims=True)) + lam_init)

    units = [(i, c) for i in range(nblk) for c in range(2)]

    if mode == "pair":
        v_all = v_ref[0]
        lane_s = lax.broadcasted_iota(jnp.int32, (s, LANES), 1)
        ones = jnp.ones_like(v_all)
        vone_ref[0] = jnp.where(lane_s < HEAD_DIM, v_all, ones)
        vone_ref[1] = jnp.where(lane_s >= HEAD_DIM, v_all, ones)

    def scores(u):
        i, c = units[u]
        nk = (i + 1) * TK
        q = q_ref[0, i * TQ:(i + 1) * TQ, :]
        qc = jnp.where(halves[c], q, jnp.zeros_like(q))
        if moba and i > 0:
            sel = drop_ref[c, :, i * TQ:(i + 1) * TQ].T
            qc = jnp.where(halves[c], q, sel.astype(BF16))
        sc_ref[u % 2, :, 0:nk] = (jnp.dot(qc, kt_ref[c if moba else 0, :, 0:nk],
                                          preferred_element_type=F32)
                                  + b_ref[c if mode == "pair" else 0, :, s - nk:])

    def softmax(u):
        nk = (units[u][0] + 1) * TK
        m = jnp.max(sc_ref[u % 2, :, 0:nk], axis=1, keepdims=True)
        p = jnp.exp2(sc_ref[u % 2, :, 0:nk] - m)
        p_ref[u % 2, :, 0:nk] = p.astype(BF16)
        return None if mode == "pair" else jnp.sum(p, axis=1, keepdims=True)

    outs = []

    def weighted_values(u, l):
        i, c = units[u]
        nk = (i + 1) * TK
        if mode == "pair":
            o = jnp.dot(p_ref[u % 2, :, 0:nk], vone_ref[c, 0:nk, :], preferred_element_type=F32)
            l = pltpu.roll(o, HEAD_DIM, 1)
        else:
            o = jnp.dot(p_ref[u % 2, :, 0:nk], v_ref[0, 0:nk, :], preferred_element_type=F32)
        outs.append(o / l)
        if c == 1:
            if mode == "pair":
                res = jnp.where(halves[0], outs[0], outs[1])
            else:
                res = _rms(outs[0] - lam * outs[1], sg_ref[...]) * (1.0 - lam_init)
            o_ref[0, i * TQ:(i + 1) * TQ, :] = res.astype(o_ref.dtype)
            outs.clear()

    scores(0)
    l_prev = None
    for u in range(len(units)):
        if u + 1 < len(units):
            scores(u + 1)
        l = softmax(u)
        if u > 0:
            weighted_values(u - 1, l_prev)
        l_prev = l
    weighted_values(len(units) - 1, l_prev)


def attention(qkv, col_q, col_k, col_v, n_blocks, strips, head0, mode, moba=False,
              lam_params=None, subln_g=None, lam_init=0.0):
    b, s, _ = qkv.shape
    nb = 2 if mode == "pair" else 1
    seq = lambda col: pl.BlockSpec((1, s, LANES), lambda p, bb: (bb, 0, col + p))
    in_specs = [seq(col_q), seq(col_k), seq(col_v),
                pl.BlockSpec((nb, TQ, s), lambda p, bb: (head0 // nb + p, 0, 0))]
    args = [qkv, qkv, qkv, strips]
    if mode == "diff":
        in_specs += [pl.BlockSpec((4, HEAD_DIM), lambda p, bb: (0, 0)),
                     pl.BlockSpec((1, LANES), lambda p, bb: (0, 0))]
        args += [lam_params.astype(F32), subln_g.reshape(1, LANES)]
    scratch = [pltpu.VMEM((2 if moba else 1, LANES, s), BF16),
               pltpu.VMEM((2, TQ, s), F32), pltpu.VMEM((2, TQ, s), BF16)]
    if mode == "pair":
        scratch += [pltpu.VMEM((2, s, LANES), BF16)]
    if moba:
        scratch += [pltpu.VMEM((2, LANES, s), F32)]
    return pl.pallas_call(
        functools.partial(_attn_kernel, mode=mode, moba=moba, lam_init=lam_init, s=s),
        grid=(n_blocks, b),
        in_specs=in_specs,
        out_specs=pl.BlockSpec((1, s, LANES), lambda p, bb: (bb, 0, p)),
        out_shape=jax.ShapeDtypeStruct((b, s, n_blocks * LANES), BF16),
        scratch_shapes=scratch,
        compiler_params=_params(2, VMEM_LIMIT),
        name="attn_" + mode + ("_moba" if moba else ""),
    )(*args)


def _mla_up_kernel(lat_ref, qg_ref, kvg_ref, wa_ref, wb_ref, wk_ref, wv_ref, cq_ref, sq_ref,
                   ck_ref, sk_ref, q_ref, k_ref, v_ref, *, per_seq):
    lat = lat_ref[...]
    tm = lat.shape[0]
    r0 = pl.multiple_of((pl.program_id(0) % per_seq) * tm, tm)
    cqn = _rms(lat[:, :MLA_Q_RANK], qg_ref[...]).astype(BF16)
    ckvn = _rms(lat[:, MLA_Q_RANK:MLA_Q_RANK + MLA_KV_RANK], kvg_ref[...]).astype(BF16)
    qa = jnp.dot(cqn, wa_ref[...], preferred_element_type=F32)
    qb = jnp.dot(cqn, wb_ref[...], preferred_element_type=F32)
    kn = jnp.dot(ckvn, wk_ref[...], preferred_element_type=F32).astype(BF16)
    v_ref[...] = jnp.dot(ckvn, wv_ref[...], preferred_element_type=F32).astype(BF16)
    x = lat[:, MLA_Q_RANK + MLA_KV_RANK:]
    kr = (x * ck_ref[pl.ds(r0, tm), :]
          + pltpu.roll(x, MLA_ROPE_DIM, 1) * sk_ref[pl.ds(r0, tm), :]).astype(BF16)
    cq = cq_ref[pl.ds(r0, tm), :]
    sq = sq_ref[pl.ds(r0, tm), :]
    hw = 2 * LANES
    for h in range(MLA_HEADS):
        q_ref[:, h * hw:(h + 1) * hw] = (qa[:, h * hw:(h + 1) * hw] * cq
                                         + qb[:, h * hw:(h + 1) * hw] * sq).astype(BF16)
        k_ref[:, h * hw:h * hw + LANES] = kn[:, h * LANES:(h + 1) * LANES]
        k_ref[:, h * hw + LANES:(h + 1) * hw] = kr


def mla_up(latent, q_norm, kv_norm, wa, wb, wk, wv, rope_tabs, s, tm=1024):
    m = latent.shape[0]
    cq, sq, ck, sk = rope_tabs
    per_seq = s // tm
    tab = lambda w: pl.BlockSpec((s, w), lambda i: (0, 0))
    full = lambda a: pl.BlockSpec(a.shape, lambda i: (0, 0))
    hw = 2 * LANES
    rows = lambda w: pl.BlockSpec((tm, w), lambda i: (i, 0))
    return pl.pallas_call(
        functools.partial(_mla_up_kernel, per_seq=per_seq),
        grid=(m // tm,),
        in_specs=[rows(latent.shape[1]),
                  pl.BlockSpec((1, MLA_Q_RANK), lambda i: (0, 0)),
                  pl.BlockSpec((1, MLA_KV_RANK), lambda i: (0, 0)),
                  full(wa), full(wb), full(wk), full(wv), tab(hw), tab(hw), tab(LANES), tab(LANES)],
        out_specs=[rows(MLA_HEADS * hw), rows(MLA_HEADS * hw), rows(MLA_HEADS * LANES)],
        out_shape=[jax.ShapeDtypeStruct((m, MLA_HEADS * hw), BF16),
                   jax.ShapeDtypeStruct((m, MLA_HEADS * hw), BF16),
                   jax.ShapeDtypeStruct((m, MLA_HEADS * LANES), BF16)],
        compiler_params=_params(1, VMEM_LIMIT),
        name="mla_up",
    )(latent, q_norm.reshape(1, -1), kv_norm.reshape(1, -1), wa, wb, wk, wv, cq, sq, ck, sk)


def _mla_attn_kernel(q_ref, k_ref, v_ref, o_ref, kt_ref, sc_ref, p_ref, *, s):
    scale = (MLA_NOPE_DIM + MLA_ROPE_DIM) ** -0.5 * LOG2E
    _store_transposed(kt_ref, 0, k_ref[0])
    row = lax.broadcasted_iota(jnp.int32, (TQ, TK), 0)
    col = lax.broadcasted_iota(jnp.int32, (TQ, TK), 1)
    causal = row >= col
    nblk = s // TQ

    def scores(i):
        nk = (i + 1) * TK
        q = q_ref[0, i * TQ:(i + 1) * TQ, :]
        sc = jnp.dot(q, kt_ref[0, :, 0:nk], preferred_element_type=F32) * scale
        if i > 0:
            sc_ref[i % 2, :, 0:nk - TK] = sc[:, :nk - TK]
        sc_ref[i % 2, :, nk - TK:nk] = jnp.where(causal, sc[:, nk - TK:], NEG)

    def softmax(i):
        nk = (i + 1) * TK
        m = jnp.max(sc_ref[i % 2, :, 0:nk], axis=1, keepdims=True)
        p = jnp.exp2(sc_ref[i % 2, :, 0:nk] - m)
        p_ref[i % 2, :, 0:nk] = p.astype(BF16)
        return jnp.sum(p, axis=1, keepdims=True)

    def weighted_values(i, l):
        nk = (i + 1) * TK
        o = jnp.dot(p_ref[i % 2, :, 0:nk], v_ref[0, 0:nk, :], preferred_element_type=F32)
        o_ref[0, i * TQ:(i + 1) * TQ, :] = (o / l).astype(o_ref.dtype)

    scores(0)
    l_prev = None
    for i in range(nblk):
        if i + 1 < nblk:
            scores(i + 1)
        l = softmax(i)
        if i > 0:
            weighted_values(i - 1, l_prev)
        l_prev = l
    weighted_values(nblk - 1, l_prev)


def mla_attention(q, k, v):
    b, s, _ = q.shape
    hw = 2 * LANES
    return pl.pallas_call(
        functools.partial(_mla_attn_kernel, s=s),
        grid=(MLA_HEADS, b),
        in_specs=[pl.BlockSpec((1, s, hw), lambda h, bb: (bb, 0, h)),
                  pl.BlockSpec((1, s, hw), lambda h, bb: (bb, 0, h)),
                  pl.BlockSpec((1, s, LANES), lambda h, bb: (bb, 0, h))],
        out_specs=pl.BlockSpec((1, s, LANES), lambda h, bb: (bb, 0, h)),
        out_shape=jax.ShapeDtypeStruct((b, s, MLA_HEADS * LANES), BF16),
        scratch_shapes=[pltpu.VMEM((1, hw, s), BF16),
                        pltpu.VMEM((2, TQ, s), F32), pltpu.VMEM((2, TQ, s), BF16)],
        compiler_params=_params(2, VMEM_LIMIT),
        name="mla_attn",
    )(q, k, v)


def _rope_tables(s):
    half = MLA_ROPE_DIM // 2
    freq = ROPE_THETA ** (-jnp.arange(half, dtype=F32) / half)
    ang = jnp.arange(s, dtype=F32)[:, None] * freq[None, :]
    cos, sin = jnp.cos(ang), jnp.sin(ang)
    cos2 = jnp.concatenate([cos, cos], axis=1)
    sin2 = jnp.concatenate([-sin, sin], axis=1)
    z64 = jnp.zeros((s, MLA_ROPE_DIM), F32)
    cq = jnp.concatenate([jnp.ones((s, MLA_NOPE_DIM), F32), cos2, z64], axis=1)
    sq = jnp.concatenate([jnp.zeros((s, MLA_NOPE_DIM), F32), sin2, z64], axis=1)
    ck = jnp.concatenate([cos2, z64], axis=1)
    sk = jnp.concatenate([sin2, z64], axis=1)
    return cq, sq, ck, sk


def _swap_halves(w):
    half = w.shape[-1] // 2
    return jnp.concatenate([w[..., half:], w[..., :half]], axis=-1)


def _mla_q_weights(w_uq):
    r = w_uq.shape[0]
    w = w_uq.reshape(r, MLA_HEADS, MLA_NOPE_DIM + MLA_ROPE_DIM)
    nope, rope = w[..., :MLA_NOPE_DIM], w[..., MLA_NOPE_DIM:]
    z64 = jnp.zeros((r, MLA_HEADS, MLA_ROPE_DIM), w.dtype)
    z128 = jnp.zeros((r, MLA_HEADS, MLA_NOPE_DIM), w.dtype)
    wa = jnp.concatenate([nope, rope, z64], axis=-1).reshape(r, -1)
    wb = jnp.concatenate([z128, _swap_halves(rope), z64], axis=-1).reshape(r, -1)
    return wa.astype(BF16), wb.astype(BF16)


def kernel(x, rel_bias, even_norm1, even_w_in, diff_lambda, diff_subln, even_w_out,
           odd_norm1, odd_w_in, mla_q_norm, mla_w_uq, mla_kv_norm, mla_w_ukv, odd_w_out,
           ffn_norm, ffn_w_in, ffn_conv_w, ffn_conv_b, ffn_w_out, final_norm):
    b, s, d = x.shape
    m = b * s
    strips = bias_strips(rel_bias, s)
    rope_tabs = _rope_tables(s)
    qscale = HEAD_DIM ** -0.5 * LOG2E
    h = x.reshape(m, d)
    for layer in range(DEPTH):
        li = layer // 2
        if layer % 2 == 0:
            lam_init = 0.8 - 0.6 * math.exp(-0.3 * layer)
            w = even_w_in[li]
            mw, dw = MOBA_HEADS * HEAD_DIM, DIFF_HEADS * 2 * HEAD_DIM
            w = jnp.concatenate([w[:, :mw] * qscale, w[:, mw:3 * mw],
                                 w[:, 3 * mw:3 * mw + dw] * qscale, w[:, 3 * mw + dw:]],
                                axis=1).astype(BF16)
            (qkv,) = in_proj(h, even_norm1[li], w, [w.shape[1]], [BF16])
            qkv = qkv.reshape(b, s, -1)
            nb = mw // LANES
            o_a = attention(qkv, 0, nb, 2 * nb, nb, strips, 0, "pair", moba=True)
            o_b = attention(qkv, 3 * nb, 4 * nb, 5 * nb, DIFF_HEADS, strips, MOBA_HEADS, "diff",
                            lam_params=diff_lambda[li], subln_g=diff_subln[li], lam_init=lam_init)
            wo = even_w_out[li].astype(BF16)
        else:
            w = odd_w_in[li]
            dw = DIL_HEADS * HEAD_DIM
            lat0 = 3 * dw
            w_dil = jnp.concatenate([w[:, :dw] * qscale, w[:, dw:lat0]], axis=1).astype(BF16)
            kr_cols = w[:, lat0 + MLA_Q_RANK + MLA_KV_RANK:]
            w_lat = jnp.concatenate([w[:, lat0:], _swap_halves(kr_cols)], axis=1).astype(BF16)
            qkv, latent = in_proj(h, odd_norm1[li], jnp.concatenate([w_dil, w_lat], axis=1),
                                  [w_dil.shape[1], w_lat.shape[1]], [BF16, F32])
            qkv = qkv.reshape(b, s, -1)
            nb = dw // LANES
            o_a = attention(qkv, 0, nb, 2 * nb, nb, strips, MOBA_HEADS + DIFF_HEADS, "pair")
            wa, wb = _mla_q_weights(mla_w_uq[li])
            wkv = mla_w_ukv[li].reshape(MLA_KV_RANK, MLA_HEADS, MLA_NOPE_DIM + MLA_V_DIM)
            wk = wkv[..., :MLA_NOPE_DIM].reshape(MLA_KV_RANK, -1).astype(BF16)
            wv = wkv[..., MLA_NOPE_DIM:].reshape(MLA_KV_RANK, -1).astype(BF16)
            q_m, k_m, v_m = mla_up(latent, mla_q_norm[li], mla_kv_norm[li], wa, wb, wk, wv,
                                   rope_tabs, s)
            o_b = mla_attention(q_m.reshape(b, s, -1), k_m.reshape(b, s, -1),
                                v_m.reshape(b, s, -1))
            wo = odd_w_out[li].astype(BF16)
        wa_w = o_a.shape[-1]
        h = proj_residual([o_a.reshape(m, -1), o_b.reshape(m, -1)],
                          [wo[:wa_w], wo[wa_w:]], h)
        act = ffn_in(h.reshape(b, s, d), ffn_norm[layer], ffn_w_in[layer].astype(BF16),
                     ffn_conv_w[layer], ffn_conv_b[layer])
        h = proj_residual([act.reshape(m, D_FF)], [ffn_w_out[layer].astype(BF16)], h,
                          final_g=final_norm if layer == DEPTH - 1 else None)
    return h.reshape(b, s, d)
```

```python
import functools
import math

import numpy as np
import jax
import jax.numpy as jnp
from jax import lax
from jax.experimental import pallas as pl
from jax.experimental.pallas import tpu as pltpu

D_MODEL = 1024
DEPTH = 4
HEAD_DIM = 64
MOBA_HEADS = 8
MOBA_BLOCK = 256
MOBA_TOPK = 3
DIFF_HEADS = 4
DIL_HEADS = 8
DIL_CONFIGS = ((128, 1), (512, 4), (2048, 16))
MLA_HEADS = 4
MLA_Q_RANK = 256
MLA_KV_RANK = 128
MLA_NOPE_DIM = 128
MLA_ROPE_DIM = 64
MLA_V_DIM = 128
ROPE_THETA = 10000.0
REL_BUCKETS = 32
REL_MAX_DIST = 1024
N_BIAS_HEADS = MOBA_HEADS + DIFF_HEADS + DIL_HEADS
D_FF = 2816
EPS = 1e-6
NEG = -1e30
LOG2E = math.log2(math.e)

LANES = 128
TQ = 256
TK = 256
FF_CHUNK = 256
FF_ROWS = 512
VMEM_LIMIT = 56 * 1024 * 1024

F32 = jnp.float32
BF16 = jnp.bfloat16


def _params(n_axes, vmem=None):
    return pltpu.CompilerParams(dimension_semantics=("arbitrary",) * n_axes,
                                vmem_limit_bytes=vmem)


def _rms(x, g):
    ms = jnp.mean(x * x, axis=-1, keepdims=True)
    return x * lax.rsqrt(ms + EPS) * g


def _in_proj_kernel(x_ref, g_ref, w_ref, *o_refs, row_chunk):
    tm = x_ref.shape[0]
    g = g_ref[...]
    for r in range(tm // row_chunk):
        rows = slice(r * row_chunk, (r + 1) * row_chunk)
        xn = _rms(x_ref[rows, :], g).astype(BF16)
        acc = jnp.dot(xn, w_ref[...], preferred_element_type=F32)
        col = 0
        for o_ref in o_refs:
            width = o_ref.shape[1]
            o_ref[rows, :] = acc[:, col:col + width].astype(o_ref.dtype)
            col += width


def in_proj(x, g, w, out_widths, out_dtypes, tm=1024, row_chunk=512):
    m, d = x.shape
    return pl.pallas_call(
        functools.partial(_in_proj_kernel, row_chunk=row_chunk),
        grid=(m // tm,),
        in_specs=[pl.BlockSpec((tm, d), lambda i: (i, 0)),
                  pl.BlockSpec((1, d), lambda i: (0, 0)),
                  pl.BlockSpec(w.shape, lambda i: (0, 0))],
        out_specs=[pl.BlockSpec((tm, n), lambda i: (i, 0)) for n in out_widths],
        out_shape=[jax.ShapeDtypeStruct((m, n), dt) for n, dt in zip(out_widths, out_dtypes)],
        compiler_params=_params(1, VMEM_LIMIT),
        name="in_proj",
    )(x, g.reshape(1, d), w)


def _proj_res_kernel(*refs, n_in, final_norm):
    a_refs = refs[:n_in]
    w_ref, res_ref = refs[n_in], refs[n_in + 1]
    o_ref, wb_ref = refs[-2], refs[-1]

    @pl.when(pl.program_id(0) == 0)
    def _():
        wb_ref[...] = w_ref[...].astype(BF16)

    acc = res_ref[...]
    row = 0
    for a_ref in a_refs:
        k = a_ref.shape[1]
        acc = acc + jnp.dot(a_ref[...], wb_ref[row:row + k, :], preferred_element_type=F32)
        row += k
    if final_norm:
        acc = _rms(acc, refs[n_in + 2][...])
    o_ref[...] = acc


def proj_residual(acts, w, res, final_g=None, tm=512):
    m, d = res.shape
    n_in = len(acts)
    in_specs = [pl.BlockSpec((tm, a.shape[1]), lambda i: (i, 0)) for a in acts]
    in_specs += [pl.BlockSpec(w.shape, lambda i: (0, 0)),
                 pl.BlockSpec((tm, d), lambda i: (i, 0))]
    args = list(acts) + [w, res]
    if final_g is not None:
        in_specs += [pl.BlockSpec((1, d), lambda i: (0, 0))]
        args += [final_g.reshape(1, d)]
    return pl.pallas_call(
        functools.partial(_proj_res_kernel, n_in=n_in, final_norm=final_g is not None),
        grid=(m // tm,),
        in_specs=in_specs,
        out_specs=pl.BlockSpec((tm, d), lambda i: (i, 0)),
        out_shape=jax.ShapeDtypeStruct((m, d), F32),
        scratch_shapes=[pltpu.VMEM(w.shape, BF16)],
        compiler_params=_params(1, VMEM_LIMIT),
        name="proj_residual",
    )(*args)


def _ffn_in_kernel(x_ref, g_ref, wu_ref, wg_ref, cw_ref, cb_ref, o_ref, xn_ref):
    @pl.when(pl.program_id(1) == 0)
    def _():
        xn_ref[...] = _rms(x_ref[0], g_ref[...]).astype(BF16)

    s = xn_ref.shape[0]
    cw = cw_ref[...]
    cb = cb_ref[...]
    sub = lax.broadcasted_iota(jnp.int32, (8, FF_CHUNK), 0)
    prev1 = prev2 = jnp.zeros((8, FF_CHUNK), F32)
    wu = wu_ref[...].astype(BF16)
    wg = wg_ref[...].astype(BF16)
    for r in range(s // FF_ROWS):
        xn = xn_ref[r * FF_ROWS:(r + 1) * FF_ROWS, :]
        u = jnp.dot(xn, wu, preferred_element_type=F32)
        gt = jnp.dot(xn, wg, preferred_element_type=F32)
        r1 = pltpu.roll(gt, 1, 0)
        r2 = pltpu.roll(gt, 2, 0)
        g1 = jnp.concatenate([jnp.where(sub >= 1, r1[:8], prev1), r1[8:]], axis=0)
        g2 = jnp.concatenate([jnp.where(sub >= 2, r2[:8], prev2), r2[8:]], axis=0)
        prev1, prev2 = r1[:8], r2[:8]
        z = cw[2:3] * gt + cw[1:2] * g1 + cw[0:1] * g2 + cb
        gelu = 0.5 * z * (1.0 + lax.erf(z * math.sqrt(0.5)))
        o_ref[0, r * FF_ROWS:(r + 1) * FF_ROWS, :] = (gelu * u).astype(BF16)


def ffn_in(h3, g, w_in, conv_w, conv_b):
    b, s, d = h3.shape
    nc = D_FF // FF_CHUNK
    return pl.pallas_call(
        _ffn_in_kernel,
        grid=(b, nc),
        in_specs=[pl.BlockSpec((1, s, d), lambda i, c: (i, 0, 0)),
                  pl.BlockSpec((1, d), lambda i, c: (0, 0)),
                  pl.BlockSpec((d, FF_CHUNK), lambda i, c: (0, c)),
                  pl.BlockSpec((d, FF_CHUNK), lambda i, c: (0, c + nc)),
                  pl.BlockSpec((3, FF_CHUNK), lambda i, c: (0, c)),
                  pl.BlockSpec((1, FF_CHUNK), lambda i, c: (0, c))],
        out_specs=pl.BlockSpec((1, s, FF_CHUNK), lambda i, c: (i, 0, c)),
        out_shape=jax.ShapeDtypeStruct((b, s, D_FF), BF16),
        scratch_shapes=[pltpu.VMEM((s, d), BF16)],
        compiler_params=_params(2, VMEM_LIMIT),
        name="ffn_in",
    )(h3, g.reshape(1, d), w_in, w_in, conv_w, conv_b.reshape(1, D_FF))


def _bucket_of_distance(s):
    max_exact = REL_BUCKETS // 2
    n_large = REL_BUCKETS - max_exact
    thresholds = []
    for k in range(1, n_large):
        t = max_exact * (REL_MAX_DIST / max_exact) ** (k / n_large)
        ti = int(round(t))
        thresholds.append(ti if abs(t - ti) < 1e-9 else int(math.ceil(t)))
    d = np.arange(s)
    large = max_exact + sum((d >= t).astype(np.int64) for t in thresholds)
    return np.where(d < max_exact, d, np.minimum(large, REL_BUCKETS - 1)).astype(np.int32)


def _dilated_log_multiplicity(s):
    d = np.arange(s)
    count = np.zeros(s, np.int64)
    for window, dil in DIL_CONFIGS:
        count += ((d % dil == 0) & (d // dil <= window // dil)).astype(np.int64)
    return np.where(count > 0, np.log(np.maximum(count, 1)), NEG).astype(np.float32)


def _bias_strip_kernel(w_ref, o_ref, *, s):
    tile = jnp.broadcast_to(w_ref[0], (TQ, s + TQ))
    tile = pltpu.roll(tile, 0, 1, stride=1, stride_axis=0)
    o_ref[0] = tile[:, TQ:]


def bias_strips(rel_bias, s):
    bucket = _bucket_of_distance(s)
    per_dist = jnp.take(rel_bias.T.astype(F32), jnp.asarray(bucket), axis=1)
    logmult = jnp.asarray(_dilated_log_multiplicity(s))
    dil0 = MOBA_HEADS + DIFF_HEADS
    covered = logmult > 0.5 * NEG
    dil_rows = jnp.where(covered[None, :], per_dist[dil0:] + logmult[None, :], NEG)
    per_dist = jnp.concatenate([per_dist[:dil0], dil_rows], axis=0) * LOG2E
    nh = per_dist.shape[0]
    w = jnp.concatenate([jnp.full((nh, 1), NEG, F32), per_dist[:, ::-1],
                         jnp.full((nh, TQ - 1), NEG, F32)], axis=1).reshape(nh, 1, s + TQ)
    return pl.pallas_call(
        functools.partial(_bias_strip_kernel, s=s),
        grid=(nh,),
        in_specs=[pl.BlockSpec((1, 1, s + TQ), lambda h: (h, 0, 0))],
        out_specs=pl.BlockSpec((1, TQ, s), lambda h: (h, 0, 0)),
        out_shape=jax.ShapeDtypeStruct((nh, TQ, s), F32),
        compiler_params=_params(1),
        name="bias_strips",
    )(w)


def _store_transposed(kt_ref, idx, k):
    s = k.shape[0]
    for j in range(s // TK):
        kt_ref[idx, :, j * TK:(j + 1) * TK] = k[j * TK:(j + 1) * TK, :].T


def _attn_kernel(*refs, mode, moba, lam_init, s):
    if mode == "diff":
        q_ref, k_ref, v_ref, b_ref, lam_ref, sg_ref, o_ref, kt_ref, sc_ref, p_ref = refs
    elif moba:
        q_ref, k_ref, v_ref, b_ref, o_ref, kt_ref, sc_ref, p_ref, vone_ref, drop_ref = refs
    else:
        q_ref, k_ref, v_ref, b_ref, o_ref, kt_ref, sc_ref, p_ref, vone_ref = refs
    nblk = s // TK
    lane = lax.broadcasted_iota(jnp.int32, (TQ, LANES), 1)
    halves = (lane < HEAD_DIM, lane >= HEAD_DIM)
    nt = (((1,), (1,)), ((), ()))

    if moba:
        k_all = k_ref[0]
        q_all = q_ref[0]
        rowblk = lax.broadcasted_iota(jnp.int32, (s, LANES), 0) // TK
        lane_s = lax.broadcasted_iota(jnp.int32, (s, LANES), 1)
        kmean = jnp.mean(k_all.astype(F32).reshape(nblk, TK, LANES), axis=1)
        k_hi = kmean.astype(BF16)
        rem = kmean - k_hi.astype(F32)
        k_mid = rem.astype(BF16)
        k_lo = (rem - k_mid.astype(F32)).astype(BF16)
        kmean3 = jnp.concatenate([k_hi, k_mid, k_lo], axis=1)
        qblk = lax.broadcasted_iota(jnp.int32, (nblk, s), 1) // TQ
        blk = lax.broadcasted_iota(jnp.int32, (nblk, s), 0)
        for c in range(2):
            base = HEAD_DIM if c == 0 else 0
            in_half = (lane_s < HEAD_DIM) if c == 0 else (lane_s >= HEAD_DIM)
            onehot = jnp.where(lane_s - base == rowblk, 1.0, 0.0).astype(BF16)
            _store_transposed(kt_ref, c, jnp.where(in_half, k_all, onehot))
            qh = jnp.where(in_half, q_all, jnp.zeros_like(q_all))
            gate = lax.dot_general(kmean3, jnp.concatenate([qh, qh, qh], axis=1), nt,
                                   preferred_element_type=F32)
            rank = jnp.zeros((nblk, s), jnp.int32)
            for n in range(nblk - 1):
                gn = gate[n:n + 1, :]
                beats = jnp.where(gn > gate, 1, jnp.where(gn == gate, jnp.where(n < blk, 1, 0), 0))
                rank = rank + jnp.where(n < qblk, beats, 0)
            drop = jnp.where(blk < qblk, jnp.where(rank >= MOBA_TOPK, NEG, 0.0), 0.0)
            drop_ref[c] = jnp.zeros((LANES, s), F32)
            drop_ref[c, base:base + nblk, :] = drop

    else:
        _store_transposed(kt_ref, 0, k_ref[0])

    if mode == "diff":
        lp = lam_ref[...]
        lam = (jnp.exp(jnp.sum(lp[0:1] * lp[1:2], axis=1, keepdims=True))
               - jnp.exp(jnp.sum(lp[2:3] * lp[3:4], axis=1, keepdims=True)) + lam_init)

    units = [(i, c) for i in range(nblk) for c in range(2)]

    if mode == "pair":
        v_all = v_ref[0]
        lane_s = lax.broadcasted_iota(jnp.int32, (s, LANES), 1)
        ones = jnp.ones_like(v_all)
        vone_ref[0] = jnp.where(lane_s < HEAD_DIM, v_all, ones)
        vone_ref[1] = jnp.where(lane_s >= HEAD_DIM, v_all, ones)

    def scores(u):
        i, c = units[u]
        nk = (i + 1) * TK
        q = q_ref[0, i * TQ:(i + 1) * TQ, :]
        qc = jnp.where(halves[c], q, jnp.zeros_like(q))
        if moba and i > 0:
            sel = drop_ref[c, :, i * TQ:(i + 1) * TQ].T
            qc = jnp.where(halves[c], q, sel.astype(BF16))
        sc_ref[u % 2, :, 0:nk] = (jnp.dot(qc, kt_ref[c if moba else 0, :, 0:nk],
                                          preferred_element_type=F32)
                                  + b_ref[c if mode == "pair" else 0, :, s - nk:])

    def softmax(u):
        nk = (units[u][0] + 1) * TK
        m = jnp.max(sc_ref[u % 2, :, 0:nk], axis=1, keepdims=True)
        p = jnp.exp2(sc_ref[u % 2, :, 0:nk] - m)
        p_ref[u % 2, :, 0:nk] = p.astype(BF16)
        return None if mode == "pair" else jnp.sum(p, axis=1, keepdims=True)

    outs = []

    def weighted_values(u, l):
        i, c = units[u]
        nk = (i + 1) * TK
        if mode == "pair":
            o = jnp.dot(p_ref[u % 2, :, 0:nk], vone_ref[c, 0:nk, :], preferred_element_type=F32)
            l = pltpu.roll(o, HEAD_DIM, 1)
        else:
            o = jnp.dot(p_ref[u % 2, :, 0:nk], v_ref[0, 0:nk, :], preferred_element_type=F32)
        outs.append(o / l)
        if c == 1:
            if mode == "pair":
                res = jnp.where(halves[0], outs[0], outs[1])
            else:
                res = _rms(outs[0] - lam * outs[1], sg_ref[...]) * (1.0 - lam_init)
            o_ref[0, i * TQ:(i + 1) * TQ, :] = res.astype(o_ref.dtype)
            outs.clear()

    scores(0)
    l_prev = None
    for u in range(len(units)):
        if u + 1 < len(units):
            scores(u + 1)
        l = softmax(u)
        if u > 0:
            weighted_values(u - 1, l_prev)
        l_prev = l
    weighted_values(len(units) - 1, l_prev)


def attention(qkv, col_q, col_k, col_v, n_blocks, strips, head0, mode, moba=False,
              lam_params=None, subln_g=None, lam_init=0.0):
    b, s, _ = qkv.shape
    nb = 2 if mode == "pair" else 1
    seq = lambda col: pl.BlockSpec((1, s, LANES), lambda p, bb: (bb, 0, col + p))
    in_specs = [seq(col_q), seq(col_k), seq(col_v),
                pl.BlockSpec((nb, TQ, s), lambda p, bb: (head0 // nb + p, 0, 0))]
    args = [qkv, qkv, qkv, strips]
    if mode == "diff":
        in_specs += [pl.BlockSpec((4, HEAD_DIM), lambda p, bb: (0, 0)),
                     pl.BlockSpec((1, LANES), lambda p, bb: (0, 0))]
        args += [lam_params.astype(F32), subln_g.reshape(1, LANES)]
    scratch = [pltpu.VMEM((2 if moba else 1, LANES, s), BF16),
               pltpu.VMEM((2, TQ, s), F32), pltpu.VMEM((2, TQ, s), BF16)]
    if mode == "pair":
        scratch += [pltpu.VMEM((2, s, LANES), BF16)]
    if moba:
        scratch += [pltpu.VMEM((2, LANES, s), F32)]
    return pl.pallas_call(
        functools.partial(_attn_kernel, mode=mode, moba=moba, lam_init=lam_init, s=s),
        grid=(n_blocks, b),
        in_specs=in_specs,
        out_specs=pl.BlockSpec((1, s, LANES), lambda p, bb: (bb, 0, p)),
        out_shape=jax.ShapeDtypeStruct((b, s, n_blocks * LANES), BF16),
        scratch_shapes=scratch,
        compiler_params=_params(2, VMEM_LIMIT),
        name="attn_" + mode + ("_moba" if moba else ""),
    )(*args)


def _mla_up_kernel(lat_ref, qg_ref, kvg_ref, wa_ref, wb_ref, wk_ref, wv_ref, cq_ref, sq_ref,
                   ck_ref, sk_ref, q_ref, k_ref, v_ref, *, per_seq):
    lat = lat_ref[...]
    tm = lat.shape[0]
    r0 = pl.multiple_of((pl.program_id(0) % per_seq) * tm, tm)
    cqn = _rms(lat[:, :MLA_Q_RANK], qg_ref[...]).astype(BF16)
    ckvn = _rms(lat[:, MLA_Q_RANK:MLA_Q_RANK + MLA_KV_RANK], kvg_ref[...]).astype(BF16)
    qa = jnp.dot(cqn, wa_ref[...], preferred_element_type=F32)
    qb = jnp.dot(cqn, wb_ref[...], preferred_element_type=F32)
    kn = jnp.dot(ckvn, wk_ref[...], preferred_element_type=F32).astype(BF16)
    v_ref[...] = jnp.dot(ckvn, wv_ref[...], preferred_element_type=F32).astype(BF16)
    x = lat[:, MLA_Q_RANK + MLA_KV_RANK:]
    kr = (x * ck_ref[pl.ds(r0, tm), :]
          + pltpu.roll(x, MLA_ROPE_DIM, 1) * sk_ref[pl.ds(r0, tm), :]).astype(BF16)
    cq = cq_ref[pl.ds(r0, tm), :]
    sq = sq_ref[pl.ds(r0, tm), :]
    hw = 2 * LANES
    for h in range(MLA_HEADS):
        q_ref[:, h * hw:(h + 1) * hw] = (qa[:, h * hw:(h + 1) * hw] * cq
                                         + qb[:, h * hw:(h + 1) * hw] * sq).astype(BF16)
        k_ref[:, h * hw:h * hw + LANES] = kn[:, h * LANES:(h + 1) * LANES]
        k_ref[:, h * hw + LANES:(h + 1) * hw] = kr


def mla_up(latent, q_norm, kv_norm, wa, wb, wk, wv, rope_tabs, s, tm=1024):
    m = latent.shape[0]
    cq, sq, ck, sk = rope_tabs
    per_seq = s // tm
    tab = lambda w: pl.BlockSpec((s, w), lambda i: (0, 0))
    full = lambda a: pl.BlockSpec(a.shape, lambda i: (0, 0))
    hw = 2 * LANES
    rows = lambda w: pl.BlockSpec((tm, w), lambda i: (i, 0))
    return pl.pallas_call(
        functools.partial(_mla_up_kernel, per_seq=per_seq),
        grid=(m // tm,),
        in_specs=[rows(latent.shape[1]),
                  pl.BlockSpec((1, MLA_Q_RANK), lambda i: (0, 0)),
                  pl.BlockSpec((1, MLA_KV_RANK), lambda i: (0, 0)),
                  full(wa), full(wb), full(wk), full(wv), tab(hw), tab(hw), tab(LANES), tab(LANES)],
        out_specs=[rows(MLA_HEADS * hw), rows(MLA_HEADS * hw), rows(MLA_HEADS * LANES)],
        out_shape=[jax.ShapeDtypeStruct((m, MLA_HEADS * hw), BF16),
                   jax.ShapeDtypeStruct((m, MLA_HEADS * hw), BF16),
                   jax.ShapeDtypeStruct((m, MLA_HEADS * LANES), BF16)],
        compiler_params=_params(1, VMEM_LIMIT),
        name="mla_up",
    )(latent, q_norm.reshape(1, -1), kv_norm.reshape(1, -1), wa, wb, wk, wv, cq, sq, ck, sk)


def _mla_attn_kernel(q_ref, k_ref, v_ref, o_ref, kt_ref, sc_ref, p_ref, *, s):
    scale = (MLA_NOPE_DIM + MLA_ROPE_DIM) ** -0.5 * LOG2E
    _store_transposed(kt_ref, 0, k_ref[0])
    row = lax.broadcasted_iota(jnp.int32, (TQ, TK), 0)
    col = lax.broadcasted_iota(jnp.int32, (TQ, TK), 1)
    causal = row >= col
    nblk = s // TQ

    def scores(i):
        nk = (i + 1) * TK
        q = q_ref[0, i * TQ:(i + 1) * TQ, :]
        sc = jnp.dot(q, kt_ref[0, :, 0:nk], preferred_element_type=F32) * scale
        if i > 0:
            sc_ref[i % 2, :, 0:nk - TK] = sc[:, :nk - TK]
        sc_ref[i % 2, :, nk - TK:nk] = jnp.where(causal, sc[:, nk - TK:], NEG)

    def softmax(i):
        nk = (i + 1) * TK
        m = jnp.max(sc_ref[i % 2, :, 0:nk], axis=1, keepdims=True)
        p = jnp.exp2(sc_ref[i % 2, :, 0:nk] - m)
        p_ref[i % 2, :, 0:nk] = p.astype(BF16)
        return jnp.sum(p, axis=1, keepdims=True)

    def weighted_values(i, l):
        nk = (i + 1) * TK
        o = jnp.dot(p_ref[i % 2, :, 0:nk], v_ref[0, 0:nk, :], preferred_element_type=F32)
        o_ref[0, i * TQ:(i + 1) * TQ, :] = (o / l).astype(o_ref.dtype)

    scores(0)
    l_prev = None
    for i in range(nblk):
        if i + 1 < nblk:
            scores(i + 1)
        l = softmax(i)
        if i > 0:
            weighted_values(i - 1, l_prev)
        l_prev = l
    weighted_values(nblk - 1, l_prev)


def mla_attention(q, k, v):
    b, s, _ = q.shape
    hw = 2 * LANES
    return pl.pallas_call(
        functools.partial(_mla_attn_kernel, s=s),
        grid=(MLA_HEADS, b),
        in_specs=[pl.BlockSpec((1, s, hw), lambda h, bb: (bb, 0, h)),
                  pl.BlockSpec((1, s, hw), lambda h, bb: (bb, 0, h)),
                  pl.BlockSpec((1, s, LANES), lambda h, bb: (bb, 0, h))],
        out_specs=pl.BlockSpec((1, s, LANES), lambda h, bb: (bb, 0, h)),
        out_shape=jax.ShapeDtypeStruct((b, s, MLA_HEADS * LANES), BF16),
        scratch_shapes=[pltpu.VMEM((1, hw, s), BF16),
                        pltpu.VMEM((2, TQ, s), F32), pltpu.VMEM((2, TQ, s), BF16)],
        compiler_params=_params(2, VMEM_LIMIT),
        name="mla_attn",
    )(q, k, v)


def _rope_tables(s):
    half = MLA_ROPE_DIM // 2
    freq = ROPE_THETA ** (-jnp.arange(half, dtype=F32) / half)
    ang = jnp.arange(s, dtype=F32)[:, None] * freq[None, :]
    cos, sin = jnp.cos(ang), jnp.sin(ang)
    cos2 = jnp.concatenate([cos, cos], axis=1)
    sin2 = jnp.concatenate([-sin, sin], axis=1)
    z64 = jnp.zeros((s, MLA_ROPE_DIM), F32)
    cq = jnp.concatenate([jnp.ones((s, MLA_NOPE_DIM), F32), cos2, z64], axis=1)
    sq = jnp.concatenate([jnp.zeros((s, MLA_NOPE_DIM), F32), sin2, z64], axis=1)
    ck = jnp.concatenate([cos2, z64], axis=1)
    sk = jnp.concatenate([sin2, z64], axis=1)
    return cq, sq, ck, sk


def _swap_halves(w):
    half = w.shape[-1] // 2
    return jnp.concatenate([w[..., half:], w[..., :half]], axis=-1)


def _mla_q_weights(w_uq):
    r = w_uq.shape[0]
    w = w_uq.reshape(r, MLA_HEADS, MLA_NOPE_DIM + MLA_ROPE_DIM)
    nope, rope = w[..., :MLA_NOPE_DIM], w[..., MLA_NOPE_DIM:]
    z64 = jnp.zeros((r, MLA_HEADS, MLA_ROPE_DIM), w.dtype)
    z128 = jnp.zeros((r, MLA_HEADS, MLA_NOPE_DIM), w.dtype)
    wa = jnp.concatenate([nope, rope, z64], axis=-1).reshape(r, -1)
    wb = jnp.concatenate([z128, _swap_halves(rope), z64], axis=-1).reshape(r, -1)
    return wa.astype(BF16), wb.astype(BF16)


def kernel(x, rel_bias, even_norm1, even_w_in, diff_lambda, diff_subln, even_w_out,
           odd_norm1, odd_w_in, mla_q_norm, mla_w_uq, mla_kv_norm, mla_w_ukv, odd_w_out,
           ffn_norm, ffn_w_in, ffn_conv_w, ffn_conv_b, ffn_w_out, final_norm):
    b, s, d = x.shape
    m = b * s
    strips = bias_strips(rel_bias, s)
    rope_tabs = _rope_tables(s)
    qscale = HEAD_DIM ** -0.5 * LOG2E
    h = x.reshape(m, d)
    for layer in range(DEPTH):
        li = layer // 2
        if layer % 2 == 0:
            lam_init = 0.8 - 0.6 * math.exp(-0.3 * layer)
            mw, dw = MOBA_HEADS * HEAD_DIM, DIFF_HEADS * 2 * HEAD_DIM
            colscale = np.ones((3 * mw + 3 * dw,), np.float32)
            colscale[:mw] = qscale
            colscale[3 * mw:3 * mw + dw] = qscale
            w = (even_w_in[li] * jnp.asarray(colscale)).astype(BF16)
            (qkv,) = in_proj(h, even_norm1[li], w, [w.shape[1]], [BF16])
            qkv = qkv.reshape(b, s, -1)
            nb = mw // LANES
            o_a = attention(qkv, 0, nb, 2 * nb, nb, strips, 0, "pair", moba=True)
            o_b = attention(qkv, 3 * nb, 4 * nb, 5 * nb, DIFF_HEADS, strips, MOBA_HEADS, "diff",
                            lam_params=diff_lambda[li], subln_g=diff_subln[li], lam_init=lam_init)
            wo = even_w_out[li]
        else:
            w = odd_w_in[li]
            dw = DIL_HEADS * HEAD_DIM
            lat0 = 3 * dw
            colscale = np.ones((w.shape[1],), np.float32)
            colscale[:dw] = qscale
            kr_cols = w[:, lat0 + MLA_Q_RANK + MLA_KV_RANK:]
            w_all = jnp.concatenate([w * jnp.asarray(colscale), _swap_halves(kr_cols)],
                                    axis=1).astype(BF16)
            qkv, latent = in_proj(h, odd_norm1[li], w_all,
                                  [lat0, w_all.shape[1] - lat0], [BF16, F32])
            qkv = qkv.reshape(b, s, -1)
            nb = dw // LANES
            o_a = attention(qkv, 0, nb, 2 * nb, nb, strips, MOBA_HEADS + DIFF_HEADS, "pair")
            wa, wb = _mla_q_weights(mla_w_uq[li])
            wkv = mla_w_ukv[li].reshape(MLA_KV_RANK, MLA_HEADS, MLA_NOPE_DIM + MLA_V_DIM)
            wk = wkv[..., :MLA_NOPE_DIM].reshape(MLA_KV_RANK, -1).astype(BF16)
            wv = wkv[..., MLA_NOPE_DIM:].reshape(MLA_KV_RANK, -1).astype(BF16)
            q_m, k_m, v_m = mla_up(latent, mla_q_norm[li], mla_kv_norm[li], wa, wb, wk, wv,
                                   rope_tabs, s)
            o_b = mla_attention(q_m.reshape(b, s, -1), k_m.reshape(b, s, -1),
                                v_m.reshape(b, s, -1))
            wo = odd_w_out[li]
        h = proj_residual([o_a.reshape(m, -1), o_b.reshape(m, -1)], wo, h)
        act = ffn_in(h.reshape(b, s, d), ffn_norm[layer], ffn_w_in[layer],
                     ffn_conv_w[layer], ffn_conv_b[layer])
        h = proj_residual([act.reshape(m, D_FF)], ffn_w_out[layer], h,
                          final_g=final_norm if layer == DEPTH - 1 else None)
    return h.reshape(b, s, d)
```

```python
import functools
import math

import numpy as np
import jax
import jax.numpy as jnp
from jax import lax
from jax.experimental import pallas as pl
from jax.experimental.pallas import tpu as pltpu

D_MODEL = 1024
DEPTH = 4
HEAD_DIM = 64
MOBA_HEADS = 8
MOBA_BLOCK = 256
MOBA_TOPK = 3
DIFF_HEADS = 4
DIL_HEADS = 8
DIL_CONFIGS = ((128, 1), (512, 4), (2048, 16))
MLA_HEADS = 4
MLA_Q_RANK = 256
MLA_KV_RANK = 128
MLA_NOPE_DIM = 128
MLA_ROPE_DIM = 64
MLA_V_DIM = 128
ROPE_THETA = 10000.0
REL_BUCKETS = 32
REL_MAX_DIST = 1024
N_BIAS_HEADS = MOBA_HEADS + DIFF_HEADS + DIL_HEADS
D_FF = 2816
EPS = 1e-6
NEG = -1e30
LOG2E = math.log2(math.e)

LANES = 128
TQ = 256
TK = 256
FF_CHUNK = 256
FF_ROWS = 512
VMEM_LIMIT = 56 * 1024 * 1024

F32 = jnp.float32
BF16 = jnp.bfloat16


def _params(n_axes, vmem=None):
    return pltpu.CompilerParams(dimension_semantics=("arbitrary",) * n_axes,
                                vmem_limit_bytes=vmem)


def _rms(x, g):
    ms = jnp.mean(x * x, axis=-1, keepdims=True)
    return x * lax.rsqrt(ms + EPS) * g


def _in_proj_kernel(x_ref, g_ref, w_ref, *o_refs, row_chunk):
    tm = x_ref.shape[0]
    g = g_ref[...]
    for r in range(tm // row_chunk):
        rows = slice(r * row_chunk, (r + 1) * row_chunk)
        xn = _rms(x_ref[rows, :], g).astype(BF16)
        acc = jnp.dot(xn, w_ref[...], preferred_element_type=F32)
        col = 0
        for o_ref in o_refs:
            width = o_ref.shape[1]
            o_ref[rows, :] = acc[:, col:col + width].astype(o_ref.dtype)
            col += width


def in_proj(x, g, w, out_widths, out_dtypes, tm=1024, row_chunk=512):
    m, d = x.shape
    return pl.pallas_call(
        functools.partial(_in_proj_kernel, row_chunk=row_chunk),
        grid=(m // tm,),
        in_specs=[pl.BlockSpec((tm, d), lambda i: (i, 0)),
                  pl.BlockSpec((1, d), lambda i: (0, 0)),
                  pl.BlockSpec(w.shape, lambda i: (0, 0))],
        out_specs=[pl.BlockSpec((tm, n), lambda i: (i, 0)) for n in out_widths],
        out_shape=[jax.ShapeDtypeStruct((m, n), dt) for n, dt in zip(out_widths, out_dtypes)],
        compiler_params=_params(1, VMEM_LIMIT),
        name="in_proj",
    )(x, g.reshape(1, d), w)


def _proj_res_kernel(*refs, n_in, final_norm):
    a_refs = refs[:n_in]
    w_ref, res_ref = refs[n_in], refs[n_in + 1]
    o_ref, wb_ref = refs[-2], refs[-1]

    @pl.when(pl.program_id(0) == 0)
    def _():
        wb_ref[...] = w_ref[...].astype(BF16)

    acc = res_ref[...]
    row = 0
    for a_ref in a_refs:
        k = a_ref.shape[1]
        acc = acc + jnp.dot(a_ref[...], wb_ref[row:row + k, :], preferred_element_type=F32)
        row += k
    if final_norm:
        acc = _rms(acc, refs[n_in + 2][...])
    o_ref[...] = acc


def proj_residual(acts, w_stack, layer, res, final_g=None, tm=512):
    m, d = res.shape
    n_in = len(acts)
    in_specs = [pl.BlockSpec((tm, a.shape[1]), lambda i: (i, 0)) for a in acts]
    in_specs += [pl.BlockSpec((None,) + w_stack.shape[1:], lambda i: (layer, 0, 0)),
                 pl.BlockSpec((tm, d), lambda i: (i, 0))]
    args = list(acts) + [w_stack, res]
    if final_g is not None:
        in_specs += [pl.BlockSpec((1, d), lambda i: (0, 0))]
        args += [final_g.reshape(1, d)]
    return pl.pallas_call(
        functools.partial(_proj_res_kernel, n_in=n_in, final_norm=final_g is not None),
        grid=(m // tm,),
        in_specs=in_specs,
        out_specs=pl.BlockSpec((tm, d), lambda i: (i, 0)),
        out_shape=jax.ShapeDtypeStruct((m, d), F32),
        scratch_shapes=[pltpu.VMEM(w_stack.shape[1:], BF16)],
        compiler_params=_params(1, VMEM_LIMIT),
        name="proj_residual",
    )(*args)


def _ffn_in_kernel(x_ref, g_ref, wu_ref, wg_ref, cw_ref, cb_ref, o_ref, xn_ref):
    @pl.when(pl.program_id(1) == 0)
    def _():
        xn_ref[...] = _rms(x_ref[0], g_ref[...]).astype(BF16)

    s = xn_ref.shape[0]
    cw = cw_ref[...]
    cb = cb_ref[...]
    sub = lax.broadcasted_iota(jnp.int32, (8, FF_CHUNK), 0)
    prev1 = prev2 = jnp.zeros((8, FF_CHUNK), F32)
    wu = wu_ref[...].astype(BF16)
    wg = wg_ref[...].astype(BF16)
    for r in range(s // FF_ROWS):
        xn = xn_ref[r * FF_ROWS:(r + 1) * FF_ROWS, :]
        u = jnp.dot(xn, wu, preferred_element_type=F32)
        gt = jnp.dot(xn, wg, preferred_element_type=F32)
        r1 = pltpu.roll(gt, 1, 0)
        r2 = pltpu.roll(gt, 2, 0)
        g1 = jnp.concatenate([jnp.where(sub >= 1, r1[:8], prev1), r1[8:]], axis=0)
        g2 = jnp.concatenate([jnp.where(sub >= 2, r2[:8], prev2), r2[8:]], axis=0)
        prev1, prev2 = r1[:8], r2[:8]
        z = cw[2:3] * gt + cw[1:2] * g1 + cw[0:1] * g2 + cb
        gelu = 0.5 * z * (1.0 + lax.erf(z * math.sqrt(0.5)))
        o_ref[0, r * FF_ROWS:(r + 1) * FF_ROWS, :] = (gelu * u).astype(BF16)


def ffn_in(h3, g, w_stack, layer, conv_w, conv_b):
    b, s, d = h3.shape
    nc = D_FF // FF_CHUNK
    return pl.pallas_call(
        _ffn_in_kernel,
        grid=(b, nc),
        in_specs=[pl.BlockSpec((1, s, d), lambda i, c: (i, 0, 0)),
                  pl.BlockSpec((1, d), lambda i, c: (0, 0)),
                  pl.BlockSpec((None, d, FF_CHUNK), lambda i, c: (layer, 0, c)),
                  pl.BlockSpec((None, d, FF_CHUNK), lambda i, c: (layer, 0, c + nc)),
                  pl.BlockSpec((3, FF_CHUNK), lambda i, c: (0, c)),
                  pl.BlockSpec((1, FF_CHUNK), lambda i, c: (0, c))],
        out_specs=pl.BlockSpec((1, s, FF_CHUNK), lambda i, c: (i, 0, c)),
        out_shape=jax.ShapeDtypeStruct((b, s, D_FF), BF16),
        scratch_shapes=[pltpu.VMEM((s, d), BF16)],
        compiler_params=_params(2, VMEM_LIMIT),
        name="ffn_in",
    )(h3, g.reshape(1, d), w_stack, w_stack, conv_w, conv_b.reshape(1, D_FF))


def _bucket_of_distance(s):
    max_exact = REL_BUCKETS // 2
    n_large = REL_BUCKETS - max_exact
    thresholds = []
    for k in range(1, n_large):
        t = max_exact * (REL_MAX_DIST / max_exact) ** (k / n_large)
        ti = int(round(t))
        thresholds.append(ti if abs(t - ti) < 1e-9 else int(math.ceil(t)))
    d = np.arange(s)
    large = max_exact + sum((d >= t).astype(np.int64) for t in thresholds)
    return np.where(d < max_exact, d, np.minimum(large, REL_BUCKETS - 1)).astype(np.int32)


def _dilated_log_multiplicity(s):
    d = np.arange(s)
    count = np.zeros(s, np.int64)
    for window, dil in DIL_CONFIGS:
        count += ((d % dil == 0) & (d // dil <= window // dil)).astype(np.int64)
    return np.where(count > 0, np.log(np.maximum(count, 1)), NEG).astype(np.float32)


def _fill_bias_strip(strip_ref, vec_ref, n, s):
    for h in range(n):
        tile = jnp.broadcast_to(vec_ref[h], (TQ, s + TQ))
        tile = pltpu.roll(tile, 0, 1, stride=1, stride_axis=0)
        strip_ref[h] = tile[:, TQ:]


def bias_vectors(rel_bias, s):
    bucket = _bucket_of_distance(s)
    per_dist = jnp.take(rel_bias.T.astype(F32), jnp.asarray(bucket), axis=1)
    logmult = jnp.asarray(_dilated_log_multiplicity(s))
    dil0 = MOBA_HEADS + DIFF_HEADS
    covered = logmult > 0.5 * NEG
    dil_rows = jnp.where(covered[None, :], per_dist[dil0:] + logmult[None, :], NEG)
    per_dist = jnp.concatenate([per_dist[:dil0], dil_rows], axis=0) * LOG2E
    nh = per_dist.shape[0]
    w = jnp.concatenate([jnp.full((nh, 1), NEG, F32), per_dist[:, ::-1],
                         jnp.full((nh, TQ - 1), NEG, F32)], axis=1)
    return w.reshape(nh, 1, s + TQ)


def _store_transposed(kt_ref, idx, k):
    s = k.shape[0]
    for j in range(s // TK):
        kt_ref[idx, :, j * TK:(j + 1) * TK] = k[j * TK:(j + 1) * TK, :].T


def _attn_kernel(*refs, mode, moba, lam_init, s):
    if mode == "diff":
        (q_ref, k_ref, v_ref, bvec_ref, lam_ref, sg_ref, o_ref,
         kt_ref, sc_ref, p_ref, vone_ref, b_ref) = refs
    elif moba:
        (q_ref, k_ref, v_ref, bvec_ref, o_ref,
         kt_ref, sc_ref, p_ref, vone_ref, b_ref, drop_ref) = refs
    else:
        q_ref, k_ref, v_ref, bvec_ref, o_ref, kt_ref, sc_ref, p_ref, vone_ref, b_ref = refs

    @pl.when(pl.program_id(1) == 0)
    def _():
        _fill_bias_strip(b_ref, bvec_ref, 2 if mode == "pair" else 1, s)

    vone_ref[:, :LANES] = v_ref[0]
    vone_ref[:, LANES:] = jnp.ones((s, LANES), BF16)
    nblk = s // TK
    lane = lax.broadcasted_iota(jnp.int32, (TQ, LANES), 1)
    halves = (lane < HEAD_DIM, lane >= HEAD_DIM)
    nt = (((1,), (1,)), ((), ()))

    if moba:
        k_all = k_ref[0]
        q_all = q_ref[0]
        rowblk = lax.broadcasted_iota(jnp.int32, (s, LANES), 0) // TK
        lane_s = lax.broadcasted_iota(jnp.int32, (s, LANES), 1)
        kmean = jnp.mean(k_all.astype(F32).reshape(nblk, TK, LANES), axis=1)
        k_hi = kmean.astype(BF16)
        rem = kmean - k_hi.astype(F32)
        k_mid = rem.astype(BF16)
        k_lo = (rem - k_mid.astype(F32)).astype(BF16)
        kmean3 = jnp.concatenate([k_hi, k_mid, k_lo], axis=1)
        qblk = lax.broadcasted_iota(jnp.int32, (nblk, s), 1) // TQ
        blk = lax.broadcasted_iota(jnp.int32, (nblk, s), 0)
        for c in range(2):
            base = HEAD_DIM if c == 0 else 0
            in_half = (lane_s < HEAD_DIM) if c == 0 else (lane_s >= HEAD_DIM)
            onehot = jnp.where(lane_s - base == rowblk, 1.0, 0.0).astype(BF16)
            _store_transposed(kt_ref, c, jnp.where(in_half, k_all, onehot))
            qh = jnp.where(in_half, q_all, jnp.zeros_like(q_all))
            gate = lax.dot_general(kmean3, jnp.concatenate([qh, qh, qh], axis=1), nt,
                                   preferred_element_type=F32)
            rank = jnp.zeros((nblk, s), jnp.int32)
            for n in range(nblk - 1):
                gn = gate[n:n + 1, :]
                beats = jnp.where(gn > gate, 1, jnp.where(gn == gate, jnp.where(n < blk, 1, 0), 0))
                rank = rank + jnp.where(n < qblk, beats, 0)
            drop = jnp.where(blk < qblk, jnp.where(rank >= MOBA_TOPK, NEG, 0.0), 0.0)
            drop_ref[c] = jnp.zeros((LANES, s), F32)
            drop_ref[c, base:base + nblk, :] = drop

    else:
        _store_transposed(kt_ref, 0, k_ref[0])

    if mode == "diff":
        lp = lam_ref[...]
        lam = (jnp.exp(jnp.sum(lp[0:1] * lp[1:2], axis=1, keepdims=True))
               - jnp.exp(jnp.sum(lp[2:3] * lp[3:4], axis=1, keepdims=True)) + lam_init)

    units = [(i, c) for i in range(nblk) for c in range(2)]

    def scores(u):
        i, c = units[u]
        nk = (i + 1) * TK
        q = q_ref[0, i * TQ:(i + 1) * TQ, :]
        qc = jnp.where(halves[c], q, jnp.zeros_like(q))
        if moba and i > 0:
            sel = drop_ref[c, :, i * TQ:(i + 1) * TQ].T
            qc = jnp.where(halves[c], q, sel.astype(BF16))
        sc_ref[u % 2, :, 0:nk] = (jnp.dot(qc, kt_ref[c if moba else 0, :, 0:nk],
                                          preferred_element_type=F32)
                                  + b_ref[c if mode == "pair" else 0, :, s - nk:])

    def softmax(u):
        nk = (units[u][0] + 1) * TK
        m = jnp.max(sc_ref[u % 2, :, 0:nk], axis=1, keepdims=True)
        p = jnp.exp2(sc_ref[u % 2, :, 0:nk] - m)
        p_ref[u % 2, :, 0:nk] = p.astype(BF16)

    outs = []

    def weighted_values(u):
        i, c = units[u]
        nk = (i + 1) * TK
        o = jnp.dot(p_ref[u % 2, :, 0:nk], vone_ref[0:nk, :], preferred_element_type=F32)
        outs.append(o[:, :LANES] / o[:, LANES:])
        if c == 1:
            if mode == "pair":
                res = jnp.where(halves[0], outs[0], outs[1])
            else:
                res = _rms(outs[0] - lam * outs[1], sg_ref[...]) * (1.0 - lam_init)
            o_ref[0, i * TQ:(i + 1) * TQ, :] = res.astype(o_ref.dtype)
            outs.clear()

    scores(0)
    for u in range(len(units)):
        if u + 1 < len(units):
            scores(u + 1)
        softmax(u)
        if u > 0:
            weighted_values(u - 1)
    weighted_values(len(units) - 1)


def attention(qkv, col_q, col_k, col_v, n_blocks, bias_vecs, head0, mode, moba=False,
              lam_params=None, subln_g=None, lam_init=0.0):
    b, s, _ = qkv.shape
    nb = 2 if mode == "pair" else 1
    seq = lambda col: pl.BlockSpec((1, s, LANES), lambda p, bb: (bb, 0, col + p))
    in_specs = [seq(col_q), seq(col_k), seq(col_v),
                pl.BlockSpec((nb, 1, s + TQ), lambda p, bb: (head0 // nb + p, 0, 0))]
    args = [qkv, qkv, qkv, bias_vecs]
    if mode == "diff":
        in_specs += [pl.BlockSpec((4, HEAD_DIM), lambda p, bb: (0, 0)),
                     pl.BlockSpec((1, LANES), lambda p, bb: (0, 0))]
        args += [lam_params.astype(F32), subln_g.reshape(1, LANES)]
    scratch = [pltpu.VMEM((2 if moba else 1, LANES, s), BF16),
               pltpu.VMEM((2, TQ, s), F32), pltpu.VMEM((2, TQ, s), BF16),
               pltpu.VMEM((s, 2 * LANES), BF16), pltpu.VMEM((nb, TQ, s), F32)]
    if moba:
        scratch += [pltpu.VMEM((2, LANES, s), F32)]
    return pl.pallas_call(
        functools.partial(_attn_kernel, mode=mode, moba=moba, lam_init=lam_init, s=s),
        grid=(n_blocks, b),
        in_specs=in_specs,
        out_specs=pl.BlockSpec((1, s, LANES), lambda p, bb: (bb, 0, p)),
        out_shape=jax.ShapeDtypeStruct((b, s, n_blocks * LANES), BF16),
        scratch_shapes=scratch,
        compiler_params=_params(2, VMEM_LIMIT),
        name="attn_" + mode + ("_moba" if moba else ""),
    )(*args)


def _mla_up_kernel(lat_ref, qg_ref, kvg_ref, wa_ref, wb_ref, wk_ref, wv_ref, cq_ref, sq_ref,
                   ck_ref, sk_ref, q_ref, k_ref, v_ref, *, per_seq):
    lat = lat_ref[...]
    tm = lat.shape[0]
    r0 = pl.multiple_of((pl.program_id(0) % per_seq) * tm, tm)
    cqn = _rms(lat[:, :MLA_Q_RANK], qg_ref[...]).astype(BF16)
    ckvn = _rms(lat[:, MLA_Q_RANK:MLA_Q_RANK + MLA_KV_RANK], kvg_ref[...]).astype(BF16)
    qa = jnp.dot(cqn, wa_ref[...], preferred_element_type=F32)
    qb = jnp.dot(cqn, wb_ref[...], preferred_element_type=F32)
    kn = jnp.dot(ckvn, wk_ref[...], preferred_element_type=F32).astype(BF16)
    v_ref[...] = jnp.dot(ckvn, wv_ref[...], preferred_element_type=F32).astype(BF16)
    x = lat[:, MLA_Q_RANK + MLA_KV_RANK:]
    kr = (x * ck_ref[pl.ds(r0, tm), :]
          + pltpu.roll(x, MLA_ROPE_DIM, 1) * sk_ref[pl.ds(r0, tm), :]).astype(BF16)
    cq = cq_ref[pl.ds(r0, tm), :]
    sq = sq_ref[pl.ds(r0, tm), :]
    hw = 2 * LANES
    for h in range(MLA_HEADS):
        q_ref[:, h * hw:(h + 1) * hw] = (qa[:, h * hw:(h + 1) * hw] * cq
                                         + qb[:, h * hw:(h + 1) * hw] * sq).astype(BF16)
        k_ref[:, h * hw:h * hw + LANES] = kn[:, h * LANES:(h + 1) * LANES]
        k_ref[:, h * hw + LANES:(h + 1) * hw] = kr


def mla_up(latent, q_norm, kv_norm, wa, wb, wk, wv, rope_tabs, s, tm=1024):
    m = latent.shape[0]
    cq, sq, ck, sk = rope_tabs
    per_seq = s // tm
    tab = lambda w: pl.BlockSpec((s, w), lambda i: (0, 0))
    full = lambda a: pl.BlockSpec(a.shape, lambda i: (0, 0))
    hw = 2 * LANES
    rows = lambda w: pl.BlockSpec((tm, w), lambda i: (i, 0))
    return pl.pallas_call(
        functools.partial(_mla_up_kernel, per_seq=per_seq),
        grid=(m // tm,),
        in_specs=[rows(latent.shape[1]),
                  pl.BlockSpec((1, MLA_Q_RANK), lambda i: (0, 0)),
                  pl.BlockSpec((1, MLA_KV_RANK), lambda i: (0, 0)),
                  full(wa), full(wb), full(wk), full(wv), tab(hw), tab(hw), tab(LANES), tab(LANES)],
        out_specs=[rows(MLA_HEADS * hw), rows(MLA_HEADS * hw), rows(MLA_HEADS * LANES)],
        out_shape=[jax.ShapeDtypeStruct((m, MLA_HEADS * hw), BF16),
                   jax.ShapeDtypeStruct((m, MLA_HEADS * hw), BF16),
                   jax.ShapeDtypeStruct((m, MLA_HEADS * LANES), BF16)],
        compiler_params=_params(1, VMEM_LIMIT),
        name="mla_up",
    )(latent, q_norm.reshape(1, -1), kv_norm.reshape(1, -1), wa, wb, wk, wv, cq, sq, ck, sk)


def _mla_attn_kernel(q_ref, k_ref, v_ref, o_ref, kt_ref, sc_ref, p_ref, vone_ref, *, s):
    scale = (MLA_NOPE_DIM + MLA_ROPE_DIM) ** -0.5 * LOG2E
    _store_transposed(kt_ref, 0, k_ref[0])
    vone_ref[:, :LANES] = v_ref[0]
    vone_ref[:, LANES:] = jnp.ones((s, LANES), BF16)
    row = lax.broadcasted_iota(jnp.int32, (TQ, TK), 0)
    col = lax.broadcasted_iota(jnp.int32, (TQ, TK), 1)
    causal = row >= col
    nblk = s // TQ

    def scores(i):
        nk = (i + 1) * TK
        q = q_ref[0, i * TQ:(i + 1) * TQ, :]
        sc = jnp.dot(q, kt_ref[0, :, 0:nk], preferred_element_type=F32) * scale
        if i > 0:
            sc_ref[i % 2, :, 0:nk - TK] = sc[:, :nk - TK]
        sc_ref[i % 2, :, nk - TK:nk] = jnp.where(causal, sc[:, nk - TK:], NEG)

    def softmax(i):
        nk = (i + 1) * TK
        m = jnp.max(sc_ref[i % 2, :, 0:nk], axis=1, keepdims=True)
        p = jnp.exp2(sc_ref[i % 2, :, 0:nk] - m)
        p_ref[i % 2, :, 0:nk] = p.astype(BF16)

    def weighted_values(i):
        nk = (i + 1) * TK
        o = jnp.dot(p_ref[i % 2, :, 0:nk], vone_ref[0:nk, :], preferred_element_type=F32)
        o_ref[0, i * TQ:(i + 1) * TQ, :] = (o[:, :LANES] / o[:, LANES:]).astype(o_ref.dtype)

    scores(0)
    for i in range(nblk):
        if i + 1 < nblk:
            scores(i + 1)
        softmax(i)
        if i > 0:
            weighted_values(i - 1)
    weighted_values(nblk - 1)


def mla_attention(q, k, v):
    b, s, _ = q.shape
    hw = 2 * LANES
    return pl.pallas_call(
        functools.partial(_mla_attn_kernel, s=s),
        grid=(MLA_HEADS, b),
        in_specs=[pl.BlockSpec((1, s, hw), lambda h, bb: (bb, 0, h)),
                  pl.BlockSpec((1, s, hw), lambda h, bb: (bb, 0, h)),
                  pl.BlockSpec((1, s, LANES), lambda h, bb: (bb, 0, h))],
        out_specs=pl.BlockSpec((1, s, LANES), lambda h, bb: (bb, 0, h)),
        out_shape=jax.ShapeDtypeStruct((b, s, MLA_HEADS * LANES), BF16),
        scratch_shapes=[pltpu.VMEM((1, hw, s), BF16),
                        pltpu.VMEM((2, TQ, s), F32), pltpu.VMEM((2, TQ, s), BF16),
                        pltpu.VMEM((s, 2 * LANES), BF16)],
        compiler_params=_params(2, VMEM_LIMIT),
        name="mla_attn",
    )(q, k, v)


def _rope_tables(s):
    half = MLA_ROPE_DIM // 2
    freq = ROPE_THETA ** (-jnp.arange(half, dtype=F32) / half)
    ang = jnp.arange(s, dtype=F32)[:, None] * freq[None, :]
    cos, sin = jnp.cos(ang), jnp.sin(ang)
    cos2 = jnp.concatenate([cos, cos], axis=1)
    sin2 = jnp.concatenate([-sin, sin], axis=1)
    z64 = jnp.zeros((s, MLA_ROPE_DIM), F32)
    cq = jnp.concatenate([jnp.ones((s, MLA_NOPE_DIM), F32), cos2, z64], axis=1)
    sq = jnp.concatenate([jnp.zeros((s, MLA_NOPE_DIM), F32), sin2, z64], axis=1)
    ck = jnp.concatenate([cos2, z64], axis=1)
    sk = jnp.concatenate([sin2, z64], axis=1)
    return cq, sq, ck, sk


def _swap_halves(w):
    half = w.shape[-1] // 2
    return jnp.concatenate([w[..., half:], w[..., :half]], axis=-1)


def _mla_q_weights(w_uq):
    r = w_uq.shape[0]
    w = w_uq.reshape(r, MLA_HEADS, MLA_NOPE_DIM + MLA_ROPE_DIM)
    nope, rope = w[..., :MLA_NOPE_DIM], w[..., MLA_NOPE_DIM:]
    z64 = jnp.zeros((r, MLA_HEADS, MLA_ROPE_DIM), w.dtype)
    z128 = jnp.zeros((r, MLA_HEADS, MLA_NOPE_DIM), w.dtype)
    wa = jnp.concatenate([nope, rope, z64], axis=-1).reshape(r, -1)
    wb = jnp.concatenate([z128, _swap_halves(rope), z64], axis=-1).reshape(r, -1)
    return wa.astype(BF16), wb.astype(BF16)


def kernel(x, rel_bias, even_norm1, even_w_in, diff_lambda, diff_subln, even_w_out,
           odd_norm1, odd_w_in, mla_q_norm, mla_w_uq, mla_kv_norm, mla_w_ukv, odd_w_out,
           ffn_norm, ffn_w_in, ffn_conv_w, ffn_conv_b, ffn_w_out, final_norm):
    b, s, d = x.shape
    m = b * s
    bias_vecs = bias_vectors(rel_bias, s)
    rope_tabs = _rope_tables(s)
    qscale = HEAD_DIM ** -0.5 * LOG2E
    h = x.reshape(m, d)
    for layer in range(DEPTH):
        li = layer // 2
        if layer % 2 == 0:
            lam_init = 0.8 - 0.6 * math.exp(-0.3 * layer)
            mw, dw = MOBA_HEADS * HEAD_DIM, DIFF_HEADS * 2 * HEAD_DIM
            colscale = np.ones((3 * mw + 3 * dw,), np.float32)
            colscale[:mw] = qscale
            colscale[3 * mw:3 * mw + dw] = qscale
            w = (even_w_in[li] * jnp.asarray(colscale)).astype(BF16)
            (qkv,) = in_proj(h, even_norm1[li], w, [w.shape[1]], [BF16])
            qkv = qkv.reshape(b, s, -1)
            nb = mw // LANES
            o_a = attention(qkv, 0, nb, 2 * nb, nb, bias_vecs, 0, "pair", moba=True)
            o_b = attention(qkv, 3 * nb, 4 * nb, 5 * nb, DIFF_HEADS, bias_vecs, MOBA_HEADS, "diff",
                            lam_params=diff_lambda[li], subln_g=diff_subln[li], lam_init=lam_init)
            wo = even_w_out
        else:
            w = odd_w_in[li]
            dw = DIL_HEADS * HEAD_DIM
            lat0 = 3 * dw
            colscale = np.ones((w.shape[1],), np.float32)
            colscale[:dw] = qscale
            kr_cols = w[:, lat0 + MLA_Q_RANK + MLA_KV_RANK:]
            w_all = jnp.concatenate([w * jnp.asarray(colscale), _swap_halves(kr_cols)],
                                    axis=1).astype(BF16)
            qkv, latent = in_proj(h, odd_norm1[li], w_all,
                                  [lat0, w_all.shape[1] - lat0], [BF16, F32])
            qkv = qkv.reshape(b, s, -1)
            nb = dw // LANES
            o_a = attention(qkv, 0, nb, 2 * nb, nb, bias_vecs, MOBA_HEADS + DIFF_HEADS, "pair")
            wa, wb = _mla_q_weights(mla_w_uq[li])
            wkv = mla_w_ukv[li].reshape(MLA_KV_RANK, MLA_HEADS, MLA_NOPE_DIM + MLA_V_DIM)
            wk = wkv[..., :MLA_NOPE_DIM].reshape(MLA_KV_RANK, -1).astype(BF16)
            wv = wkv[..., MLA_NOPE_DIM:].reshape(MLA_KV_RANK, -1).astype(BF16)
            q_m, k_m, v_m = mla_up(latent, mla_q_norm[li], mla_kv_norm[li], wa, wb, wk, wv,
                                   rope_tabs, s)
            o_b = mla_attention(q_m.reshape(b, s, -1), k_m.reshape(b, s, -1),
                                v_m.reshape(b, s, -1))
            wo = odd_w_out
        h = proj_residual([o_a.reshape(m, -1), o_b.reshape(m, -1)], wo, li, h)
        act = ffn_in(h.reshape(b, s, d), ffn_norm[layer], ffn_w_in, layer,
                     ffn_conv_w[layer], ffn_conv_b[layer])
        h = proj_residual([act.reshape(m, D_FF)], ffn_w_out, layer, h,
                          final_g=final_norm if layer == DEPTH - 1 else None)
    return h.reshape(b, s, d)
```

```python
import functools
import math

import numpy as np
import jax
import jax.numpy as jnp
from jax import lax
from jax.experimental import pallas as pl
from jax.experimental.pallas import tpu as pltpu

D_MODEL = 1024
DEPTH = 4
HEAD_DIM = 64
MOBA_HEADS = 8
MOBA_BLOCK = 256
MOBA_TOPK = 3
DIFF_HEADS = 4
DIL_HEADS = 8
DIL_CONFIGS = ((128, 1), (512, 4), (2048, 16))
MLA_HEADS = 4
MLA_Q_RANK = 256
MLA_KV_RANK = 128
MLA_NOPE_DIM = 128
MLA_ROPE_DIM = 64
MLA_V_DIM = 128
ROPE_THETA = 10000.0
REL_BUCKETS = 32
REL_MAX_DIST = 1024
N_BIAS_HEADS = MOBA_HEADS + DIFF_HEADS + DIL_HEADS
D_FF = 2816
EPS = 1e-6
NEG = -1e30
LOG2E = math.log2(math.e)

LANES = 128
TQ = 256
TK = 256
FF_STEP_ROWS = 1024
FF_CHUNK = 256
FF_ROWS = 512
VMEM_LIMIT = 56 * 1024 * 1024

F32 = jnp.float32
BF16 = jnp.bfloat16


def _params(n_axes, vmem=None):
    return pltpu.CompilerParams(dimension_semantics=("arbitrary",) * n_axes,
                                vmem_limit_bytes=vmem)


def _rms(x, g):
    ms = jnp.mean(x * x, axis=-1, keepdims=True)
    return x * lax.rsqrt(ms + EPS) * g


def _in_proj_kernel(x_ref, g_ref, w_ref, *o_refs, row_chunk):
    tm = x_ref.shape[0]
    g = g_ref[...]
    for r in range(tm // row_chunk):
        rows = slice(r * row_chunk, (r + 1) * row_chunk)
        xn = _rms(x_ref[rows, :], g).astype(BF16)
        acc = jnp.dot(xn, w_ref[...], preferred_element_type=F32)
        col = 0
        for o_ref in o_refs:
            width = o_ref.shape[1]
            o_ref[rows, :] = acc[:, col:col + width].astype(o_ref.dtype)
            col += width


def in_proj(x, g, w, out_widths, out_dtypes, tm=1024, row_chunk=512):
    m, d = x.shape
    return pl.pallas_call(
        functools.partial(_in_proj_kernel, row_chunk=row_chunk),
        grid=(m // tm,),
        in_specs=[pl.BlockSpec((tm, d), lambda i: (i, 0)),
                  pl.BlockSpec((1, d), lambda i: (0, 0)),
                  pl.BlockSpec(w.shape, lambda i: (0, 0))],
        out_specs=[pl.BlockSpec((tm, n), lambda i: (i, 0)) for n in out_widths],
        out_shape=[jax.ShapeDtypeStruct((m, n), dt) for n, dt in zip(out_widths, out_dtypes)],
        compiler_params=_params(1, VMEM_LIMIT),
        name="in_proj",
    )(x, g.reshape(1, d), w)


def _proj_res_kernel(*refs, n_in, final_norm):
    a_refs = refs[:n_in]
    w_ref, res_ref = refs[n_in], refs[n_in + 1]
    o_ref, wb_ref = refs[-2], refs[-1]

    @pl.when(pl.program_id(0) == 0)
    def _():
        wb_ref[...] = w_ref[...].astype(BF16)

    acc = res_ref[...]
    row = 0
    for a_ref in a_refs:
        k = a_ref.shape[1]
        acc = acc + jnp.dot(a_ref[...], wb_ref[row:row + k, :], preferred_element_type=F32)
        row += k
    if final_norm:
        acc = _rms(acc, refs[n_in + 2][...])
    o_ref[...] = acc


def proj_residual(acts, w_stack, layer, res, final_g=None, tm=512):
    m, d = res.shape
    n_in = len(acts)
    in_specs = [pl.BlockSpec((tm, a.shape[1]), lambda i: (i, 0)) for a in acts]
    in_specs += [pl.BlockSpec((None,) + w_stack.shape[1:], lambda i: (layer, 0, 0)),
                 pl.BlockSpec((tm, d), lambda i: (i, 0))]
    args = list(acts) + [w_stack, res]
    if final_g is not None:
        in_specs += [pl.BlockSpec((1, d), lambda i: (0, 0))]
        args += [final_g.reshape(1, d)]
    return pl.pallas_call(
        functools.partial(_proj_res_kernel, n_in=n_in, final_norm=final_g is not None),
        grid=(m // tm,),
        in_specs=in_specs,
        out_specs=pl.BlockSpec((tm, d), lambda i: (i, 0)),
        out_shape=jax.ShapeDtypeStruct((m, d), F32),
        scratch_shapes=[pltpu.VMEM(w_stack.shape[1:], BF16)],
        compiler_params=_params(1, VMEM_LIMIT),
        name="proj_residual",
    )(*args)


def _ffn_in_kernel(x_ref, g_ref, w_ref, cw_ref, cb_ref, o_ref, xn_ref, h1_ref, h2_ref):
    rows = x_ref.shape[1]
    first = pl.program_id(1) == 0
    for r in range(rows // FF_ROWS):
        rs = slice(r * FF_ROWS, (r + 1) * FF_ROWS)
        xn_ref[rs, :] = _rms(x_ref[0, rs, :], g_ref[...]).astype(BF16)
    sub = lax.broadcasted_iota(jnp.int32, (8, FF_CHUNK), 0)
    zeros = jnp.zeros((8, FF_CHUNK), F32)
    for c in range(D_FF // FF_CHUNK):
        cols = slice(c * FF_CHUNK, (c + 1) * FF_CHUNK)
        wu = w_ref[:, cols]
        wg = w_ref[:, D_FF + c * FF_CHUNK:D_FF + (c + 1) * FF_CHUNK]
        cw = cw_ref[:, cols]
        cb = cb_ref[:, cols]
        prev1 = jnp.where(first, zeros, h1_ref[c])
        prev2 = jnp.where(first, zeros, h2_ref[c])
        for r in range(rows // FF_ROWS):
            rs = slice(r * FF_ROWS, (r + 1) * FF_ROWS)
            xn = xn_ref[rs, :]
            u = jnp.dot(xn, wu, preferred_element_type=F32)
            gt = jnp.dot(xn, wg, preferred_element_type=F32)
            r1 = pltpu.roll(gt, 1, 0)
            r2 = pltpu.roll(gt, 2, 0)
            g1 = jnp.concatenate([jnp.where(sub >= 1, r1[:8], prev1), r1[8:]], axis=0)
            g2 = jnp.concatenate([jnp.where(sub >= 2, r2[:8], prev2), r2[8:]], axis=0)
            prev1, prev2 = r1[:8], r2[:8]
            z = cw[2:3] * gt + cw[1:2] * g1 + cw[0:1] * g2 + cb
            gelu = 0.5 * z * (1.0 + lax.erf(z * math.sqrt(0.5)))
            o_ref[0, rs, cols] = (gelu * u).astype(BF16)
        h1_ref[c] = prev1
        h2_ref[c] = prev2


def ffn_in(h3, g, w_stack, layer, conv_w, conv_b):
    b, s, d = h3.shape
    nc = D_FF // FF_CHUNK
    return pl.pallas_call(
        _ffn_in_kernel,
        grid=(b, s // FF_STEP_ROWS),
        in_specs=[pl.BlockSpec((1, FF_STEP_ROWS, d), lambda i, j: (i, j, 0)),
                  pl.BlockSpec((1, d), lambda i, j: (0, 0)),
                  pl.BlockSpec((None, d, 2 * D_FF), lambda i, j: (layer, 0, 0)),
                  pl.BlockSpec((3, D_FF), lambda i, j: (0, 0)),
                  pl.BlockSpec((1, D_FF), lambda i, j: (0, 0))],
        out_specs=pl.BlockSpec((1, FF_STEP_ROWS, D_FF), lambda i, j: (i, j, 0)),
        out_shape=jax.ShapeDtypeStruct((b, s, D_FF), BF16),
        scratch_shapes=[pltpu.VMEM((FF_STEP_ROWS, d), BF16),
                        pltpu.VMEM((nc, 8, FF_CHUNK), F32), pltpu.VMEM((nc, 8, FF_CHUNK), F32)],
        compiler_params=_params(2, VMEM_LIMIT),
        name="ffn_in",
    )(h3, g.reshape(1, d), w_stack, conv_w, conv_b.reshape(1, D_FF))


def _bucket_of_distance(s):
    max_exact = REL_BUCKETS // 2
    n_large = REL_BUCKETS - max_exact
    thresholds = []
    for k in range(1, n_large):
        t = max_exact * (REL_MAX_DIST / max_exact) ** (k / n_large)
        ti = int(round(t))
        thresholds.append(ti if abs(t - ti) < 1e-9 else int(math.ceil(t)))
    d = np.arange(s)
    large = max_exact + sum((d >= t).astype(np.int64) for t in thresholds)
    return np.where(d < max_exact, d, np.minimum(large, REL_BUCKETS - 1)).astype(np.int32)


def _dilated_log_multiplicity(s):
    d = np.arange(s)
    count = np.zeros(s, np.int64)
    for window, dil in DIL_CONFIGS:
        count += ((d % dil == 0) & (d // dil <= window // dil)).astype(np.int64)
    return np.where(count > 0, np.log(np.maximum(count, 1)), NEG).astype(np.float32)


def _fill_bias_strip(strip_ref, vec_ref, n, s):
    for h in range(n):
        tile = jnp.broadcast_to(vec_ref[h], (TQ, s + TQ))
        tile = pltpu.roll(tile, 0, 1, stride=1, stride_axis=0)
        strip_ref[h] = tile[:, TQ:]


def bias_vectors(rel_bias, s):
    bucket = _bucket_of_distance(s)
    per_dist = jnp.take(rel_bias.T.astype(F32), jnp.asarray(bucket), axis=1)
    logmult = jnp.asarray(_dilated_log_multiplicity(s))
    dil0 = MOBA_HEADS + DIFF_HEADS
    covered = logmult > 0.5 * NEG
    dil_rows = jnp.where(covered[None, :], per_dist[dil0:] + logmult[None, :], NEG)
    per_dist = jnp.concatenate([per_dist[:dil0], dil_rows], axis=0) * LOG2E
    nh = per_dist.shape[0]
    w = jnp.concatenate([jnp.full((nh, 1), NEG, F32), per_dist[:, ::-1],
                         jnp.full((nh, TQ - 1), NEG, F32)], axis=1)
    return w.reshape(nh, 1, s + TQ)


def _store_transposed(kt_ref, idx, k):
    s = k.shape[0]
    for j in range(s // TK):
        kt_ref[idx, :, j * TK:(j + 1) * TK] = k[j * TK:(j + 1) * TK, :].T


def _attn_kernel(*refs, mode, moba, lam_init, s):
    if mode == "diff":
        (q_ref, k_ref, v_ref, bvec_ref, lam_ref, sg_ref, o_ref,
         kt_ref, sc_ref, p_ref, vone_ref, b_ref) = refs
    elif moba:
        (q_ref, k_ref, v_ref, bvec_ref, o_ref,
         kt_ref, sc_ref, p_ref, vone_ref, b_ref, drop_ref) = refs
    else:
        q_ref, k_ref, v_ref, bvec_ref, o_ref, kt_ref, sc_ref, p_ref, vone_ref, b_ref = refs

    @pl.when(pl.program_id(1) == 0)
    def _():
        _fill_bias_strip(b_ref, bvec_ref, 2 if mode == "pair" else 1, s)

    vone_ref[:, :LANES] = v_ref[0]
    vone_ref[:, LANES:] = jnp.ones((s, LANES), BF16)
    nblk = s // TK
    lane = lax.broadcasted_iota(jnp.int32, (TQ, LANES), 1)
    halves = (lane < HEAD_DIM, lane >= HEAD_DIM)
    nt = (((1,), (1,)), ((), ()))

    if moba:
        k_all = k_ref[0]
        q_all = q_ref[0]
        rowblk = lax.broadcasted_iota(jnp.int32, (s, LANES), 0) // TK
        lane_s = lax.broadcasted_iota(jnp.int32, (s, LANES), 1)
        kmean = jnp.mean(k_all.astype(F32).reshape(nblk, TK, LANES), axis=1)
        k_hi = kmean.astype(BF16)
        rem = kmean - k_hi.astype(F32)
        k_mid = rem.astype(BF16)
        k_lo = (rem - k_mid.astype(F32)).astype(BF16)
        kmean3 = jnp.concatenate([k_hi, k_mid, k_lo], axis=1)
        qblk = lax.broadcasted_iota(jnp.int32, (nblk, s), 1) // TQ
        blk = lax.broadcasted_iota(jnp.int32, (nblk, s), 0)
        for c in range(2):
            base = HEAD_DIM if c == 0 else 0
            in_half = (lane_s < HEAD_DIM) if c == 0 else (lane_s >= HEAD_DIM)
            onehot = jnp.where(lane_s - base == rowblk, 1.0, 0.0).astype(BF16)
            _store_transposed(kt_ref, c, jnp.where(in_half, k_all, onehot))
            qh = jnp.where(in_half, q_all, jnp.zeros_like(q_all))
            gate = lax.dot_general(kmean3, jnp.concatenate([qh, qh, qh], axis=1), nt,
                                   preferred_element_type=F32)
            rank = jnp.zeros((nblk, s), jnp.int32)
            for n in range(nblk - 1):
                gn = gate[n:n + 1, :]
                beats = jnp.where(gn > gate, 1, jnp.where(gn == gate, jnp.where(n < blk, 1, 0), 0))
                rank = rank + jnp.where(n < qblk, beats, 0)
            drop = jnp.where(blk < qblk, jnp.where(rank >= MOBA_TOPK, NEG, 0.0), 0.0)
            drop_ref[c] = jnp.zeros((LANES, s), F32)
            drop_ref[c, base:base + nblk, :] = drop

    else:
        _store_transposed(kt_ref, 0, k_ref[0])

    if mode == "diff":
        lp = lam_ref[...]
        lam = (jnp.exp(jnp.sum(lp[0:1] * lp[1:2], axis=1, keepdims=True))
               - jnp.exp(jnp.sum(lp[2:3] * lp[3:4], axis=1, keepdims=True)) + lam_init)

    units = [(i, c) for i in range(nblk) for c in range(2)]

    def scores(u):
        i, c = units[u]
        nk = (i + 1) * TK
        q = q_ref[0, i * TQ:(i + 1) * TQ, :]
        qc = jnp.where(halves[c], q, jnp.zeros_like(q))
        if moba and i > 0:
            sel = drop_ref[c, :, i * TQ:(i + 1) * TQ].T
            qc = jnp.where(halves[c], q, sel.astype(BF16))
        sc_ref[u % 2, :, 0:nk] = (jnp.dot(qc, kt_ref[c if moba else 0, :, 0:nk],
                                          preferred_element_type=F32)
                                  + b_ref[c if mode == "pair" else 0, :, s - nk:])

    def softmax(u):
        nk = (units[u][0] + 1) * TK
        m = jnp.max(sc_ref[u % 2, :, 0:nk], axis=1, keepdims=True)
        p = jnp.exp2(sc_ref[u % 2, :, 0:nk] - m)
        p_ref[u % 2, :, 0:nk] = p.astype(BF16)

    outs = []

    def weighted_values(u):
        i, c = units[u]
        nk = (i + 1) * TK
        o = jnp.dot(p_ref[u % 2, :, 0:nk], vone_ref[0:nk, :], preferred_element_type=F32)
        outs.append(o[:, :LANES] / o[:, LANES:])
        if c == 1:
            if mode == "pair":
                res = jnp.where(halves[0], outs[0], outs[1])
            else:
                res = _rms(outs[0] - lam * outs[1], sg_ref[...]) * (1.0 - lam_init)
            o_ref[0, i * TQ:(i + 1) * TQ, :] = res.astype(o_ref.dtype)
            outs.clear()

    scores(0)
    for u in range(len(units)):
        if u + 1 < len(units):
            scores(u + 1)
        softmax(u)
        if u > 0:
            weighted_values(u - 1)
    weighted_values(len(units) - 1)


def attention(qkv, col_q, col_k, col_v, n_blocks, bias_vecs, head0, mode, moba=False,
              lam_params=None, subln_g=None, lam_init=0.0):
    b, s, _ = qkv.shape
    nb = 2 if mode == "pair" else 1
    seq = lambda col: pl.BlockSpec((1, s, LANES), lambda p, bb: (bb, 0, col + p))
    in_specs = [seq(col_q), seq(col_k), seq(col_v),
                pl.BlockSpec((nb, 1, s + TQ), lambda p, bb: (head0 // nb + p, 0, 0))]
    args = [qkv, qkv, qkv, bias_vecs]
    if mode == "diff":
        in_specs += [pl.BlockSpec((4, HEAD_DIM), lambda p, bb: (0, 0)),
                     pl.BlockSpec((1, LANES), lambda p, bb: (0, 0))]
        args += [lam_params.astype(F32), subln_g.reshape(1, LANES)]
    scratch = [pltpu.VMEM((2 if moba else 1, LANES, s), BF16),
               pltpu.VMEM((2, TQ, s), F32), pltpu.VMEM((2, TQ, s), BF16),
               pltpu.VMEM((s, 2 * LANES), BF16), pltpu.VMEM((nb, TQ, s), F32)]
    if moba:
        scratch += [pltpu.VMEM((2, LANES, s), F32)]
    return pl.pallas_call(
        functools.partial(_attn_kernel, mode=mode, moba=moba, lam_init=lam_init, s=s),
        grid=(n_blocks, b),
        in_specs=in_specs,
        out_specs=pl.BlockSpec((1, s, LANES), lambda p, bb: (bb, 0, p)),
        out_shape=jax.ShapeDtypeStruct((b, s, n_blocks * LANES), BF16),
        scratch_shapes=scratch,
        compiler_params=_params(2, VMEM_LIMIT),
        name="attn_" + mode + ("_moba" if moba else ""),
    )(*args)


def _mla_up_kernel(lat_ref, qg_ref, kvg_ref, wa_ref, wb_ref, wk_ref, wv_ref, cq_ref, sq_ref,
                   ck_ref, sk_ref, q_ref, k_ref, v_ref, *, per_seq):
    lat = lat_ref[...]
    tm = lat.shape[0]
    r0 = pl.multiple_of((pl.program_id(0) % per_seq) * tm, tm)
    cqn = _rms(lat[:, :MLA_Q_RANK], qg_ref[...]).astype(BF16)
    ckvn = _rms(lat[:, MLA_Q_RANK:MLA_Q_RANK + MLA_KV_RANK], kvg_ref[...]).astype(BF16)
    qa = jnp.dot(cqn, wa_ref[...], preferred_element_type=F32)
    qb = jnp.dot(cqn, wb_ref[...], preferred_element_type=F32)
    kn = jnp.dot(ckvn, wk_ref[...], preferred_element_type=F32).astype(BF16)
    v_ref[...] = jnp.dot(ckvn, wv_ref[...], preferred_element_type=F32).astype(BF16)
    x = lat[:, MLA_Q_RANK + MLA_KV_RANK:]
    kr = (x * ck_ref[pl.ds(r0, tm), :]
          + pltpu.roll(x, MLA_ROPE_DIM, 1) * sk_ref[pl.ds(r0, tm), :]).astype(BF16)
    cq = cq_ref[pl.ds(r0, tm), :]
    sq = sq_ref[pl.ds(r0, tm), :]
    hw = 2 * LANES
    for h in range(MLA_HEADS):
        q_ref[:, h * hw:(h + 1) * hw] = (qa[:, h * hw:(h + 1) * hw] * cq
                                         + qb[:, h * hw:(h + 1) * hw] * sq).astype(BF16)
        k_ref[:, h * hw:h * hw + LANES] = kn[:, h * LANES:(h + 1) * LANES]
        k_ref[:, h * hw + LANES:(h + 1) * hw] = kr


def mla_up(latent, q_norm, kv_norm, wa, wb, wk, wv, rope_tabs, s, tm=1024):
    m = latent.shape[0]
    cq, sq, ck, sk = rope_tabs
    per_seq = s // tm
    tab = lambda w: pl.BlockSpec((s, w), lambda i: (0, 0))
    full = lambda a: pl.BlockSpec(a.shape, lambda i: (0, 0))
    hw = 2 * LANES
    rows = lambda w: pl.BlockSpec((tm, w), lambda i: (i, 0))
    return pl.pallas_call(
        functools.partial(_mla_up_kernel, per_seq=per_seq),
        grid=(m // tm,),
        in_specs=[rows(latent.shape[1]),
                  pl.BlockSpec((1, MLA_Q_RANK), lambda i: (0, 0)),
                  pl.BlockSpec((1, MLA_KV_RANK), lambda i: (0, 0)),
                  full(wa), full(wb), full(wk), full(wv), tab(hw), tab(hw), tab(LANES), tab(LANES)],
        out_specs=[rows(MLA_HEADS * hw), rows(MLA_HEADS * hw), rows(MLA_HEADS * LANES)],
        out_shape=[jax.ShapeDtypeStruct((m, MLA_HEADS * hw), BF16),
                   jax.ShapeDtypeStruct((m, MLA_HEADS * hw), BF16),
                   jax.ShapeDtypeStruct((m, MLA_HEADS * LANES), BF16)],
        compiler_params=_params(1, VMEM_LIMIT),
        name="mla_up",
    )(latent, q_norm.reshape(1, -1), kv_norm.reshape(1, -1), wa, wb, wk, wv, cq, sq, ck, sk)


def _mla_attn_kernel(q_ref, k_ref, v_ref, o_ref, kt_ref, sc_ref, p_ref, vone_ref, *, s):
    scale = (MLA_NOPE_DIM + MLA_ROPE_DIM) ** -0.5 * LOG2E
    _store_transposed(kt_ref, 0, k_ref[0])
    vone_ref[:, :LANES] = v_ref[0]
    vone_ref[:, LANES:] = jnp.ones((s, LANES), BF16)
    row = lax.broadcasted_iota(jnp.int32, (TQ, TK), 0)
    col = lax.broadcasted_iota(jnp.int32, (TQ, TK), 1)
    causal = row >= col
    nblk = s // TQ

    def scores(i):
        nk = (i + 1) * TK
        q = q_ref[0, i * TQ:(i + 1) * TQ, :]
        sc = jnp.dot(q, kt_ref[0, :, 0:nk], preferred_element_type=F32) * scale
        if i > 0:
            sc_ref[i % 2, :, 0:nk - TK] = sc[:, :nk - TK]
        sc_ref[i % 2, :, nk - TK:nk] = jnp.where(causal, sc[:, nk - TK:], NEG)

    def softmax(i):
        nk = (i + 1) * TK
        m = jnp.max(sc_ref[i % 2, :, 0:nk], axis=1, keepdims=True)
        p = jnp.exp2(sc_ref[i % 2, :, 0:nk] - m)
        p_ref[i % 2, :, 0:nk] = p.astype(BF16)

    def weighted_values(i):
        nk = (i + 1) * TK
        o = jnp.dot(p_ref[i % 2, :, 0:nk], vone_ref[0:nk, :], preferred_element_type=F32)
        o_ref[0, i * TQ:(i + 1) * TQ, :] = (o[:, :LANES] / o[:, LANES:]).astype(o_ref.dtype)

    scores(0)
    for i in range(nblk):
        if i + 1 < nblk:
            scores(i + 1)
        softmax(i)
        if i > 0:
            weighted_values(i - 1)
    weighted_values(nblk - 1)


def mla_attention(q, k, v):
    b, s, _ = q.shape
    hw = 2 * LANES
    return pl.pallas_call(
        functools.partial(_mla_attn_kernel, s=s),
        grid=(MLA_HEADS, b),
        in_specs=[pl.BlockSpec((1, s, hw), lambda h, bb: (bb, 0, h)),
                  pl.BlockSpec((1, s, hw), lambda h, bb: (bb, 0, h)),
                  pl.BlockSpec((1, s, LANES), lambda h, bb: (bb, 0, h))],
        out_specs=pl.BlockSpec((1, s, LANES), lambda h, bb: (bb, 0, h)),
        out_shape=jax.ShapeDtypeStruct((b, s, MLA_HEADS * LANES), BF16),
        scratch_shapes=[pltpu.VMEM((1, hw, s), BF16),
                        pltpu.VMEM((2, TQ, s), F32), pltpu.VMEM((2, TQ, s), BF16),
                        pltpu.VMEM((s, 2 * LANES), BF16)],
        compiler_params=_params(2, VMEM_LIMIT),
        name="mla_attn",
    )(q, k, v)


def _rope_tables(s):
    half = MLA_ROPE_DIM // 2
    freq = ROPE_THETA ** (-jnp.arange(half, dtype=F32) / half)
    ang = jnp.arange(s, dtype=F32)[:, None] * freq[None, :]
    cos, sin = jnp.cos(ang), jnp.sin(ang)
    cos2 = jnp.concatenate([cos, cos], axis=1)
    sin2 = jnp.concatenate([-sin, sin], axis=1)
    z64 = jnp.zeros((s, MLA_ROPE_DIM), F32)
    cq = jnp.concatenate([jnp.ones((s, MLA_NOPE_DIM), F32), cos2, z64], axis=1)
    sq = jnp.concatenate([jnp.zeros((s, MLA_NOPE_DIM), F32), sin2, z64], axis=1)
    ck = jnp.concatenate([cos2, z64], axis=1)
    sk = jnp.concatenate([sin2, z64], axis=1)
    return cq, sq, ck, sk


def _swap_halves(w):
    half = w.shape[-1] // 2
    return jnp.concatenate([w[..., half:], w[..., :half]], axis=-1)


def _mla_q_weights(w_uq):
    r = w_uq.shape[0]
    w = w_uq.reshape(r, MLA_HEADS, MLA_NOPE_DIM + MLA_ROPE_DIM)
    nope, rope = w[..., :MLA_NOPE_DIM], w[..., MLA_NOPE_DIM:]
    z64 = jnp.zeros((r, MLA_HEADS, MLA_ROPE_DIM), w.dtype)
    z128 = jnp.zeros((r, MLA_HEADS, MLA_NOPE_DIM), w.dtype)
    wa = jnp.concatenate([nope, rope, z64], axis=-1).reshape(r, -1)
    wb = jnp.concatenate([z128, _swap_halves(rope), z64], axis=-1).reshape(r, -1)
    return wa.astype(BF16), wb.astype(BF16)


def kernel(x, rel_bias, even_norm1, even_w_in, diff_lambda, diff_subln, even_w_out,
           odd_norm1, odd_w_in, mla_q_norm, mla_w_uq, mla_kv_norm, mla_w_ukv, odd_w_out,
           ffn_norm, ffn_w_in, ffn_conv_w, ffn_conv_b, ffn_w_out, final_norm):
    b, s, d = x.shape
    m = b * s
    bias_vecs = bias_vectors(rel_bias, s)
    rope_tabs = _rope_tables(s)
    qscale = HEAD_DIM ** -0.5 * LOG2E
    ffn_w_in_bf16 = ffn_w_in.astype(BF16)
    h = x.reshape(m, d)
    for layer in range(DEPTH):
        li = layer // 2
        if layer % 2 == 0:
            lam_init = 0.8 - 0.6 * math.exp(-0.3 * layer)
            mw, dw = MOBA_HEADS * HEAD_DIM, DIFF_HEADS * 2 * HEAD_DIM
            colscale = np.ones((3 * mw + 3 * dw,), np.float32)
            colscale[:mw] = qscale
            colscale[3 * mw:3 * mw + dw] = qscale
            w = (even_w_in[li] * jnp.asarray(colscale)).astype(BF16)
            (qkv,) = in_proj(h, even_norm1[li], w, [w.shape[1]], [BF16])
            qkv = qkv.reshape(b, s, -1)
            nb = mw // LANES
            o_a = attention(qkv, 0, nb, 2 * nb, nb, bias_vecs, 0, "pair", moba=True)
            o_b = attention(qkv, 3 * nb, 4 * nb, 5 * nb, DIFF_HEADS, bias_vecs, MOBA_HEADS, "diff",
                            lam_params=diff_lambda[li], subln_g=diff_subln[li], lam_init=lam_init)
            wo = even_w_out
        else:
            w = odd_w_in[li]
            dw = DIL_HEADS * HEAD_DIM
            lat0 = 3 * dw
            colscale = np.ones((w.shape[1],), np.float32)
            colscale[:dw] = qscale
            kr_cols = w[:, lat0 + MLA_Q_RANK + MLA_KV_RANK:]
            w_all = jnp.concatenate([w * jnp.asarray(colscale), _swap_halves(kr_cols)],
                                    axis=1).astype(BF16)
            qkv, latent = in_proj(h, odd_norm1[li], w_all,
                                  [lat0, w_all.shape[1] - lat0], [BF16, F32])
            qkv = qkv.reshape(b, s, -1)
            nb = dw // LANES
            o_a = attention(qkv, 0, nb, 2 * nb, nb, bias_vecs, MOBA_HEADS + DIFF_HEADS, "pair")
            wa, wb = _mla_q_weights(mla_w_uq[li])
            wkv = mla_w_ukv[li].reshape(MLA_KV_RANK, MLA_HEADS, MLA_NOPE_DIM + MLA_V_DIM)
            wk = wkv[..., :MLA_NOPE_DIM].reshape(MLA_KV_RANK, -1).astype(BF16)
            wv = wkv[..., MLA_NOPE_DIM:].reshape(MLA_KV_RANK, -1).astype(BF16)
            q_m, k_m, v_m = mla_up(latent, mla_q_norm[li], mla_kv_norm[li], wa, wb, wk, wv,
                                   rope_tabs, s)
            o_b = mla_attention(q_m.reshape(b, s, -1), k_m.reshape(b, s, -1),
                                v_m.reshape(b, s, -1))
            wo = odd_w_out
        h = proj_residual([o_a.reshape(m, -1), o_b.reshape(m, -1)], wo, li, h)
        act = ffn_in(h.reshape(b, s, d), ffn_norm[layer], ffn_w_in_bf16, layer,
                     ffn_conv_w[layer], ffn_conv_b[layer])
        h = proj_residual([act.reshape(m, D_FF)], ffn_w_out, layer, h,
                          final_g=final_norm if layer == DEPTH - 1 else None)
    return h.reshape(b, s, d)
```

```python
import functools
import math

import numpy as np
import jax
import jax.numpy as jnp
from jax import lax
from jax.experimental import pallas as pl
from jax.experimental.pallas import tpu as pltpu

D_MODEL = 1024
DEPTH = 4
HEAD_DIM = 64
MOBA_HEADS = 8
MOBA_BLOCK = 256
MOBA_TOPK = 3
DIFF_HEADS = 4
DIL_HEADS = 8
DIL_CONFIGS = ((128, 1), (512, 4), (2048, 16))
MLA_HEADS = 4
MLA_Q_RANK = 256
MLA_KV_RANK = 128
MLA_NOPE_DIM = 128
MLA_ROPE_DIM = 64
MLA_V_DIM = 128
ROPE_THETA = 10000.0
REL_BUCKETS = 32
REL_MAX_DIST = 1024
N_BIAS_HEADS = MOBA_HEADS + DIFF_HEADS + DIL_HEADS
D_FF = 2816
EPS = 1e-6
NEG = -1e30
LOG2E = math.log2(math.e)

LANES = 128
TQ = 256
TK = 256
FF_STEP_ROWS = 1024
FF_CHUNK = 256
FF_ROWS = 512
VMEM_LIMIT = 56 * 1024 * 1024

F32 = jnp.float32
BF16 = jnp.bfloat16


def _params(n_axes, vmem=None):
    return pltpu.CompilerParams(dimension_semantics=("arbitrary",) * n_axes,
                                vmem_limit_bytes=vmem)


def _rms(x, g):
    ms = jnp.mean(x * x, axis=-1, keepdims=True)
    return x * lax.rsqrt(ms + EPS) * g


def _in_proj_kernel(x_ref, g_ref, w_ref, *o_refs, row_chunk):
    tm = x_ref.shape[0]
    g = g_ref[...]
    for r in range(tm // row_chunk):
        rows = slice(r * row_chunk, (r + 1) * row_chunk)
        xn = _rms(x_ref[rows, :], g).astype(BF16)
        acc = jnp.dot(xn, w_ref[...], preferred_element_type=F32)
        col = 0
        for o_ref in o_refs:
            width = o_ref.shape[1]
            o_ref[rows, :] = acc[:, col:col + width].astype(o_ref.dtype)
            col += width


def in_proj(x, g, w, out_widths, out_dtypes, tm=1024, row_chunk=512):
    m, d = x.shape
    return pl.pallas_call(
        functools.partial(_in_proj_kernel, row_chunk=row_chunk),
        grid=(m // tm,),
        in_specs=[pl.BlockSpec((tm, d), lambda i: (i, 0)),
                  pl.BlockSpec((1, d), lambda i: (0, 0)),
                  pl.BlockSpec(w.shape, lambda i: (0, 0))],
        out_specs=[pl.BlockSpec((tm, n), lambda i: (i, 0)) for n in out_widths],
        out_shape=[jax.ShapeDtypeStruct((m, n), dt) for n, dt in zip(out_widths, out_dtypes)],
        compiler_params=_params(1, VMEM_LIMIT),
        name="in_proj",
    )(x, g.reshape(1, d), w)


def _proj_res_kernel(*refs, n_in, final_norm):
    a_refs = refs[:n_in]
    w_ref, res_ref = refs[n_in], refs[n_in + 1]
    o_ref, wb_ref = refs[-2], refs[-1]

    @pl.when(pl.program_id(0) == 0)
    def _():
        wb_ref[...] = w_ref[...].astype(BF16)

    acc = res_ref[...]
    row = 0
    for a_ref in a_refs:
        k = a_ref.shape[1]
        acc = acc + jnp.dot(a_ref[...], wb_ref[row:row + k, :], preferred_element_type=F32)
        row += k
    if final_norm:
        acc = _rms(acc, refs[n_in + 2][...])
    o_ref[...] = acc


def proj_residual(acts, w_stack, layer, res, final_g=None, tm=512):
    m, d = res.shape
    n_in = len(acts)
    in_specs = [pl.BlockSpec((tm, a.shape[1]), lambda i: (i, 0)) for a in acts]
    in_specs += [pl.BlockSpec((None,) + w_stack.shape[1:], lambda i: (layer, 0, 0)),
                 pl.BlockSpec((tm, d), lambda i: (i, 0))]
    args = list(acts) + [w_stack, res]
    if final_g is not None:
        in_specs += [pl.BlockSpec((1, d), lambda i: (0, 0))]
        args += [final_g.reshape(1, d)]
    return pl.pallas_call(
        functools.partial(_proj_res_kernel, n_in=n_in, final_norm=final_g is not None),
        grid=(m // tm,),
        in_specs=in_specs,
        out_specs=pl.BlockSpec((tm, d), lambda i: (i, 0)),
        out_shape=jax.ShapeDtypeStruct((m, d), F32),
        scratch_shapes=[pltpu.VMEM(w_stack.shape[1:], BF16)],
        compiler_params=_params(1, VMEM_LIMIT),
        name="proj_residual",
    )(*args)


def _ffn_in_kernel(x_ref, g_ref, w_ref, cw_ref, cb_ref, o_ref, xn_ref, h1_ref, h2_ref):
    rows = x_ref.shape[1]
    first = pl.program_id(1) == 0
    for r in range(rows // FF_ROWS):
        rs = slice(r * FF_ROWS, (r + 1) * FF_ROWS)
        xn_ref[rs, :] = _rms(x_ref[0, rs, :], g_ref[...]).astype(BF16)
    sub = lax.broadcasted_iota(jnp.int32, (8, FF_CHUNK), 0)
    zeros = jnp.zeros((8, FF_CHUNK), F32)
    for c in range(D_FF // FF_CHUNK):
        cols = slice(c * FF_CHUNK, (c + 1) * FF_CHUNK)
        wu = w_ref[:, cols]
        wg = w_ref[:, D_FF + c * FF_CHUNK:D_FF + (c + 1) * FF_CHUNK]
        cw = cw_ref[:, cols]
        cb = cb_ref[:, cols]
        prev1 = jnp.where(first, zeros, h1_ref[c])
        prev2 = jnp.where(first, zeros, h2_ref[c])
        for r in range(rows // FF_ROWS):
            rs = slice(r * FF_ROWS, (r + 1) * FF_ROWS)
            xn = xn_ref[rs, :]
            u = jnp.dot(xn, wu, preferred_element_type=F32)
            gt = jnp.dot(xn, wg, preferred_element_type=F32)
            r1 = pltpu.roll(gt, 1, 0)
            r2 = pltpu.roll(gt, 2, 0)
            g1 = jnp.concatenate([jnp.where(sub >= 1, r1[:8], prev1), r1[8:]], axis=0)
            g2 = jnp.concatenate([jnp.where(sub >= 2, r2[:8], prev2), r2[8:]], axis=0)
            prev1, prev2 = r1[:8], r2[:8]
            z = cw[2:3] * gt + cw[1:2] * g1 + cw[0:1] * g2 + cb
            gelu = 0.5 * z * (1.0 + lax.erf(z * math.sqrt(0.5)))
            o_ref[0, rs, cols] = (gelu * u).astype(BF16)
        h1_ref[c] = prev1
        h2_ref[c] = prev2


def ffn_in(h3, g, w_stack, layer, conv_w, conv_b):
    b, s, d = h3.shape
    nc = D_FF // FF_CHUNK
    return pl.pallas_call(
        _ffn_in_kernel,
        grid=(b, s // FF_STEP_ROWS),
        in_specs=[pl.BlockSpec((1, FF_STEP_ROWS, d), lambda i, j: (i, j, 0)),
                  pl.BlockSpec((1, d), lambda i, j: (0, 0)),
                  pl.BlockSpec((None, d, 2 * D_FF), lambda i, j: (layer, 0, 0)),
                  pl.BlockSpec((3, D_FF), lambda i, j: (0, 0)),
                  pl.BlockSpec((1, D_FF), lambda i, j: (0, 0))],
        out_specs=pl.BlockSpec((1, FF_STEP_ROWS, D_FF), lambda i, j: (i, j, 0)),
        out_shape=jax.ShapeDtypeStruct((b, s, D_FF), BF16),
        scratch_shapes=[pltpu.VMEM((FF_STEP_ROWS, d), BF16),
                        pltpu.VMEM((nc, 8, FF_CHUNK), F32), pltpu.VMEM((nc, 8, FF_CHUNK), F32)],
        compiler_params=_params(2, VMEM_LIMIT),
        name="ffn_in",
    )(h3, g.reshape(1, d), w_stack, conv_w, conv_b.reshape(1, D_FF))


def _bucket_of_distance(s):
    max_exact = REL_BUCKETS // 2
    n_large = REL_BUCKETS - max_exact
    thresholds = []
    for k in range(1, n_large):
        t = max_exact * (REL_MAX_DIST / max_exact) ** (k / n_large)
        ti = int(round(t))
        thresholds.append(ti if abs(t - ti) < 1e-9 else int(math.ceil(t)))
    d = np.arange(s)
    large = max_exact + sum((d >= t).astype(np.int64) for t in thresholds)
    return np.where(d < max_exact, d, np.minimum(large, REL_BUCKETS - 1)).astype(np.int32)


def _dilated_log_multiplicity(s):
    d = np.arange(s)
    count = np.zeros(s, np.int64)
    for window, dil in DIL_CONFIGS:
        count += ((d % dil == 0) & (d // dil <= window // dil)).astype(np.int64)
    return np.where(count > 0, np.log(np.maximum(count, 1)), NEG).astype(np.float32)


def _fill_bias_strip(strip_ref, vec_ref, n, s):
    for h in range(n):
        tile = jnp.broadcast_to(vec_ref[h], (TQ, s + TQ))
        tile = pltpu.roll(tile, 0, 1, stride=1, stride_axis=0)
        strip_ref[h] = tile[:, TQ:]


def bias_vectors(rel_bias, s):
    bucket = _bucket_of_distance(s)
    per_dist = jnp.take(rel_bias.T.astype(F32), jnp.asarray(bucket), axis=1)
    logmult = jnp.asarray(_dilated_log_multiplicity(s))
    dil0 = MOBA_HEADS + DIFF_HEADS
    covered = logmult > 0.5 * NEG
    dil_rows = jnp.where(covered[None, :], per_dist[dil0:] + logmult[None, :], NEG)
    per_dist = jnp.concatenate([per_dist[:dil0], dil_rows], axis=0) * LOG2E
    nh = per_dist.shape[0]
    w = jnp.concatenate([jnp.full((nh, 1), NEG, F32), per_dist[:, ::-1],
                         jnp.full((nh, TQ - 1), NEG, F32)], axis=1)
    return w.reshape(nh, 1, s + TQ)


def _store_transposed(kt_ref, idx, k):
    s = k.shape[0]
    for j in range(s // TK):
        kt_ref[idx, :, j * TK:(j + 1) * TK] = k[j * TK:(j + 1) * TK, :].T


def _block_order(n):
    return list(range(1, n, 2)) + list(range((n - 1) // 2 * 2, -1, -2))


def _attn_kernel(*refs, mode, moba, lam_init, s):
    if mode == "diff":
        (q_ref, k_ref, v_ref, bvec_ref, lam_ref, sg_ref, o_ref,
         kt_ref, sc_ref, p_ref, vone_ref, b_ref) = refs
    elif moba:
        (q_ref, k_ref, v_ref, bvec_ref, o_ref,
         kt_ref, sc_ref, p_ref, vone_ref, b_ref, drop_ref) = refs
    else:
        q_ref, k_ref, v_ref, bvec_ref, o_ref, kt_ref, sc_ref, p_ref, vone_ref, b_ref = refs

    @pl.when(pl.program_id(1) == 0)
    def _():
        _fill_bias_strip(b_ref, bvec_ref, 2 if mode == "pair" else 1, s)

    vone_ref[:, :LANES] = v_ref[0]
    vone_ref[:, LANES:] = jnp.ones((s, LANES), BF16)
    nblk = s // TK
    lane = lax.broadcasted_iota(jnp.int32, (TQ, LANES), 1)
    halves = (lane < HEAD_DIM, lane >= HEAD_DIM)
    nt = (((1,), (1,)), ((), ()))

    if moba:
        k_all = k_ref[0]
        q_all = q_ref[0]
        rowblk = lax.broadcasted_iota(jnp.int32, (s, LANES), 0) // TK
        lane_s = lax.broadcasted_iota(jnp.int32, (s, LANES), 1)
        kmean = jnp.mean(k_all.astype(F32).reshape(nblk, TK, LANES), axis=1)
        k_hi = kmean.astype(BF16)
        rem = kmean - k_hi.astype(F32)
        k_mid = rem.astype(BF16)
        k_lo = (rem - k_mid.astype(F32)).astype(BF16)
        kmean3 = jnp.concatenate([k_hi, k_mid, k_lo], axis=1)
        qblk = lax.broadcasted_iota(jnp.int32, (nblk, s), 1) // TQ
        blk = lax.broadcasted_iota(jnp.int32, (nblk, s), 0)
        for c in range(2):
            base = HEAD_DIM if c == 0 else 0
            in_half = (lane_s < HEAD_DIM) if c == 0 else (lane_s >= HEAD_DIM)
            onehot = jnp.where(lane_s - base == rowblk, 1.0, 0.0).astype(BF16)
            _store_transposed(kt_ref, c, jnp.where(in_half, k_all, onehot))
            qh = jnp.where(in_half, q_all, jnp.zeros_like(q_all))
            gate = lax.dot_general(kmean3, jnp.concatenate([qh, qh, qh], axis=1), nt,
                                   preferred_element_type=F32)
            rank = jnp.zeros((nblk, s), jnp.int32)
            for n in range(nblk - 1):
                gn = gate[n:n + 1, :]
                beats = jnp.where(gn > gate, 1, jnp.where(gn == gate, jnp.where(n < blk, 1, 0), 0))
                rank = rank + jnp.where(n < qblk, beats, 0)
            drop = jnp.where(blk < qblk, jnp.where(rank >= MOBA_TOPK, NEG, 0.0), 0.0)
            drop_ref[c] = jnp.zeros((LANES, s), F32)
            drop_ref[c, base:base + nblk, :] = drop

    else:
        _store_transposed(kt_ref, 0, k_ref[0])

    if mode == "diff":
        lp = lam_ref[...]
        lam = (jnp.exp(jnp.sum(lp[0:1] * lp[1:2], axis=1, keepdims=True))
               - jnp.exp(jnp.sum(lp[2:3] * lp[3:4], axis=1, keepdims=True)) + lam_init)

    units = [(i, c) for i in _block_order(nblk) for c in range(2)]

    def scores(u):
        i, c = units[u]
        nk = (i + 1) * TK
        q = q_ref[0, i * TQ:(i + 1) * TQ, :]
        qc = jnp.where(halves[c], q, jnp.zeros_like(q))
        if moba and i > 0:
            sel = drop_ref[c, :, i * TQ:(i + 1) * TQ].T
            qc = jnp.where(halves[c], q, sel.astype(BF16))
        sc_ref[u % 2, :, 0:nk] = (jnp.dot(qc, kt_ref[c if moba else 0, :, 0:nk],
                                          preferred_element_type=F32)
                                  + b_ref[c if mode == "pair" else 0, :, s - nk:])

    def softmax(u):
        nk = (units[u][0] + 1) * TK
        m = jnp.max(sc_ref[u % 2, :, 0:nk], axis=1, keepdims=True)
        p = jnp.exp2(sc_ref[u % 2, :, 0:nk] - m)
        p_ref[u % 2, :, 0:nk] = p.astype(BF16)

    outs = []

    def weighted_values(u):
        i, c = units[u]
        nk = (i + 1) * TK
        o = jnp.dot(p_ref[u % 2, :, 0:nk], vone_ref[0:nk, :], preferred_element_type=F32)
        outs.append(o[:, :LANES] / o[:, LANES:])
        if c == 1:
            if mode == "pair":
                res = jnp.where(halves[0], outs[0], outs[1])
            else:
                res = _rms(outs[0] - lam * outs[1], sg_ref[...]) * (1.0 - lam_init)
            o_ref[0, i * TQ:(i + 1) * TQ, :] = res.astype(o_ref.dtype)
            outs.clear()

    scores(0)
    for u in range(len(units)):
        if u + 1 < len(units):
            scores(u + 1)
        softmax(u)
        if u > 0:
            weighted_values(u - 1)
    weighted_values(len(units) - 1)


def attention(qkv, col_q, col_k, col_v, n_blocks, bias_vecs, head0, mode, moba=False,
              lam_params=None, subln_g=None, lam_init=0.0):
    b, s, _ = qkv.shape
    nb = 2 if mode == "pair" else 1
    seq = lambda col: pl.BlockSpec((1, s, LANES), lambda p, bb: (bb, 0, col + p))
    in_specs = [seq(col_q), seq(col_k), seq(col_v),
                pl.BlockSpec((nb, 1, s + TQ), lambda p, bb: (head0 // nb + p, 0, 0))]
    args = [qkv, qkv, qkv, bias_vecs]
    if mode == "diff":
        in_specs += [pl.BlockSpec((4, HEAD_DIM), lambda p, bb: (0, 0)),
                     pl.BlockSpec((1, LANES), lambda p, bb: (0, 0))]
        args += [lam_params.astype(F32), subln_g.reshape(1, LANES)]
    scratch = [pltpu.VMEM((2 if moba else 1, LANES, s), BF16),
               pltpu.VMEM((2, TQ, s), F32), pltpu.VMEM((2, TQ, s), BF16),
               pltpu.VMEM((s, 2 * LANES), BF16), pltpu.VMEM((nb, TQ, s), F32)]
    if moba:
        scratch += [pltpu.VMEM((2, LANES, s), F32)]
    return pl.pallas_call(
        functools.partial(_attn_kernel, mode=mode, moba=moba, lam_init=lam_init, s=s),
        grid=(n_blocks, b),
        in_specs=in_specs,
        out_specs=pl.BlockSpec((1, s, LANES), lambda p, bb: (bb, 0, p)),
        out_shape=jax.ShapeDtypeStruct((b, s, n_blocks * LANES), BF16),
        scratch_shapes=scratch,
        compiler_params=_params(2, VMEM_LIMIT),
        name="attn_" + mode + ("_moba" if moba else ""),
    )(*args)


def _mla_up_kernel(lat_ref, qg_ref, kvg_ref, wa_ref, wb_ref, wk_ref, wv_ref, cq_ref, sq_ref,
                   ck_ref, sk_ref, q_ref, k_ref, v_ref, *, per_seq):
    lat = lat_ref[...]
    tm = lat.shape[0]
    r0 = pl.multiple_of((pl.program_id(0) % per_seq) * tm, tm)
    cqn = _rms(lat[:, :MLA_Q_RANK], qg_ref[...]).astype(BF16)
    ckvn = _rms(lat[:, MLA_Q_RANK:MLA_Q_RANK + MLA_KV_RANK], kvg_ref[...]).astype(BF16)
    qa = jnp.dot(cqn, wa_ref[...], preferred_element_type=F32)
    qb = jnp.dot(cqn, wb_ref[...], preferred_element_type=F32)
    kn = jnp.dot(ckvn, wk_ref[...], preferred_element_type=F32).astype(BF16)
    v_ref[...] = jnp.dot(ckvn, wv_ref[...], preferred_element_type=F32).astype(BF16)
    x = lat[:, MLA_Q_RANK + MLA_KV_RANK:]
    kr = (x * ck_ref[pl.ds(r0, tm), :]
          + pltpu.roll(x, MLA_ROPE_DIM, 1) * sk_ref[pl.ds(r0, tm), :]).astype(BF16)
    cq = cq_ref[pl.ds(r0, tm), :]
    sq = sq_ref[pl.ds(r0, tm), :]
    hw = 2 * LANES
    for h in range(MLA_HEADS):
        q_ref[:, h * hw:(h + 1) * hw] = (qa[:, h * hw:(h + 1) * hw] * cq
                                         + qb[:, h * hw:(h + 1) * hw] * sq).astype(BF16)
        k_ref[:, h * hw:h * hw + LANES] = kn[:, h * LANES:(h + 1) * LANES]
        k_ref[:, h * hw + LANES:(h + 1) * hw] = kr


def mla_up(latent, q_norm, kv_norm, wa, wb, wk, wv, rope_tabs, s, tm=1024):
    m = latent.shape[0]
    cq, sq, ck, sk = rope_tabs
    per_seq = s // tm
    tab = lambda w: pl.BlockSpec((s, w), lambda i: (0, 0))
    full = lambda a: pl.BlockSpec(a.shape, lambda i: (0, 0))
    hw = 2 * LANES
    rows = lambda w: pl.BlockSpec((tm, w), lambda i: (i, 0))
    return pl.pallas_call(
        functools.partial(_mla_up_kernel, per_seq=per_seq),
        grid=(m // tm,),
        in_specs=[rows(latent.shape[1]),
                  pl.BlockSpec((1, MLA_Q_RANK), lambda i: (0, 0)),
                  pl.BlockSpec((1, MLA_KV_RANK), lambda i: (0, 0)),
                  full(wa), full(wb), full(wk), full(wv), tab(hw), tab(hw), tab(LANES), tab(LANES)],
        out_specs=[rows(MLA_HEADS * hw), rows(MLA_HEADS * hw), rows(MLA_HEADS * LANES)],
        out_shape=[jax.ShapeDtypeStruct((m, MLA_HEADS * hw), BF16),
                   jax.ShapeDtypeStruct((m, MLA_HEADS * hw), BF16),
                   jax.ShapeDtypeStruct((m, MLA_HEADS * LANES), BF16)],
        compiler_params=_params(1, VMEM_LIMIT),
        name="mla_up",
    )(latent, q_norm.reshape(1, -1), kv_norm.reshape(1, -1), wa, wb, wk, wv, cq, sq, ck, sk)


def _mla_attn_kernel(q_ref, k_ref, v_ref, o_ref, kt_ref, sc_ref, p_ref, vone_ref, *, s):
    scale = (MLA_NOPE_DIM + MLA_ROPE_DIM) ** -0.5 * LOG2E
    _store_transposed(kt_ref, 0, k_ref[0])
    vone_ref[:, :LANES] = v_ref[0]
    vone_ref[:, LANES:] = jnp.ones((s, LANES), BF16)
    row = lax.broadcasted_iota(jnp.int32, (TQ, TK), 0)
    col = lax.broadcasted_iota(jnp.int32, (TQ, TK), 1)
    causal = row >= col
    nblk = s // TQ

    order = _block_order(nblk)

    def scores(u):
        i = order[u]
        nk = (i + 1) * TK
        q = q_ref[0, i * TQ:(i + 1) * TQ, :]
        sc = jnp.dot(q, kt_ref[0, :, 0:nk], preferred_element_type=F32) * scale
        if i > 0:
            sc_ref[u % 2, :, 0:nk - TK] = sc[:, :nk - TK]
        sc_ref[u % 2, :, nk - TK:nk] = jnp.where(causal, sc[:, nk - TK:], NEG)

    def softmax(u):
        nk = (order[u] + 1) * TK
        m = jnp.max(sc_ref[u % 2, :, 0:nk], axis=1, keepdims=True)
        p = jnp.exp2(sc_ref[u % 2, :, 0:nk] - m)
        p_ref[u % 2, :, 0:nk] = p.astype(BF16)

    def weighted_values(u):
        i = order[u]
        nk = (i + 1) * TK
        o = jnp.dot(p_ref[u % 2, :, 0:nk], vone_ref[0:nk, :], preferred_element_type=F32)
        o_ref[0, i * TQ:(i + 1) * TQ, :] = (o[:, :LANES] / o[:, LANES:]).astype(o_ref.dtype)

    scores(0)
    for u in range(nblk):
        if u + 1 < nblk:
            scores(u + 1)
        softmax(u)
        if u > 0:
            weighted_values(u - 1)
    weighted_values(nblk - 1)


def mla_attention(q, k, v):
    b, s, _ = q.shape
    hw = 2 * LANES
    return pl.pallas_call(
        functools.partial(_mla_attn_kernel, s=s),
        grid=(MLA_HEADS, b),
        in_specs=[pl.BlockSpec((1, s, hw), lambda h, bb: (bb, 0, h)),
                  pl.BlockSpec((1, s, hw), lambda h, bb: (bb, 0, h)),
                  pl.BlockSpec((1, s, LANES), lambda h, bb: (bb, 0, h))],
        out_specs=pl.BlockSpec((1, s, LANES), lambda h, bb: (bb, 0, h)),
        out_shape=jax.ShapeDtypeStruct((b, s, MLA_HEADS * LANES), BF16),
        scratch_shapes=[pltpu.VMEM((1, hw, s), BF16),
                        pltpu.VMEM((2, TQ, s), F32), pltpu.VMEM((2, TQ, s), BF16),
                        pltpu.VMEM((s, 2 * LANES), BF16)],
        compiler_params=_params(2, VMEM_LIMIT),
        name="mla_attn",
    )(q, k, v)


def _rope_tables(s):
    half = MLA_ROPE_DIM // 2
    freq = ROPE_THETA ** (-jnp.arange(half, dtype=F32) / half)
    ang = jnp.arange(s, dtype=F32)[:, None] * freq[None, :]
    cos, sin = jnp.cos(ang), jnp.sin(ang)
    cos2 = jnp.concatenate([cos, cos], axis=1)
    sin2 = jnp.concatenate([-sin, sin], axis=1)
    z64 = jnp.zeros((s, MLA_ROPE_DIM), F32)
    cq = jnp.concatenate([jnp.ones((s, MLA_NOPE_DIM), F32), cos2, z64], axis=1)
    sq = jnp.concatenate([jnp.zeros((s, MLA_NOPE_DIM), F32), sin2, z64], axis=1)
    ck = jnp.concatenate([cos2, z64], axis=1)
    sk = jnp.concatenate([sin2, z64], axis=1)
    return cq, sq, ck, sk


def _swap_halves(w):
    half = w.shape[-1] // 2
    return jnp.concatenate([w[..., half:], w[..., :half]], axis=-1)


def _mla_q_weights(w_uq):
    r = w_uq.shape[0]
    w = w_uq.reshape(r, MLA_HEADS, MLA_NOPE_DIM + MLA_ROPE_DIM)
    nope, rope = w[..., :MLA_NOPE_DIM], w[..., MLA_NOPE_DIM:]
    z64 = jnp.zeros((r, MLA_HEADS, MLA_ROPE_DIM), w.dtype)
    z128 = jnp.zeros((r, MLA_HEADS, MLA_NOPE_DIM), w.dtype)
    wa = jnp.concatenate([nope, rope, z64], axis=-1).reshape(r, -1)
    wb = jnp.concatenate([z128, _swap_halves(rope), z64], axis=-1).reshape(r, -1)
    return wa.astype(BF16), wb.astype(BF16)


def kernel(x, rel_bias, even_norm1, even_w_in, diff_lambda, diff_subln, even_w_out,
           odd_norm1, odd_w_in, mla_q_norm, mla_w_uq, mla_kv_norm, mla_w_ukv, odd_w_out,
           ffn_norm, ffn_w_in, ffn_conv_w, ffn_conv_b, ffn_w_out, final_norm):
    b, s, d = x.shape
    m = b * s
    bias_vecs = bias_vectors(rel_bias, s)
    rope_tabs = _rope_tables(s)
    qscale = HEAD_DIM ** -0.5 * LOG2E
    ffn_w_in_bf16 = ffn_w_in.astype(BF16)
    h = x.reshape(m, d)
    for layer in range(DEPTH):
        li = layer // 2
        if layer % 2 == 0:
            lam_init = 0.8 - 0.6 * math.exp(-0.3 * layer)
            mw, dw = MOBA_HEADS * HEAD_DIM, DIFF_HEADS * 2 * HEAD_DIM
            colscale = np.ones((3 * mw + 3 * dw,), np.float32)
            colscale[:mw] = qscale
            colscale[3 * mw:3 * mw + dw] = qscale
            w = (even_w_in[li] * jnp.asarray(colscale)).astype(BF16)
            (qkv,) = in_proj(h, even_norm1[li], w, [w.shape[1]], [BF16])
            qkv = qkv.reshape(b, s, -1)
            nb = mw // LANES
            o_a = attention(qkv, 0, nb, 2 * nb, nb, bias_vecs, 0, "pair", moba=True)
            o_b = attention(qkv, 3 * nb, 4 * nb, 5 * nb, DIFF_HEADS, bias_vecs, MOBA_HEADS, "diff",
                            lam_params=diff_lambda[li], subln_g=diff_subln[li], lam_init=lam_init)
            wo = even_w_out
        else:
            w = odd_w_in[li]
            dw = DIL_HEADS * HEAD_DIM
            lat0 = 3 * dw
            colscale = np.ones((w.shape[1],), np.float32)
            colscale[:dw] = qscale
            kr_cols = w[:, lat0 + MLA_Q_RANK + MLA_KV_RANK:]
            w_all = jnp.concatenate([w * jnp.asarray(colscale), _swap_halves(kr_cols)],
                                    axis=1).astype(BF16)
            qkv, latent = in_proj(h, odd_norm1[li], w_all,
                                  [lat0, w_all.shape[1] - lat0], [BF16, F32])
            qkv = qkv.reshape(b, s, -1)
            nb = dw // LANES
            o_a = attention(qkv, 0, nb, 2 * nb, nb, bias_vecs, MOBA_HEADS + DIFF_HEADS, "pair")
            wa, wb = _mla_q_weights(mla_w_uq[li])
            wkv = mla_w_ukv[li].reshape(MLA_KV_RANK, MLA_HEADS, MLA_NOPE_DIM + MLA_V_DIM)
            wk = wkv[..., :MLA_NOPE_DIM].reshape(MLA_KV_RANK, -1).astype(BF16)
            wv = wkv[..., MLA_NOPE_DIM:].reshape(MLA_KV_RANK, -1).astype(BF16)
            q_m, k_m, v_m = mla_up(latent, mla_q_norm[li], mla_kv_norm[li], wa, wb, wk, wv,
                                   rope_tabs, s)
            o_b = mla_attention(q_m.reshape(b, s, -1), k_m.reshape(b, s, -1),
                                v_m.reshape(b, s, -1))
            wo = odd_w_out
        h = proj_residual([o_a.reshape(m, -1), o_b.reshape(m, -1)], wo, li, h)
        act = ffn_in(h.reshape(b, s, d), ffn_norm[layer], ffn_w_in_bf16, layer,
                     ffn_conv_w[layer], ffn_conv_b[layer])
        h = proj_residual([act.reshape(m, D_FF)], ffn_w_out, layer, h,
                          final_g=final_norm if layer == DEPTH - 1 else None)
    return h.reshape(b, s, d)
```

```python
import functools
import math

import numpy as np
import jax
import jax.numpy as jnp
from jax import lax
from jax.experimental import pallas as pl
from jax.experimental.pallas import tpu as pltpu

D_MODEL = 1024
DEPTH = 4
HEAD_DIM = 64
MOBA_HEADS = 8
MOBA_BLOCK = 256
MOBA_TOPK = 3
DIFF_HEADS = 4
DIL_HEADS = 8
DIL_CONFIGS = ((128, 1), (512, 4), (2048, 16))
MLA_HEADS = 4
MLA_Q_RANK = 256
MLA_KV_RANK = 128
MLA_NOPE_DIM = 128
MLA_ROPE_DIM = 64
MLA_V_DIM = 128
ROPE_THETA = 10000.0
REL_BUCKETS = 32
REL_MAX_DIST = 1024
N_BIAS_HEADS = MOBA_HEADS + DIFF_HEADS + DIL_HEADS
D_FF = 2816
EPS = 1e-6
NEG = -1e30
LOG2E = math.log2(math.e)

LANES = 128
TQ = 256
TK = 256
FF_STEP_ROWS = 1024
FF_CHUNK = 256
FF_ROWS = 512
VMEM_LIMIT = 56 * 1024 * 1024

F32 = jnp.float32
BF16 = jnp.bfloat16


def _params(n_axes, vmem=None):
    return pltpu.CompilerParams(dimension_semantics=("arbitrary",) * n_axes,
                                vmem_limit_bytes=vmem)


def _rms(x, g):
    ms = jnp.mean(x * x, axis=-1, keepdims=True)
    return x * lax.rsqrt(ms + EPS) * g


def _in_proj_kernel(x_ref, g_ref, w_ref, *o_refs, row_chunk):
    tm = x_ref.shape[0]
    g = g_ref[...]
    for r in range(tm // row_chunk):
        rows = slice(r * row_chunk, (r + 1) * row_chunk)
        xn = _rms(x_ref[rows, :], g).astype(BF16)
        acc = jnp.dot(xn, w_ref[...], preferred_element_type=F32)
        col = 0
        for o_ref in o_refs:
            width = o_ref.shape[1]
            o_ref[rows, :] = acc[:, col:col + width].astype(o_ref.dtype)
            col += width


def in_proj(x, g, w, out_widths, out_dtypes, tm=1024, row_chunk=512):
    m, d = x.shape
    return pl.pallas_call(
        functools.partial(_in_proj_kernel, row_chunk=row_chunk),
        grid=(m // tm,),
        in_specs=[pl.BlockSpec((tm, d), lambda i: (i, 0)),
                  pl.BlockSpec((1, d), lambda i: (0, 0)),
                  pl.BlockSpec(w.shape, lambda i: (0, 0))],
        out_specs=[pl.BlockSpec((tm, n), lambda i: (i, 0)) for n in out_widths],
        out_shape=[jax.ShapeDtypeStruct((m, n), dt) for n, dt in zip(out_widths, out_dtypes)],
        compiler_params=_params(1, VMEM_LIMIT),
        name="in_proj",
    )(x, g.reshape(1, d), w)


def _proj_res_kernel(*refs, n_in, final_norm):
    a_refs = refs[:n_in]
    w_ref, res_ref = refs[n_in], refs[n_in + 1]
    o_ref, wb_ref = refs[-2], refs[-1]

    @pl.when(pl.program_id(0) == 0)
    def _():
        wb_ref[...] = w_ref[...].astype(BF16)

    acc = res_ref[...]
    row = 0
    for a_ref in a_refs:
        k = a_ref.shape[1]
        acc = acc + jnp.dot(a_ref[...], wb_ref[row:row + k, :], preferred_element_type=F32)
        row += k
    if final_norm:
        acc = _rms(acc, refs[n_in + 2][...])
    o_ref[...] = acc


def proj_residual(acts, w_stack, layer, res, final_g=None, tm=512):
    m, d = res.shape
    n_in = len(acts)
    in_specs = [pl.BlockSpec((tm, a.shape[1]), lambda i: (i, 0)) for a in acts]
    in_specs += [pl.BlockSpec((None,) + w_stack.shape[1:], lambda i: (layer, 0, 0)),
                 pl.BlockSpec((tm, d), lambda i: (i, 0))]
    args = list(acts) + [w_stack, res]
    if final_g is not None:
        in_specs += [pl.BlockSpec((1, d), lambda i: (0, 0))]
        args += [final_g.reshape(1, d)]
    return pl.pallas_call(
        functools.partial(_proj_res_kernel, n_in=n_in, final_norm=final_g is not None),
        grid=(m // tm,),
        in_specs=in_specs,
        out_specs=pl.BlockSpec((tm, d), lambda i: (i, 0)),
        out_shape=jax.ShapeDtypeStruct((m, d), F32),
        scratch_shapes=[pltpu.VMEM(w_stack.shape[1:], BF16)],
        compiler_params=_params(1, VMEM_LIMIT),
        name="proj_residual",
    )(*args)


def _ffn_in_kernel(oa_ref, ob_ref, wo_ref, res_ref, g_ref, w_ref, cw_ref, cb_ref,
                   act_ref, h_ref, xn_ref, h1_ref, h2_ref):
    rows = res_ref.shape[1]
    ka = oa_ref.shape[2]
    first = pl.program_id(1) == 0
    for r in range(rows // FF_ROWS):
        rs = slice(r * FF_ROWS, (r + 1) * FF_ROWS)
        hh = (res_ref[0, rs, :]
              + jnp.dot(oa_ref[0, rs, :], wo_ref[:ka, :], preferred_element_type=F32)
              + jnp.dot(ob_ref[0, rs, :], wo_ref[ka:, :], preferred_element_type=F32))
        h_ref[0, rs, :] = hh
        xn_ref[rs, :] = _rms(hh, g_ref[...]).astype(BF16)
    sub = lax.broadcasted_iota(jnp.int32, (8, FF_CHUNK), 0)
    zeros = jnp.zeros((8, FF_CHUNK), F32)
    for c in range(D_FF // FF_CHUNK):
        cols = slice(c * FF_CHUNK, (c + 1) * FF_CHUNK)
        wu = w_ref[:, cols]
        wg = w_ref[:, D_FF + c * FF_CHUNK:D_FF + (c + 1) * FF_CHUNK]
        cw = cw_ref[:, cols]
        cb = cb_ref[:, cols]
        prev1 = jnp.where(first, zeros, h1_ref[c])
        prev2 = jnp.where(first, zeros, h2_ref[c])
        for r in range(rows // FF_ROWS):
            rs = slice(r * FF_ROWS, (r + 1) * FF_ROWS)
            xn = xn_ref[rs, :]
            u = jnp.dot(xn, wu, preferred_element_type=F32)
            gt = jnp.dot(xn, wg, preferred_element_type=F32)
            r1 = pltpu.roll(gt, 1, 0)
            r2 = pltpu.roll(gt, 2, 0)
            g1 = jnp.concatenate([jnp.where(sub >= 1, r1[:8], prev1), r1[8:]], axis=0)
            g2 = jnp.concatenate([jnp.where(sub >= 2, r2[:8], prev2), r2[8:]], axis=0)
            prev1, prev2 = r1[:8], r2[:8]
            z = cw[2:3] * gt + cw[1:2] * g1 + cw[0:1] * g2 + cb
            gelu = 0.5 * z * (1.0 + lax.erf(z * math.sqrt(0.5)))
            act_ref[0, rs, cols] = (gelu * u).astype(BF16)
        h1_ref[c] = prev1
        h2_ref[c] = prev2


def attn_out_ffn_in(o_a, o_b, wo_stack, wo_layer, h3, g, w_stack, layer, conv_w, conv_b):
    b, s, d = h3.shape
    nc = D_FF // FF_CHUNK
    once = dict(pipeline_mode=pl.Buffered(1))
    rows3 = lambda w: pl.BlockSpec((1, FF_STEP_ROWS, w), lambda i, j: (i, j, 0))
    return pl.pallas_call(
        _ffn_in_kernel,
        grid=(b, s // FF_STEP_ROWS),
        in_specs=[rows3(o_a.shape[2]), rows3(o_b.shape[2]),
                  pl.BlockSpec((None,) + wo_stack.shape[1:], lambda i, j: (wo_layer, 0, 0), **once),
                  rows3(d),
                  pl.BlockSpec((1, d), lambda i, j: (0, 0)),
                  pl.BlockSpec((None, d, 2 * D_FF), lambda i, j: (layer, 0, 0), **once),
                  pl.BlockSpec((3, D_FF), lambda i, j: (0, 0)),
                  pl.BlockSpec((1, D_FF), lambda i, j: (0, 0))],
        out_specs=[rows3(D_FF), rows3(d)],
        out_shape=[jax.ShapeDtypeStruct((b, s, D_FF), BF16),
                   jax.ShapeDtypeStruct((b, s, d), F32)],
        scratch_shapes=[pltpu.VMEM((FF_STEP_ROWS, d), BF16),
                        pltpu.VMEM((nc, 8, FF_CHUNK), F32), pltpu.VMEM((nc, 8, FF_CHUNK), F32)],
        compiler_params=_params(2, VMEM_LIMIT),
        name="attn_out_ffn_in",
    )(o_a, o_b, wo_stack, h3, g.reshape(1, d), w_stack, conv_w, conv_b.reshape(1, D_FF))


def _bucket_of_distance(s):
    max_exact = REL_BUCKETS // 2
    n_large = REL_BUCKETS - max_exact
    thresholds = []
    for k in range(1, n_large):
        t = max_exact * (REL_MAX_DIST / max_exact) ** (k / n_large)
        ti = int(round(t))
        thresholds.append(ti if abs(t - ti) < 1e-9 else int(math.ceil(t)))
    d = np.arange(s)
    large = max_exact + sum((d >= t).astype(np.int64) for t in thresholds)
    return np.where(d < max_exact, d, np.minimum(large, REL_BUCKETS - 1)).astype(np.int32)


def _dilated_log_multiplicity(s):
    d = np.arange(s)
    count = np.zeros(s, np.int64)
    for window, dil in DIL_CONFIGS:
        count += ((d % dil == 0) & (d // dil <= window // dil)).astype(np.int64)
    return np.where(count > 0, np.log(np.maximum(count, 1)), NEG).astype(np.float32)


def _fill_bias_strip(strip_ref, vec_ref, n, s):
    for h in range(n):
        tile = jnp.broadcast_to(vec_ref[h], (TQ, s + TQ))
        tile = pltpu.roll(tile, 0, 1, stride=1, stride_axis=0)
        strip_ref[h] = tile[:, TQ:]


def bias_vectors(rel_bias, s):
    bucket = _bucket_of_distance(s)
    per_dist = jnp.take(rel_bias.T.astype(F32), jnp.asarray(bucket), axis=1)
    logmult = jnp.asarray(_dilated_log_multiplicity(s))
    dil0 = MOBA_HEADS + DIFF_HEADS
    covered = logmult > 0.5 * NEG
    dil_rows = jnp.where(covered[None, :], per_dist[dil0:] + logmult[None, :], NEG)
    per_dist = jnp.concatenate([per_dist[:dil0], dil_rows], axis=0) * LOG2E
    nh = per_dist.shape[0]
    w = jnp.concatenate([jnp.full((nh, 1), NEG, F32), per_dist[:, ::-1],
                         jnp.full((nh, TQ - 1), NEG, F32)], axis=1)
    return w.reshape(nh, 1, s + TQ)


def _store_transposed(kt_ref, idx, k):
    s = k.shape[0]
    for j in range(s // TK):
        kt_ref[idx, :, j * TK:(j + 1) * TK] = k[j * TK:(j + 1) * TK, :].T


def _block_order(n):
    return list(range(1, n, 2)) + list(range((n - 1) // 2 * 2, -1, -2))


def _attn_kernel(*refs, mode, moba, lam_init, s):
    if mode == "diff":
        (q_ref, k_ref, v_ref, bvec_ref, lam_ref, sg_ref, o_ref,
         kt_ref, sc_ref, p_ref, vone_ref, b_ref) = refs
    elif moba:
        (q_ref, k_ref, v_ref, bvec_ref, o_ref,
         kt_ref, sc_ref, p_ref, vone_ref, b_ref, drop_ref) = refs
    else:
        q_ref, k_ref, v_ref, bvec_ref, o_ref, kt_ref, sc_ref, p_ref, vone_ref, b_ref = refs

    @pl.when(pl.program_id(1) == 0)
    def _():
        _fill_bias_strip(b_ref, bvec_ref, 2 if mode == "pair" else 1, s)

    vone_ref[:, :LANES] = v_ref[0]
    vone_ref[:, LANES:] = jnp.ones((s, LANES), BF16)
    nblk = s // TK
    lane = lax.broadcasted_iota(jnp.int32, (TQ, LANES), 1)
    halves = (lane < HEAD_DIM, lane >= HEAD_DIM)
    nt = (((1,), (1,)), ((), ()))

    if moba:
        k_all = k_ref[0]
        q_all = q_ref[0]
        rowblk = lax.broadcasted_iota(jnp.int32, (s, LANES), 0) // TK
        lane_s = lax.broadcasted_iota(jnp.int32, (s, LANES), 1)
        kmean = jnp.mean(k_all.astype(F32).reshape(nblk, TK, LANES), axis=1)
        k_hi = kmean.astype(BF16)
        rem = kmean - k_hi.astype(F32)
        k_mid = rem.astype(BF16)
        k_lo = (rem - k_mid.astype(F32)).astype(BF16)
        kmean3 = jnp.concatenate([k_hi, k_mid, k_lo], axis=1)
        qblk = lax.broadcasted_iota(jnp.int32, (nblk, s), 1) // TQ
        blk = lax.broadcasted_iota(jnp.int32, (nblk, s), 0)
        for c in range(2):
            base = HEAD_DIM if c == 0 else 0
            in_half = (lane_s < HEAD_DIM) if c == 0 else (lane_s >= HEAD_DIM)
            onehot = jnp.where(lane_s - base == rowblk, 1.0, 0.0).astype(BF16)
            _store_transposed(kt_ref, c, jnp.where(in_half, k_all, onehot))
            qh = jnp.where(in_half, q_all, jnp.zeros_like(q_all))
            gate = lax.dot_general(kmean3, jnp.concatenate([qh, qh, qh], axis=1), nt,
                                   preferred_element_type=F32)
            rank = jnp.zeros((nblk, s), jnp.int32)
            for n in range(nblk - 1):
                gn = gate[n:n + 1, :]
                beats = jnp.where(gn > gate, 1, jnp.where(gn == gate, jnp.where(n < blk, 1, 0), 0))
                rank = rank + jnp.where(n < qblk, beats, 0)
            drop = jnp.where(blk < qblk, jnp.where(rank >= MOBA_TOPK, NEG, 0.0), 0.0)
            drop_ref[c] = jnp.zeros((LANES, s), F32)
            drop_ref[c, base:base + nblk, :] = drop

    else:
        _store_transposed(kt_ref, 0, k_ref[0])

    if mode == "diff":
        lp = lam_ref[...]
        lam = (jnp.exp(jnp.sum(lp[0:1] * lp[1:2], axis=1, keepdims=True))
               - jnp.exp(jnp.sum(lp[2:3] * lp[3:4], axis=1, keepdims=True)) + lam_init)

    units = [(i, c) for i in _block_order(nblk) for c in range(2)]

    def scores(u):
        i, c = units[u]
        nk = (i + 1) * TK
        q = q_ref[0, i * TQ:(i + 1) * TQ, :]
        qc = jnp.where(halves[c], q, jnp.zeros_like(q))
        if moba and i > 0:
            sel = drop_ref[c, :, i * TQ:(i + 1) * TQ].T
            qc = jnp.where(halves[c], q, sel.astype(BF16))
        sc_ref[u % 2, :, 0:nk] = (jnp.dot(qc, kt_ref[c if moba else 0, :, 0:nk],
                                          preferred_element_type=F32)
                                  + b_ref[c if mode == "pair" else 0, :, s - nk:])

    def softmax(u):
        nk = (units[u][0] + 1) * TK
        m = jnp.max(sc_ref[u % 2, :, 0:nk], axis=1, keepdims=True)
        p = jnp.exp2(sc_ref[u % 2, :, 0:nk] - m)
        p_ref[u % 2, :, 0:nk] = p.astype(BF16)

    outs = []

    def weighted_values(u):
        i, c = units[u]
        nk = (i + 1) * TK
        o = jnp.dot(p_ref[u % 2, :, 0:nk], vone_ref[0:nk, :], preferred_element_type=F32)
        outs.append(o[:, :LANES] / o[:, LANES:])
        if c == 1:
            if mode == "pair":
                res = jnp.where(halves[0], outs[0], outs[1])
            else:
                res = _rms(outs[0] - lam * outs[1], sg_ref[...]) * (1.0 - lam_init)
            o_ref[0, i * TQ:(i + 1) * TQ, :] = res.astype(o_ref.dtype)
            outs.clear()

    scores(0)
    for u in range(len(units)):
        if u + 1 < len(units):
            scores(u + 1)
        softmax(u)
        if u > 0:
            weighted_values(u - 1)
    weighted_values(len(units) - 1)


def attention(qkv, col_q, col_k, col_v, n_blocks, bias_vecs, head0, mode, moba=False,
              lam_params=None, subln_g=None, lam_init=0.0):
    b, s, _ = qkv.shape
    nb = 2 if mode == "pair" else 1
    seq = lambda col: pl.BlockSpec((1, s, LANES), lambda p, bb: (bb, 0, col + p))
    in_specs = [seq(col_q), seq(col_k), seq(col_v),
                pl.BlockSpec((nb, 1, s + TQ), lambda p, bb: (head0 // nb + p, 0, 0))]
    args = [qkv, qkv, qkv, bias_vecs]
    if mode == "diff":
        in_specs += [pl.BlockSpec((4, HEAD_DIM), lambda p, bb: (0, 0)),
                     pl.BlockSpec((1, LANES), lambda p, bb: (0, 0))]
        args += [lam_params.astype(F32), subln_g.reshape(1, LANES)]
    scratch = [pltpu.VMEM((2 if moba else 1, LANES, s), BF16),
               pltpu.VMEM((2, TQ, s), F32), pltpu.VMEM((2, TQ, s), BF16),
               pltpu.VMEM((s, 2 * LANES), BF16), pltpu.VMEM((nb, TQ, s), F32)]
    if moba:
        scratch += [pltpu.VMEM((2, LANES, s), F32)]
    return pl.pallas_call(
        functools.partial(_attn_kernel, mode=mode, moba=moba, lam_init=lam_init, s=s),
        grid=(n_blocks, b),
        in_specs=in_specs,
        out_specs=pl.BlockSpec((1, s, LANES), lambda p, bb: (bb, 0, p)),
        out_shape=jax.ShapeDtypeStruct((b, s, n_blocks * LANES), BF16),
        scratch_shapes=scratch,
        compiler_params=_params(2, VMEM_LIMIT),
        name="attn_" + mode + ("_moba" if moba else ""),
    )(*args)


def _mla_up_kernel(lat_ref, qg_ref, kvg_ref, wa_ref, wb_ref, wk_ref, wv_ref, cq_ref, sq_ref,
                   ck_ref, sk_ref, q_ref, k_ref, v_ref, *, per_seq):
    lat = lat_ref[...]
    tm = lat.shape[0]
    r0 = pl.multiple_of((pl.program_id(0) % per_seq) * tm, tm)
    cqn = _rms(lat[:, :MLA_Q_RANK], qg_ref[...]).astype(BF16)
    ckvn = _rms(lat[:, MLA_Q_RANK:MLA_Q_RANK + MLA_KV_RANK], kvg_ref[...]).astype(BF16)
    qa = jnp.dot(cqn, wa_ref[...], preferred_element_type=F32)
    qb = jnp.dot(cqn, wb_ref[...], preferred_element_type=F32)
    kn = jnp.dot(ckvn, wk_ref[...], preferred_element_type=F32).astype(BF16)
    v_ref[...] = jnp.dot(ckvn, wv_ref[...], preferred_element_type=F32).astype(BF16)
    x = lat[:, MLA_Q_RANK + MLA_KV_RANK:]
    kr = (x * ck_ref[pl.ds(r0, tm), :]
          + pltpu.roll(x, MLA_ROPE_DIM, 1) * sk_ref[pl.ds(r0, tm), :]).astype(BF16)
    cq = cq_ref[pl.ds(r0, tm), :]
    sq = sq_ref[pl.ds(r0, tm), :]
    hw = 2 * LANES
    for h in range(MLA_HEADS):
        q_ref[:, h * hw:(h + 1) * hw] = (qa[:, h * hw:(h + 1) * hw] * cq
                                         + qb[:, h * hw:(h + 1) * hw] * sq).astype(BF16)
        k_ref[:, h * hw:h * hw + LANES] = kn[:, h * LANES:(h + 1) * LANES]
        k_ref[:, h * hw + LANES:(h + 1) * hw] = kr


def mla_up(latent, q_norm, kv_norm, wa, wb, wk, wv, rope_tabs, s, tm=1024):
    m = latent.shape[0]
    cq, sq, ck, sk = rope_tabs
    per_seq = s // tm
    tab = lambda w: pl.BlockSpec((s, w), lambda i: (0, 0))
    full = lambda a: pl.BlockSpec(a.shape, lambda i: (0, 0))
    hw = 2 * LANES
    rows = lambda w: pl.BlockSpec((tm, w), lambda i: (i, 0))
    return pl.pallas_call(
        functools.partial(_mla_up_kernel, per_seq=per_seq),
        grid=(m // tm,),
        in_specs=[rows(latent.shape[1]),
                  pl.BlockSpec((1, MLA_Q_RANK), lambda i: (0, 0)),
                  pl.BlockSpec((1, MLA_KV_RANK), lambda i: (0, 0)),
                  full(wa), full(wb), full(wk), full(wv), tab(hw), tab(hw), tab(LANES), tab(LANES)],
        out_specs=[rows(MLA_HEADS * hw), rows(MLA_HEADS * hw), rows(MLA_HEADS * LANES)],
        out_shape=[jax.ShapeDtypeStruct((m, MLA_HEADS * hw), BF16),
                   jax.ShapeDtypeStruct((m, MLA_HEADS * hw), BF16),
                   jax.ShapeDtypeStruct((m, MLA_HEADS * LANES), BF16)],
        compiler_params=_params(1, VMEM_LIMIT),
        name="mla_up",
    )(latent, q_norm.reshape(1, -1), kv_norm.reshape(1, -1), wa, wb, wk, wv, cq, sq, ck, sk)


def _mla_attn_kernel(q_ref, k_ref, v_ref, o_ref, kt_ref, sc_ref, p_ref, vone_ref, *, s):
    scale = (MLA_NOPE_DIM + MLA_ROPE_DIM) ** -0.5 * LOG2E
    _store_transposed(kt_ref, 0, k_ref[0])
    vone_ref[:, :LANES] = v_ref[0]
    vone_ref[:, LANES:] = jnp.ones((s, LANES), BF16)
    row = lax.broadcasted_iota(jnp.int32, (TQ, TK), 0)
    col = lax.broadcasted_iota(jnp.int32, (TQ, TK), 1)
    causal = row >= col
    nblk = s // TQ

    order = _block_order(nblk)

    def scores(u):
        i = order[u]
        nk = (i + 1) * TK
        q = q_ref[0, i * TQ:(i + 1) * TQ, :]
        sc = jnp.dot(q, kt_ref[0, :, 0:nk], preferred_element_type=F32) * scale
        if i > 0:
            sc_ref[u % 2, :, 0:nk - TK] = sc[:, :nk - TK]
        sc_ref[u % 2, :, nk - TK:nk] = jnp.where(causal, sc[:, nk - TK:], NEG)

    def softmax(u):
        nk = (order[u] + 1) * TK
        m = jnp.max(sc_ref[u % 2, :, 0:nk], axis=1, keepdims=True)
        p = jnp.exp2(sc_ref[u % 2, :, 0:nk] - m)
        p_ref[u % 2, :, 0:nk] = p.astype(BF16)

    def weighted_values(u):
        i = order[u]
        nk = (i + 1) * TK
        o = jnp.dot(p_ref[u % 2, :, 0:nk], vone_ref[0:nk, :], preferred_element_type=F32)
        o_ref[0, i * TQ:(i + 1) * TQ, :] = (o[:, :LANES] / o[:, LANES:]).astype(o_ref.dtype)

    scores(0)
    for u in range(nblk):
        if u + 1 < nblk:
            scores(u + 1)
        softmax(u)
        if u > 0:
            weighted_values(u - 1)
    weighted_values(nblk - 1)


def mla_attention(q, k, v):
    b, s, _ = q.shape
    hw = 2 * LANES
    return pl.pallas_call(
        functools.partial(_mla_attn_kernel, s=s),
        grid=(MLA_HEADS, b),
        in_specs=[pl.BlockSpec((1, s, hw), lambda h, bb: (bb, 0, h)),
                  pl.BlockSpec((1, s, hw), lambda h, bb: (bb, 0, h)),
                  pl.BlockSpec((1, s, LANES), lambda h, bb: (bb, 0, h))],
        out_specs=pl.BlockSpec((1, s, LANES), lambda h, bb: (bb, 0, h)),
        out_shape=jax.ShapeDtypeStruct((b, s, MLA_HEADS * LANES), BF16),
        scratch_shapes=[pltpu.VMEM((1, hw, s), BF16),
                        pltpu.VMEM((2, TQ, s), F32), pltpu.VMEM((2, TQ, s), BF16),
                        pltpu.VMEM((s, 2 * LANES), BF16)],
        compiler_params=_params(2, VMEM_LIMIT),
        name="mla_attn",
    )(q, k, v)


def _rope_tables(s):
    half = MLA_ROPE_DIM // 2
    freq = ROPE_THETA ** (-jnp.arange(half, dtype=F32) / half)
    ang = jnp.arange(s, dtype=F32)[:, None] * freq[None, :]
    cos, sin = jnp.cos(ang), jnp.sin(ang)
    cos2 = jnp.concatenate([cos, cos], axis=1)
    sin2 = jnp.concatenate([-sin, sin], axis=1)
    z64 = jnp.zeros((s, MLA_ROPE_DIM), F32)
    cq = jnp.concatenate([jnp.ones((s, MLA_NOPE_DIM), F32), cos2, z64], axis=1)
    sq = jnp.concatenate([jnp.zeros((s, MLA_NOPE_DIM), F32), sin2, z64], axis=1)
    ck = jnp.concatenate([cos2, z64], axis=1)
    sk = jnp.concatenate([sin2, z64], axis=1)
    return cq, sq, ck, sk


def _swap_halves(w):
    half = w.shape[-1] // 2
    return jnp.concatenate([w[..., half:], w[..., :half]], axis=-1)


def _mla_q_weights(w_uq):
    r = w_uq.shape[0]
    w = w_uq.reshape(r, MLA_HEADS, MLA_NOPE_DIM + MLA_ROPE_DIM)
    nope, rope = w[..., :MLA_NOPE_DIM], w[..., MLA_NOPE_DIM:]
    z64 = jnp.zeros((r, MLA_HEADS, MLA_ROPE_DIM), w.dtype)
    z128 = jnp.zeros((r, MLA_HEADS, MLA_NOPE_DIM), w.dtype)
    wa = jnp.concatenate([nope, rope, z64], axis=-1).reshape(r, -1)
    wb = jnp.concatenate([z128, _swap_halves(rope), z64], axis=-1).reshape(r, -1)
    return wa.astype(BF16), wb.astype(BF16)


def kernel(x, rel_bias, even_norm1, even_w_in, diff_lambda, diff_subln, even_w_out,
           odd_norm1, odd_w_in, mla_q_norm, mla_w_uq, mla_kv_norm, mla_w_ukv, odd_w_out,
           ffn_norm, ffn_w_in, ffn_conv_w, ffn_conv_b, ffn_w_out, final_norm):
    b, s, d = x.shape
    m = b * s
    bias_vecs = bias_vectors(rel_bias, s)
    rope_tabs = _rope_tables(s)
    qscale = HEAD_DIM ** -0.5 * LOG2E
    ffn_w_in_bf16 = ffn_w_in.astype(BF16)
    even_w_out_bf16 = even_w_out.astype(BF16)
    odd_w_out_bf16 = odd_w_out.astype(BF16)
    h = x.reshape(m, d)
    for layer in range(DEPTH):
        li = layer // 2
        if layer % 2 == 0:
            lam_init = 0.8 - 0.6 * math.exp(-0.3 * layer)
            mw, dw = MOBA_HEADS * HEAD_DIM, DIFF_HEADS * 2 * HEAD_DIM
            colscale = np.ones((3 * mw + 3 * dw,), np.float32)
            colscale[:mw] = qscale
            colscale[3 * mw:3 * mw + dw] = qscale
            w = (even_w_in[li] * jnp.asarray(colscale)).astype(BF16)
            (qkv,) = in_proj(h, even_norm1[li], w, [w.shape[1]], [BF16])
            qkv = qkv.reshape(b, s, -1)
            nb = mw // LANES
            o_a = attention(qkv, 0, nb, 2 * nb, nb, bias_vecs, 0, "pair", moba=True)
            o_b = attention(qkv, 3 * nb, 4 * nb, 5 * nb, DIFF_HEADS, bias_vecs, MOBA_HEADS, "diff",
                            lam_params=diff_lambda[li], subln_g=diff_subln[li], lam_init=lam_init)
            wo = even_w_out_bf16
        else:
            w = odd_w_in[li]
            dw = DIL_HEADS * HEAD_DIM
            lat0 = 3 * dw
            colscale = np.ones((w.shape[1],), np.float32)
            colscale[:dw] = qscale
            kr_cols = w[:, lat0 + MLA_Q_RANK + MLA_KV_RANK:]
            w_all = jnp.concatenate([w * jnp.asarray(colscale), _swap_halves(kr_cols)],
                                    axis=1).astype(BF16)
            qkv, latent = in_proj(h, odd_norm1[li], w_all,
                                  [lat0, w_all.shape[1] - lat0], [BF16, F32])
            qkv = qkv.reshape(b, s, -1)
            nb = dw // LANES
            o_a = attention(qkv, 0, nb, 2 * nb, nb, bias_vecs, MOBA_HEADS + DIFF_HEADS, "pair")
            wa, wb = _mla_q_weights(mla_w_uq[li])
            wkv = mla_w_ukv[li].reshape(MLA_KV_RANK, MLA_HEADS, MLA_NOPE_DIM + MLA_V_DIM)
            wk = wkv[..., :MLA_NOPE_DIM].reshape(MLA_KV_RANK, -1).astype(BF16)
            wv = wkv[..., MLA_NOPE_DIM:].reshape(MLA_KV_RANK, -1).astype(BF16)
            q_m, k_m, v_m = mla_up(latent, mla_q_norm[li], mla_kv_norm[li], wa, wb, wk, wv,
                                   rope_tabs, s)
            o_b = mla_attention(q_m.reshape(b, s, -1), k_m.reshape(b, s, -1),
                                v_m.reshape(b, s, -1))
            wo = odd_w_out_bf16
        act, h3 = attn_out_ffn_in(o_a, o_b, wo, li, h.reshape(b, s, d), ffn_norm[layer],
                                  ffn_w_in_bf16, layer, ffn_conv_w[layer], ffn_conv_b[layer])
        h = proj_residual([act.reshape(m, D_FF)], ffn_w_out, layer, h3.reshape(m, d),
                          final_g=final_norm if layer == DEPTH - 1 else None)
    return h.reshape(b, s, d)
```

```python
import functools
import math

import numpy as np
import jax
import jax.numpy as jnp
from jax import lax
from jax.experimental import pallas as pl
from jax.experimental.pallas import tpu as pltpu

D_MODEL = 1024
DEPTH = 4
HEAD_DIM = 64
MOBA_HEADS = 8
MOBA_BLOCK = 256
MOBA_TOPK = 3
DIFF_HEADS = 4
DIL_HEADS = 8
DIL_CONFIGS = ((128, 1), (512, 4), (2048, 16))
MLA_HEADS = 4
MLA_Q_RANK = 256
MLA_KV_RANK = 128
MLA_NOPE_DIM = 128
MLA_ROPE_DIM = 64
MLA_V_DIM = 128
ROPE_THETA = 10000.0
REL_BUCKETS = 32
REL_MAX_DIST = 1024
N_BIAS_HEADS = MOBA_HEADS + DIFF_HEADS + DIL_HEADS
D_FF = 2816
EPS = 1e-6
NEG = -1e30
LOG2E = math.log2(math.e)

LANES = 128
TQ = 256
TK = 256
ATTN_SEQS = 2
FF_STEP_ROWS = 1024
FF_CHUNK = 256
FF_ROWS = 512
VMEM_LIMIT = 56 * 1024 * 1024

F32 = jnp.float32
BF16 = jnp.bfloat16


def _params(n_axes, vmem=None):
    return pltpu.CompilerParams(dimension_semantics=("arbitrary",) * n_axes,
                                vmem_limit_bytes=vmem)


def _rms(x, g):
    ms = jnp.mean(x * x, axis=-1, keepdims=True)
    return x * lax.rsqrt(ms + EPS) * g


def _in_proj_kernel(x_ref, g_ref, w_ref, *o_refs, row_chunk):
    tm = x_ref.shape[0]
    g = g_ref[...]
    for r in range(tm // row_chunk):
        rows = slice(r * row_chunk, (r + 1) * row_chunk)
        xn = _rms(x_ref[rows, :], g).astype(BF16)
        acc = jnp.dot(xn, w_ref[...], preferred_element_type=F32)
        col = 0
        for o_ref in o_refs:
            width = o_ref.shape[1]
            o_ref[rows, :] = acc[:, col:col + width].astype(o_ref.dtype)
            col += width


def in_proj(x, g, w, out_widths, out_dtypes, tm=1024, row_chunk=512):
    m, d = x.shape
    return pl.pallas_call(
        functools.partial(_in_proj_kernel, row_chunk=row_chunk),
        grid=(m // tm,),
        in_specs=[pl.BlockSpec((tm, d), lambda i: (i, 0)),
                  pl.BlockSpec((1, d), lambda i: (0, 0)),
                  pl.BlockSpec(w.shape, lambda i: (0, 0))],
        out_specs=[pl.BlockSpec((tm, n), lambda i: (i, 0)) for n in out_widths],
        out_shape=[jax.ShapeDtypeStruct((m, n), dt) for n, dt in zip(out_widths, out_dtypes)],
        compiler_params=_params(1, VMEM_LIMIT),
        name="in_proj",
    )(x, g.reshape(1, d), w)


def _proj_res_kernel(*refs, n_in, final_norm):
    a_refs = refs[:n_in]
    w_ref, res_ref = refs[n_in], refs[n_in + 1]
    o_ref, wb_ref = refs[-2], refs[-1]

    @pl.when(pl.program_id(0) == 0)
    def _():
        wb_ref[...] = w_ref[...].astype(BF16)

    acc = res_ref[...]
    row = 0
    for a_ref in a_refs:
        k = a_ref.shape[1]
        acc = acc + jnp.dot(a_ref[...], wb_ref[row:row + k, :], preferred_element_type=F32)
        row += k
    if final_norm:
        acc = _rms(acc, refs[n_in + 2][...])
    o_ref[...] = acc


def proj_residual(acts, w_stack, layer, res, final_g=None, tm=512):
    m, d = res.shape
    n_in = len(acts)
    in_specs = [pl.BlockSpec((tm, a.shape[1]), lambda i: (i, 0)) for a in acts]
    in_specs += [pl.BlockSpec((None,) + w_stack.shape[1:], lambda i: (layer, 0, 0)),
                 pl.BlockSpec((tm, d), lambda i: (i, 0))]
    args = list(acts) + [w_stack, res]
    if final_g is not None:
        in_specs += [pl.BlockSpec((1, d), lambda i: (0, 0))]
        args += [final_g.reshape(1, d)]
    return pl.pallas_call(
        functools.partial(_proj_res_kernel, n_in=n_in, final_norm=final_g is not None),
        grid=(m // tm,),
        in_specs=in_specs,
        out_specs=pl.BlockSpec((tm, d), lambda i: (i, 0)),
        out_shape=jax.ShapeDtypeStruct((m, d), F32),
        scratch_shapes=[pltpu.VMEM(w_stack.shape[1:], BF16)],
        compiler_params=_params(1, VMEM_LIMIT),
        name="proj_residual",
    )(*args)


def _ffn_in_kernel(oa_ref, ob_ref, wo_ref, res_ref, g_ref, w_ref, cw_ref, cb_ref,
                   act_ref, h_ref, xn_ref, h1_ref, h2_ref):
    rows = res_ref.shape[1]
    ka = oa_ref.shape[2]
    first = pl.program_id(1) == 0
    for r in range(rows // FF_ROWS):
        rs = slice(r * FF_ROWS, (r + 1) * FF_ROWS)
        hh = (res_ref[0, rs, :]
              + jnp.dot(oa_ref[0, rs, :], wo_ref[:ka, :], preferred_element_type=F32)
              + jnp.dot(ob_ref[0, rs, :], wo_ref[ka:, :], preferred_element_type=F32))
        h_ref[0, rs, :] = hh
        xn_ref[rs, :] = _rms(hh, g_ref[...]).astype(BF16)
    sub = lax.broadcasted_iota(jnp.int32, (8, FF_CHUNK), 0)
    zeros = jnp.zeros((8, FF_CHUNK), F32)
    for c in range(D_FF // FF_CHUNK):
        cols = slice(c * FF_CHUNK, (c + 1) * FF_CHUNK)
        wu = w_ref[:, cols]
        wg = w_ref[:, D_FF + c * FF_CHUNK:D_FF + (c + 1) * FF_CHUNK]
        cw = cw_ref[:, cols]
        cb = cb_ref[:, cols]
        prev1 = jnp.where(first, zeros, h1_ref[c])
        prev2 = jnp.where(first, zeros, h2_ref[c])
        for r in range(rows // FF_ROWS):
            rs = slice(r * FF_ROWS, (r + 1) * FF_ROWS)
            xn = xn_ref[rs, :]
            u = jnp.dot(xn, wu, preferred_element_type=F32)
            gt = jnp.dot(xn, wg, preferred_element_type=F32)
            r1 = pltpu.roll(gt, 1, 0)
            r2 = pltpu.roll(gt, 2, 0)
            g1 = jnp.concatenate([jnp.where(sub >= 1, r1[:8], prev1), r1[8:]], axis=0)
            g2 = jnp.concatenate([jnp.where(sub >= 2, r2[:8], prev2), r2[8:]], axis=0)
            prev1, prev2 = r1[:8], r2[:8]
            z = cw[2:3] * gt + cw[1:2] * g1 + cw[0:1] * g2 + cb
            gelu = 0.5 * z * (1.0 + lax.erf(z * math.sqrt(0.5)))
            act_ref[0, rs, cols] = (gelu * u).astype(BF16)
        h1_ref[c] = prev1
        h2_ref[c] = prev2


def attn_out_ffn_in(o_a, o_b, wo_stack, wo_layer, h3, g, w_stack, layer, conv_w, conv_b):
    b, s, d = h3.shape
    nc = D_FF // FF_CHUNK
    once = dict(pipeline_mode=pl.Buffered(1))
    rows3 = lambda w: pl.BlockSpec((1, FF_STEP_ROWS, w), lambda i, j: (i, j, 0))
    return pl.pallas_call(
        _ffn_in_kernel,
        grid=(b, s // FF_STEP_ROWS),
        in_specs=[rows3(o_a.shape[2]), rows3(o_b.shape[2]),
                  pl.BlockSpec((None,) + wo_stack.shape[1:], lambda i, j: (wo_layer, 0, 0), **once),
                  rows3(d),
                  pl.BlockSpec((1, d), lambda i, j: (0, 0)),
                  pl.BlockSpec((None, d, 2 * D_FF), lambda i, j: (layer, 0, 0), **once),
                  pl.BlockSpec((3, D_FF), lambda i, j: (0, 0)),
                  pl.BlockSpec((1, D_FF), lambda i, j: (0, 0))],
        out_specs=[rows3(D_FF), rows3(d)],
        out_shape=[jax.ShapeDtypeStruct((b, s, D_FF), BF16),
                   jax.ShapeDtypeStruct((b, s, d), F32)],
        scratch_shapes=[pltpu.VMEM((FF_STEP_ROWS, d), BF16),
                        pltpu.VMEM((nc, 8, FF_CHUNK), F32), pltpu.VMEM((nc, 8, FF_CHUNK), F32)],
        compiler_params=_params(2, VMEM_LIMIT),
        name="attn_out_ffn_in",
    )(o_a, o_b, wo_stack, h3, g.reshape(1, d), w_stack, conv_w, conv_b.reshape(1, D_FF))


def _bucket_of_distance(s):
    max_exact = REL_BUCKETS // 2
    n_large = REL_BUCKETS - max_exact
    thresholds = []
    for k in range(1, n_large):
        t = max_exact * (REL_MAX_DIST / max_exact) ** (k / n_large)
        ti = int(round(t))
        thresholds.append(ti if abs(t - ti) < 1e-9 else int(math.ceil(t)))
    d = np.arange(s)
    large = max_exact + sum((d >= t).astype(np.int64) for t in thresholds)
    return np.where(d < max_exact, d, np.minimum(large, REL_BUCKETS - 1)).astype(np.int32)


def _dilated_log_multiplicity(s):
    d = np.arange(s)
    count = np.zeros(s, np.int64)
    for window, dil in DIL_CONFIGS:
        count += ((d % dil == 0) & (d // dil <= window // dil)).astype(np.int64)
    return np.where(count > 0, np.log(np.maximum(count, 1)), NEG).astype(np.float32)


def _fill_bias_strip(strip_ref, vec_ref, n, s):
    for h in range(n):
        tile = jnp.broadcast_to(vec_ref[h], (TQ, s + TQ))
        tile = pltpu.roll(tile, 0, 1, stride=1, stride_axis=0)
        strip_ref[h] = tile[:, TQ:]


def bias_vectors(rel_bias, s):
    bucket = _bucket_of_distance(s)
    per_dist = jnp.take(rel_bias.T.astype(F32), jnp.asarray(bucket), axis=1)
    logmult = jnp.asarray(_dilated_log_multiplicity(s))
    dil0 = MOBA_HEADS + DIFF_HEADS
    covered = logmult > 0.5 * NEG
    dil_rows = jnp.where(covered[None, :], per_dist[dil0:] + logmult[None, :], NEG)
    per_dist = jnp.concatenate([per_dist[:dil0], dil_rows], axis=0) * LOG2E
    nh = per_dist.shape[0]
    w = jnp.concatenate([jnp.full((nh, 1), NEG, F32), per_dist[:, ::-1],
                         jnp.full((nh, TQ - 1), NEG, F32)], axis=1)
    return w.reshape(nh, 1, s + TQ)


def _store_transposed(kt_ref, idx, k):
    s = k.shape[0]
    for j in range(s // TK):
        kt_ref[idx, :, j * TK:(j + 1) * TK] = k[j * TK:(j + 1) * TK, :].T


def _block_order(n):
    return list(range(1, n, 2)) + list(range((n - 1) // 2 * 2, -1, -2))


def _attn_kernel(*refs, mode, moba, lam_init, s):
    if mode == "diff":
        (q_ref, k_ref, v_ref, bvec_ref, lam_ref, sg_ref, o_ref,
         kt_ref, sc_ref, p_ref, vone_ref, b_ref) = refs
    elif moba:
        (q_ref, k_ref, v_ref, bvec_ref, o_ref,
         kt_ref, sc_ref, p_ref, vone_ref, b_ref, drop_ref) = refs
    else:
        q_ref, k_ref, v_ref, bvec_ref, o_ref, kt_ref, sc_ref, p_ref, vone_ref, b_ref = refs
    n_seq = q_ref.shape[0]
    n_kt = 2 if moba else 1

    @pl.when(pl.program_id(1) == 0)
    def _():
        _fill_bias_strip(b_ref, bvec_ref, 2 if mode == "pair" else 1, s)

    nblk = s // TK
    lane = lax.broadcasted_iota(jnp.int32, (TQ, LANES), 1)
    halves = (lane < HEAD_DIM, lane >= HEAD_DIM)
    nt = (((1,), (1,)), ((), ()))

    for b in range(n_seq):
        vone_ref[b, :, :LANES] = v_ref[b]
        vone_ref[b, :, LANES:] = jnp.ones((s, LANES), BF16)
        if not moba:
            _store_transposed(kt_ref, b, k_ref[b])
            continue
        k_all = k_ref[b]
        q_all = q_ref[b]
        rowblk = lax.broadcasted_iota(jnp.int32, (s, LANES), 0) // TK
        lane_s = lax.broadcasted_iota(jnp.int32, (s, LANES), 1)
        kmean = jnp.mean(k_all.astype(F32).reshape(nblk, TK, LANES), axis=1)
        k_hi = kmean.astype(BF16)
        rem = kmean - k_hi.astype(F32)
        k_mid = rem.astype(BF16)
        k_lo = (rem - k_mid.astype(F32)).astype(BF16)
        kmean3 = jnp.concatenate([k_hi, k_mid, k_lo], axis=1)
        qblk = lax.broadcasted_iota(jnp.int32, (nblk, s), 1) // TQ
        blk = lax.broadcasted_iota(jnp.int32, (nblk, s), 0)
        for c in range(2):
            base = HEAD_DIM if c == 0 else 0
            in_half = (lane_s < HEAD_DIM) if c == 0 else (lane_s >= HEAD_DIM)
            onehot = jnp.where(lane_s - base == rowblk, 1.0, 0.0).astype(BF16)
            _store_transposed(kt_ref, 2 * b + c, jnp.where(in_half, k_all, onehot))
            qh = jnp.where(in_half, q_all, jnp.zeros_like(q_all))
            gate = lax.dot_general(kmean3, jnp.concatenate([qh, qh, qh], axis=1), nt,
                                   preferred_element_type=F32)
            rank = jnp.zeros((nblk, s), jnp.int32)
            for n in range(nblk - 1):
                gn = gate[n:n + 1, :]
                beats = jnp.where(gn > gate, 1, jnp.where(gn == gate, jnp.where(n < blk, 1, 0), 0))
                rank = rank + jnp.where(n < qblk, beats, 0)
            drop = jnp.where(blk < qblk, jnp.where(rank >= MOBA_TOPK, NEG, 0.0), 0.0)
            drop_ref[2 * b + c] = jnp.zeros((LANES, s), F32)
            drop_ref[2 * b + c, base:base + nblk, :] = drop

    if mode == "diff":
        lp = lam_ref[...]
        lam = (jnp.exp(jnp.sum(lp[0:1] * lp[1:2], axis=1, keepdims=True))
               - jnp.exp(jnp.sum(lp[2:3] * lp[3:4], axis=1, keepdims=True)) + lam_init)

    units = [(b, i, c) for b in range(n_seq) for i in _block_order(nblk) for c in range(2)]

    def scores(u):
        b, i, c = units[u]
        nk = (i + 1) * TK
        q = q_ref[b, i * TQ:(i + 1) * TQ, :]
        qc = jnp.where(halves[c], q, jnp.zeros_like(q))
        if moba and i > 0:
            sel = drop_ref[2 * b + c, :, i * TQ:(i + 1) * TQ].T
            qc = jnp.where(halves[c], q, sel.astype(BF16))
        sc_ref[u % 2, :, 0:nk] = (jnp.dot(qc, kt_ref[n_kt * b + (c if moba else 0), :, 0:nk],
                                          preferred_element_type=F32)
                                  + b_ref[c if mode == "pair" else 0, :, s - nk:])

    def softmax(u):
        nk = (units[u][1] + 1) * TK
        m = jnp.max(sc_ref[u % 2, :, 0:nk], axis=1, keepdims=True)
        p = jnp.exp2(sc_ref[u % 2, :, 0:nk] - m)
        p_ref[u % 2, :, 0:nk] = p.astype(BF16)

    outs = []

    def weighted_values(u):
        b, i, c = units[u]
        nk = (i + 1) * TK
        o = jnp.dot(p_ref[u % 2, :, 0:nk], vone_ref[b, 0:nk, :], preferred_element_type=F32)
        outs.append(o[:, :LANES] / o[:, LANES:])
        if c == 1:
            if mode == "pair":
                res = jnp.where(halves[0], outs[0], outs[1])
            else:
                res = _rms(outs[0] - lam * outs[1], sg_ref[...]) * (1.0 - lam_init)
            o_ref[b, i * TQ:(i + 1) * TQ, :] = res.astype(o_ref.dtype)
            outs.clear()

    scores(0)
    for u in range(len(units)):
        if u + 1 < len(units):
            scores(u + 1)
        softmax(u)
        if u > 0:
            weighted_values(u - 1)
    weighted_values(len(units) - 1)


def attention(qkv, col_q, col_k, col_v, n_blocks, bias_vecs, head0, mode, moba=False,
              lam_params=None, subln_g=None, lam_init=0.0):
    b, s, _ = qkv.shape
    nb = 2 if mode == "pair" else 1
    nsq = ATTN_SEQS
    seq = lambda col: pl.BlockSpec((nsq, s, LANES), lambda p, bb: (bb, 0, col + p))
    in_specs = [seq(col_q), seq(col_k), seq(col_v),
                pl.BlockSpec((nb, 1, s + TQ), lambda p, bb: (head0 // nb + p, 0, 0))]
    args = [qkv, qkv, qkv, bias_vecs]
    if mode == "diff":
        in_specs += [pl.BlockSpec((4, HEAD_DIM), lambda p, bb: (0, 0)),
                     pl.BlockSpec((1, LANES), lambda p, bb: (0, 0))]
        args += [lam_params.astype(F32), subln_g.reshape(1, LANES)]
    scratch = [pltpu.VMEM((nsq * (2 if moba else 1), LANES, s), BF16),
               pltpu.VMEM((2, TQ, s), F32), pltpu.VMEM((2, TQ, s), BF16),
               pltpu.VMEM((nsq, s, 2 * LANES), BF16), pltpu.VMEM((nb, TQ, s), F32)]
    if moba:
        scratch += [pltpu.VMEM((nsq * 2, LANES, s), F32)]
    return pl.pallas_call(
        functools.partial(_attn_kernel, mode=mode, moba=moba, lam_init=lam_init, s=s),
        grid=(n_blocks, b // nsq),
        in_specs=in_specs,
        out_specs=pl.BlockSpec((nsq, s, LANES), lambda p, bb: (bb, 0, p)),
        out_shape=jax.ShapeDtypeStruct((b, s, n_blocks * LANES), BF16),
        scratch_shapes=scratch,
        compiler_params=_params(2, VMEM_LIMIT),
        name="attn_" + mode + ("_moba" if moba else ""),
    )(*args)


def _mla_up_kernel(lat_ref, qg_ref, kvg_ref, wa_ref, wb_ref, wk_ref, wv_ref, cq_ref, sq_ref,
                   ck_ref, sk_ref, q_ref, k_ref, v_ref, *, per_seq):
    lat = lat_ref[...]
    tm = lat.shape[0]
    r0 = pl.multiple_of((pl.program_id(0) % per_seq) * tm, tm)
    cqn = _rms(lat[:, :MLA_Q_RANK], qg_ref[...]).astype(BF16)
    ckvn = _rms(lat[:, MLA_Q_RANK:MLA_Q_RANK + MLA_KV_RANK], kvg_ref[...]).astype(BF16)
    qa = jnp.dot(cqn, wa_ref[...], preferred_element_type=F32)
    qb = jnp.dot(cqn, wb_ref[...], preferred_element_type=F32)
    kn = jnp.dot(ckvn, wk_ref[...], preferred_element_type=F32).astype(BF16)
    v_ref[...] = jnp.dot(ckvn, wv_ref[...], preferred_element_type=F32).astype(BF16)
    x = lat[:, MLA_Q_RANK + MLA_KV_RANK:]
    kr = (x * ck_ref[pl.ds(r0, tm), :]
          + pltpu.roll(x, MLA_ROPE_DIM, 1) * sk_ref[pl.ds(r0, tm), :]).astype(BF16)
    cq = cq_ref[pl.ds(r0, tm), :]
    sq = sq_ref[pl.ds(r0, tm), :]
    hw = 2 * LANES
    for h in range(MLA_HEADS):
        q_ref[:, h * hw:(h + 1) * hw] = (qa[:, h * hw:(h + 1) * hw] * cq
                                         + qb[:, h * hw:(h + 1) * hw] * sq).astype(BF16)
        k_ref[:, h * hw:h * hw + LANES] = kn[:, h * LANES:(h + 1) * LANES]
        k_ref[:, h * hw + LANES:(h + 1) * hw] = kr


def mla_up(latent, q_norm, kv_norm, wa, wb, wk, wv, rope_tabs, s, tm=1024):
    m = latent.shape[0]
    cq, sq, ck, sk = rope_tabs
    per_seq = s // tm
    tab = lambda w: pl.BlockSpec((s, w), lambda i: (0, 0))
    full = lambda a: pl.BlockSpec(a.shape, lambda i: (0, 0))
    hw = 2 * LANES
    rows = lambda w: pl.BlockSpec((tm, w), lambda i: (i, 0))
    return pl.pallas_call(
        functools.partial(_mla_up_kernel, per_seq=per_seq),
        grid=(m // tm,),
        in_specs=[rows(latent.shape[1]),
                  pl.BlockSpec((1, MLA_Q_RANK), lambda i: (0, 0)),
                  pl.BlockSpec((1, MLA_KV_RANK), lambda i: (0, 0)),
                  full(wa), full(wb), full(wk), full(wv), tab(hw), tab(hw), tab(LANES), tab(LANES)],
        out_specs=[rows(MLA_HEADS * hw), rows(MLA_HEADS * hw), rows(MLA_HEADS * LANES)],
        out_shape=[jax.ShapeDtypeStruct((m, MLA_HEADS * hw), BF16),
                   jax.ShapeDtypeStruct((m, MLA_HEADS * hw), BF16),
                   jax.ShapeDtypeStruct((m, MLA_HEADS * LANES), BF16)],
        compiler_params=_params(1, VMEM_LIMIT),
        name="mla_up",
    )(latent, q_norm.reshape(1, -1), kv_norm.reshape(1, -1), wa, wb, wk, wv, cq, sq, ck, sk)


def _mla_attn_kernel(q_ref, k_ref, v_ref, o_ref, kt_ref, sc_ref, p_ref, vone_ref, *, s):
    scale = (MLA_NOPE_DIM + MLA_ROPE_DIM) ** -0.5 * LOG2E
    _store_transposed(kt_ref, 0, k_ref[0])
    vone_ref[:, :LANES] = v_ref[0]
    vone_ref[:, LANES:] = jnp.ones((s, LANES), BF16)
    row = lax.broadcasted_iota(jnp.int32, (TQ, TK), 0)
    col = lax.broadcasted_iota(jnp.int32, (TQ, TK), 1)
    causal = row >= col
    nblk = s // TQ

    order = _block_order(nblk)

    def scores(u):
        i = order[u]
        nk = (i + 1) * TK
        q = q_ref[0, i * TQ:(i + 1) * TQ, :]
        sc = jnp.dot(q, kt_ref[0, :, 0:nk], preferred_element_type=F32) * scale
        if i > 0:
            sc_ref[u % 2, :, 0:nk - TK] = sc[:, :nk - TK]
        sc_ref[u % 2, :, nk - TK:nk] = jnp.where(causal, sc[:, nk - TK:], NEG)

    def softmax(u):
        nk = (order[u] + 1) * TK
        m = jnp.max(sc_ref[u % 2, :, 0:nk], axis=1, keepdims=True)
        p = jnp.exp2(sc_ref[u % 2, :, 0:nk] - m)
        p_ref[u % 2, :, 0:nk] = p.astype(BF16)

    def weighted_values(u):
        i = order[u]
        nk = (i + 1) * TK
        o = jnp.dot(p_ref[u % 2, :, 0:nk], vone_ref[0:nk, :], preferred_element_type=F32)
        o_ref[0, i * TQ:(i + 1) * TQ, :] = (o[:, :LANES] / o[:, LANES:]).astype(o_ref.dtype)

    scores(0)
    for u in range(nblk):
        if u + 1 < nblk:
            scores(u + 1)
        softmax(u)
        if u > 0:
            weighted_values(u - 1)
    weighted_values(nblk - 1)


def mla_attention(q, k, v):
    b, s, _ = q.shape
    hw = 2 * LANES
    return pl.pallas_call(
        functools.partial(_mla_attn_kernel, s=s),
        grid=(MLA_HEADS, b),
        in_specs=[pl.BlockSpec((1, s, hw), lambda h, bb: (bb, 0, h)),
                  pl.BlockSpec((1, s, hw), lambda h, bb: (bb, 0, h)),
                  pl.BlockSpec((1, s, LANES), lambda h, bb: (bb, 0, h))],
        out_specs=pl.BlockSpec((1, s, LANES), lambda h, bb: (bb, 0, h)),
        out_shape=jax.ShapeDtypeStruct((b, s, MLA_HEADS * LANES), BF16),
        scratch_shapes=[pltpu.VMEM((1, hw, s), BF16),
                        pltpu.VMEM((2, TQ, s), F32), pltpu.VMEM((2, TQ, s), BF16),
                        pltpu.VMEM((s, 2 * LANES), BF16)],
        compiler_params=_params(2, VMEM_LIMIT),
        name="mla_attn",
    )(q, k, v)


def _rope_tables(s):
    half = MLA_ROPE_DIM // 2
    freq = ROPE_THETA ** (-jnp.arange(half, dtype=F32) / half)
    ang = jnp.arange(s, dtype=F32)[:, None] * freq[None, :]
    cos, sin = jnp.cos(ang), jnp.sin(ang)
    cos2 = jnp.concatenate([cos, cos], axis=1)
    sin2 = jnp.concatenate([-sin, sin], axis=1)
    z64 = jnp.zeros((s, MLA_ROPE_DIM), F32)
    cq = jnp.concatenate([jnp.ones((s, MLA_NOPE_DIM), F32), cos2, z64], axis=1)
    sq = jnp.concatenate([jnp.zeros((s, MLA_NOPE_DIM), F32), sin2, z64], axis=1)
    ck = jnp.concatenate([cos2, z64], axis=1)
    sk = jnp.concatenate([sin2, z64], axis=1)
    return cq, sq, ck, sk


def _swap_halves(w):
    half = w.shape[-1] // 2
    return jnp.concatenate([w[..., half:], w[..., :half]], axis=-1)


def _mla_q_weights(w_uq):
    r = w_uq.shape[0]
    w = w_uq.reshape(r, MLA_HEADS, MLA_NOPE_DIM + MLA_ROPE_DIM)
    nope, rope = w[..., :MLA_NOPE_DIM], w[..., MLA_NOPE_DIM:]
    z64 = jnp.zeros((r, MLA_HEADS, MLA_ROPE_DIM), w.dtype)
    z128 = jnp.zeros((r, MLA_HEADS, MLA_NOPE_DIM), w.dtype)
    wa = jnp.concatenate([nope, rope, z64], axis=-1).reshape(r, -1)
    wb = jnp.concatenate([z128, _swap_halves(rope), z64], axis=-1).reshape(r, -1)
    return wa.astype(BF16), wb.astype(BF16)


def kernel(x, rel_bias, even_norm1, even_w_in, diff_lambda, diff_subln, even_w_out,
           odd_norm1, odd_w_in, mla_q_norm, mla_w_uq, mla_kv_norm, mla_w_ukv, odd_w_out,
           ffn_norm, ffn_w_in, ffn_conv_w, ffn_conv_b, ffn_w_out, final_norm):
    b, s, d = x.shape
    m = b * s
    bias_vecs = bias_vectors(rel_bias, s)
    rope_tabs = _rope_tables(s)
    qscale = HEAD_DIM ** -0.5 * LOG2E
    ffn_w_in_bf16 = ffn_w_in.astype(BF16)
    even_w_out_bf16 = even_w_out.astype(BF16)
    odd_w_out_bf16 = odd_w_out.astype(BF16)
    h = x.reshape(m, d)
    for layer in range(DEPTH):
        li = layer // 2
        if layer % 2 == 0:
            lam_init = 0.8 - 0.6 * math.exp(-0.3 * layer)
            mw, dw = MOBA_HEADS * HEAD_DIM, DIFF_HEADS * 2 * HEAD_DIM
            colscale = np.ones((3 * mw + 3 * dw,), np.float32)
            colscale[:mw] = qscale
            colscale[3 * mw:3 * mw + dw] = qscale
            w = (even_w_in[li] * jnp.asarray(colscale)).astype(BF16)
            (qkv,) = in_proj(h, even_norm1[li], w, [w.shape[1]], [BF16])
            qkv = qkv.reshape(b, s, -1)
            nb = mw // LANES
            o_a = attention(qkv, 0, nb, 2 * nb, nb, bias_vecs, 0, "pair", moba=True)
            o_b = attention(qkv, 3 * nb, 4 * nb, 5 * nb, DIFF_HEADS, bias_vecs, MOBA_HEADS, "diff",
                            lam_params=diff_lambda[li], subln_g=diff_subln[li], lam_init=lam_init)
            wo = even_w_out_bf16
        else:
            w = odd_w_in[li]
            dw = DIL_HEADS * HEAD_DIM
            lat0 = 3 * dw
            colscale = np.ones((w.shape[1],), np.float32)
            colscale[:dw] = qscale
            kr_cols = w[:, lat0 + MLA_Q_RANK + MLA_KV_RANK:]
            w_all = jnp.concatenate([w * jnp.asarray(colscale), _swap_halves(kr_cols)],
                                    axis=1).astype(BF16)
            qkv, latent = in_proj(h, odd_norm1[li], w_all,
                                  [lat0, w_all.shape[1] - lat0], [BF16, F32])
            qkv = qkv.reshape(b, s, -1)
            nb = dw // LANES
            o_a = attention(qkv, 0, nb, 2 * nb, nb, bias_vecs, MOBA_HEADS + DIFF_HEADS, "pair")
            wa, wb = _mla_q_weights(mla_w_uq[li])
            wkv = mla_w_ukv[li].reshape(MLA_KV_RANK, MLA_HEADS, MLA_NOPE_DIM + MLA_V_DIM)
            wk = wkv[..., :MLA_NOPE_DIM].reshape(MLA_KV_RANK, -1).astype(BF16)
            wv = wkv[..., MLA_NOPE_DIM:].reshape(MLA_KV_RANK, -1).astype(BF16)
            q_m, k_m, v_m = mla_up(latent, mla_q_norm[li], mla_kv_norm[li], wa, wb, wk, wv,
                                   rope_tabs, s)
            o_b = mla_attention(q_m.reshape(b, s, -1), k_m.reshape(b, s, -1),
                                v_m.reshape(b, s, -1))
            wo = odd_w_out_bf16
        act, h3 = attn_out_ffn_in(o_a, o_b, wo, li, h.reshape(b, s, d), ffn_norm[layer],
                                  ffn_w_in_bf16, layer, ffn_conv_w[layer], ffn_conv_b[layer])
        h = proj_residual([act.reshape(m, D_FF)], ffn_w_out, layer, h3.reshape(m, d),
                          final_g=final_norm if layer == DEPTH - 1 else None)
    return h.reshape(b, s, d)
```

```python
import functools
import math

import numpy as np
import jax
import jax.numpy as jnp
from jax import lax
from jax.experimental import pallas as pl
from jax.experimental.pallas import tpu as pltpu

D_MODEL = 1024
DEPTH = 4
HEAD_DIM = 64
MOBA_HEADS = 8
MOBA_BLOCK = 256
MOBA_TOPK = 3
DIFF_HEADS = 4
DIL_HEADS = 8
DIL_CONFIGS = ((128, 1), (512, 4), (2048, 16))
MLA_HEADS = 4
MLA_Q_RANK = 256
MLA_KV_RANK = 128
MLA_NOPE_DIM = 128
MLA_ROPE_DIM = 64
MLA_V_DIM = 128
ROPE_THETA = 10000.0
REL_BUCKETS = 32
REL_MAX_DIST = 1024
N_BIAS_HEADS = MOBA_HEADS + DIFF_HEADS + DIL_HEADS
D_FF = 2816
EPS = 1e-6
NEG = -1e30
LOG2E = math.log2(math.e)

LANES = 128
TQ = 256
TK = 256
ATTN_SEQS = 2
FF_STEP_ROWS = 1024
FF_CHUNK = 256
FF_ROWS = 512
VMEM_LIMIT = 56 * 1024 * 1024

F32 = jnp.float32
BF16 = jnp.bfloat16


def _params(n_axes, vmem=None):
    return pltpu.CompilerParams(dimension_semantics=("arbitrary",) * n_axes,
                                vmem_limit_bytes=vmem)


def _rms(x, g):
    ms = jnp.mean(x * x, axis=-1, keepdims=True)
    return x * lax.rsqrt(ms + EPS) * g


def _in_proj_kernel(x_ref, g_ref, w_ref, cast_ref, *o_refs, row_chunk):
    *o_refs, cast_out_ref = o_refs
    cast_out_ref[...] = cast_ref[...].astype(BF16)
    tm = x_ref.shape[0]
    g = g_ref[...]
    for r in range(tm // row_chunk):
        rows = slice(r * row_chunk, (r + 1) * row_chunk)
        xn = _rms(x_ref[rows, :], g).astype(BF16)
        acc = jnp.dot(xn, w_ref[...], preferred_element_type=F32)
        col = 0
        for o_ref in o_refs:
            width = o_ref.shape[1]
            o_ref[rows, :] = acc[:, col:col + width].astype(o_ref.dtype)
            col += width


def in_proj(x, g, w, out_widths, out_dtypes, cast_stack, cast_layer, tm=1024, row_chunk=512):
    m, d = x.shape
    steps = m // tm
    rows, cols = cast_stack.shape[1:]
    slab = rows // steps
    return pl.pallas_call(
        functools.partial(_in_proj_kernel, row_chunk=row_chunk),
        grid=(steps,),
        in_specs=[pl.BlockSpec((tm, d), lambda i: (i, 0)),
                  pl.BlockSpec((1, d), lambda i: (0, 0)),
                  pl.BlockSpec(w.shape, lambda i: (0, 0)),
                  pl.BlockSpec((None, slab, cols), lambda i: (cast_layer, i, 0))],
        out_specs=[pl.BlockSpec((tm, n), lambda i: (i, 0)) for n in out_widths]
        + [pl.BlockSpec((slab, cols), lambda i: (i, 0))],
        out_shape=[jax.ShapeDtypeStruct((m, n), dt) for n, dt in zip(out_widths, out_dtypes)]
        + [jax.ShapeDtypeStruct((rows, cols), BF16)],
        compiler_params=_params(1, VMEM_LIMIT),
        name="in_proj",
    )(x, g.reshape(1, d), w, cast_stack)


def _proj_res_kernel(*refs, n_in, final_norm, row_chunk):
    a_refs = refs[:n_in]
    w_ref, res_ref = refs[n_in], refs[n_in + 1]
    o_ref, wb_ref = refs[-2], refs[-1]

    @pl.when(pl.program_id(0) == 0)
    def _():
        wb_ref[...] = w_ref[...].astype(BF16)

    for r in range(res_ref.shape[0] // row_chunk):
        rows = slice(r * row_chunk, (r + 1) * row_chunk)
        acc = res_ref[rows, :]
        row = 0
        for a_ref in a_refs:
            k = a_ref.shape[1]
            acc = acc + jnp.dot(a_ref[rows, :], wb_ref[row:row + k, :],
                                preferred_element_type=F32)
            row += k
        if final_norm:
            acc = _rms(acc, refs[n_in + 2][...])
        o_ref[rows, :] = acc


def proj_residual(acts, w_stack, layer, res, final_g=None, tm=1024, row_chunk=512):
    m, d = res.shape
    n_in = len(acts)
    in_specs = [pl.BlockSpec((tm, a.shape[1]), lambda i: (i, 0)) for a in acts]
    in_specs += [pl.BlockSpec((None,) + w_stack.shape[1:], lambda i: (layer, 0, 0),
                              pipeline_mode=pl.Buffered(1)),
                 pl.BlockSpec((tm, d), lambda i: (i, 0))]
    args = list(acts) + [w_stack, res]
    if final_g is not None:
        in_specs += [pl.BlockSpec((1, d), lambda i: (0, 0))]
        args += [final_g.reshape(1, d)]
    return pl.pallas_call(
        functools.partial(_proj_res_kernel, n_in=n_in, final_norm=final_g is not None,
                          row_chunk=row_chunk),
        grid=(m // tm,),
        in_specs=in_specs,
        out_specs=pl.BlockSpec((tm, d), lambda i: (i, 0)),
        out_shape=jax.ShapeDtypeStruct((m, d), F32),
        scratch_shapes=[pltpu.VMEM(w_stack.shape[1:], BF16)],
        compiler_params=_params(1, VMEM_LIMIT),
        name="proj_residual",
    )(*args)


def _ffn_in_kernel(oa_ref, ob_ref, wo_ref, res_ref, g_ref, w_ref, cw_ref, cb_ref,
                   act_ref, h_ref, xn_ref, h1_ref, h2_ref):
    rows = res_ref.shape[1]
    ka = oa_ref.shape[2]
    first = pl.program_id(1) == 0
    for r in range(rows // FF_ROWS):
        rs = slice(r * FF_ROWS, (r + 1) * FF_ROWS)
        hh = (res_ref[0, rs, :]
              + jnp.dot(oa_ref[0, rs, :], wo_ref[:ka, :], preferred_element_type=F32)
              + jnp.dot(ob_ref[0, rs, :], wo_ref[ka:, :], preferred_element_type=F32))
        h_ref[0, rs, :] = hh
        xn_ref[rs, :] = _rms(hh, g_ref[...]).astype(BF16)
    sub = lax.broadcasted_iota(jnp.int32, (8, FF_CHUNK), 0)
    zeros = jnp.zeros((8, FF_CHUNK), F32)
    for c in range(D_FF // FF_CHUNK):
        cols = slice(c * FF_CHUNK, (c + 1) * FF_CHUNK)
        wu = w_ref[:, cols]
        wg = w_ref[:, D_FF + c * FF_CHUNK:D_FF + (c + 1) * FF_CHUNK]
        cw = cw_ref[:, cols]
        cb = cb_ref[:, cols]
        prev1 = jnp.where(first, zeros, h1_ref[c])
        prev2 = jnp.where(first, zeros, h2_ref[c])
        for r in range(rows // FF_ROWS):
            rs = slice(r * FF_ROWS, (r + 1) * FF_ROWS)
            xn = xn_ref[rs, :]
            u = jnp.dot(xn, wu, preferred_element_type=F32)
            gt = jnp.dot(xn, wg, preferred_element_type=F32)
            r1 = pltpu.roll(gt, 1, 0)
            r2 = pltpu.roll(gt, 2, 0)
            g1 = jnp.concatenate([jnp.where(sub >= 1, r1[:8], prev1), r1[8:]], axis=0)
            g2 = jnp.concatenate([jnp.where(sub >= 2, r2[:8], prev2), r2[8:]], axis=0)
            prev1, prev2 = r1[:8], r2[:8]
            z = cw[2:3] * gt + cw[1:2] * g1 + cw[0:1] * g2 + cb
            gelu = 0.5 * z * (1.0 + lax.erf(z * math.sqrt(0.5)))
            act_ref[0, rs, cols] = (gelu * u).astype(BF16)
        h1_ref[c] = prev1
        h2_ref[c] = prev2


def attn_out_ffn_in(o_a, o_b, wo_stack, wo_layer, h3, g, w_ug, conv_w, conv_b):
    b, s, d = h3.shape
    nc = D_FF // FF_CHUNK
    once = dict(pipeline_mode=pl.Buffered(1))
    rows3 = lambda w: pl.BlockSpec((1, FF_STEP_ROWS, w), lambda i, j: (i, j, 0))
    return pl.pallas_call(
        _ffn_in_kernel,
        grid=(b, s // FF_STEP_ROWS),
        in_specs=[rows3(o_a.shape[2]), rows3(o_b.shape[2]),
                  pl.BlockSpec((None,) + wo_stack.shape[1:], lambda i, j: (wo_layer, 0, 0), **once),
                  rows3(d),
                  pl.BlockSpec((1, d), lambda i, j: (0, 0)),
                  pl.BlockSpec((d, 2 * D_FF), lambda i, j: (0, 0), **once),
                  pl.BlockSpec((3, D_FF), lambda i, j: (0, 0)),
                  pl.BlockSpec((1, D_FF), lambda i, j: (0, 0))],
        out_specs=[rows3(D_FF), rows3(d)],
        out_shape=[jax.ShapeDtypeStruct((b, s, D_FF), BF16),
                   jax.ShapeDtypeStruct((b, s, d), F32)],
        scratch_shapes=[pltpu.VMEM((FF_STEP_ROWS, d), BF16),
                        pltpu.VMEM((nc, 8, FF_CHUNK), F32), pltpu.VMEM((nc, 8, FF_CHUNK), F32)],
        compiler_params=_params(2, VMEM_LIMIT),
        name="attn_out_ffn_in",
    )(o_a, o_b, wo_stack, h3, g.reshape(1, d), w_ug, conv_w, conv_b.reshape(1, D_FF))


def _bucket_of_distance(s):
    max_exact = REL_BUCKETS // 2
    n_large = REL_BUCKETS - max_exact
    thresholds = []
    for k in range(1, n_large):
        t = max_exact * (REL_MAX_DIST / max_exact) ** (k / n_large)
        ti = int(round(t))
        thresholds.append(ti if abs(t - ti) < 1e-9 else int(math.ceil(t)))
    d = np.arange(s)
    large = max_exact + sum((d >= t).astype(np.int64) for t in thresholds)
    return np.where(d < max_exact, d, np.minimum(large, REL_BUCKETS - 1)).astype(np.int32)


def _dilated_log_multiplicity(s):
    d = np.arange(s)
    count = np.zeros(s, np.int64)
    for window, dil in DIL_CONFIGS:
        count += ((d % dil == 0) & (d // dil <= window // dil)).astype(np.int64)
    return np.where(count > 0, np.log(np.maximum(count, 1)), NEG).astype(np.float32)


def _fill_bias_strip(strip_ref, vec_ref, n, s):
    for h in range(n):
        tile = jnp.broadcast_to(vec_ref[h], (TQ, s + TQ))
        tile = pltpu.roll(tile, 0, 1, stride=1, stride_axis=0)
        strip_ref[h] = tile[:, TQ:]


def bias_vectors(rel_bias, s):
    bucket = _bucket_of_distance(s)
    per_dist = jnp.take(rel_bias.T.astype(F32), jnp.asarray(bucket), axis=1)
    logmult = jnp.asarray(_dilated_log_multiplicity(s))
    dil0 = MOBA_HEADS + DIFF_HEADS
    covered = logmult > 0.5 * NEG
    dil_rows = jnp.where(covered[None, :], per_dist[dil0:] + logmult[None, :], NEG)
    per_dist = jnp.concatenate([per_dist[:dil0], dil_rows], axis=0) * LOG2E
    nh = per_dist.shape[0]
    w = jnp.concatenate([jnp.full((nh, 1), NEG, F32), per_dist[:, ::-1],
                         jnp.full((nh, TQ - 1), NEG, F32)], axis=1)
    return w.reshape(nh, 1, s + TQ)


def _store_transposed(kt_ref, idx, k):
    s, width = k.shape
    for j in range(s // TK):
        kt_ref[idx, 0:width, j * TK:(j + 1) * TK] = k[j * TK:(j + 1) * TK, :].T


def _block_order(n):
    return list(range(1, n, 2)) + list(range((n - 1) // 2 * 2, -1, -2))


def _attn_kernel(*refs, mode, moba, lam_init, s):
    if mode == "diff":
        (q_ref, k_ref, v_ref, bvec_ref, lam_ref, sg_ref, o_ref,
         kt_ref, sc_ref, p_ref, vone_ref, b_ref) = refs
    elif moba:
        (q_ref, k_ref, v_ref, bvec_ref, o_ref,
         kt_ref, sc_ref, p_ref, vone_ref, b_ref, drop_ref) = refs
    else:
        q_ref, k_ref, v_ref, bvec_ref, o_ref, kt_ref, sc_ref, p_ref, vone_ref, b_ref = refs
    n_seq = q_ref.shape[0]

    @pl.when(pl.program_id(1) == 0)
    def _():
        _fill_bias_strip(b_ref, bvec_ref, 2 if mode == "pair" else 1, s)

    nblk = s // TK
    lane = lax.broadcasted_iota(jnp.int32, (TQ, LANES), 1)
    halves = (lane < HEAD_DIM, lane >= HEAD_DIM)
    nt = (((1,), (1,)), ((), ()))

    for b in range(n_seq):
        vone_ref[b, :, :LANES] = v_ref[b]
        vone_ref[b, :, LANES:] = jnp.ones((s, LANES), BF16)
        _store_transposed(kt_ref, b, k_ref[b])
        if not moba:
            continue
        k_all = k_ref[b]
        q_all = q_ref[b]
        sel_row = lax.broadcasted_iota(jnp.int32, (LANES, s), 0)
        key_blk = lax.broadcasted_iota(jnp.int32, (LANES, s), 1) // TK
        kt_ref[b, LANES:, :] = jnp.where((sel_row < 2 * nblk) & (sel_row % nblk == key_blk),
                                         1.0, 0.0).astype(BF16)
        lane_s = lax.broadcasted_iota(jnp.int32, (s, LANES), 1)
        kmean = jnp.mean(k_all.astype(F32).reshape(nblk, TK, LANES), axis=1)
        k_hi = kmean.astype(BF16)
        rem = kmean - k_hi.astype(F32)
        k_mid = rem.astype(BF16)
        k_lo = (rem - k_mid.astype(F32)).astype(BF16)
        kmean3 = jnp.concatenate([k_hi, k_mid, k_lo], axis=1)
        qblk = lax.broadcasted_iota(jnp.int32, (nblk, s), 1) // TQ
        blk = lax.broadcasted_iota(jnp.int32, (nblk, s), 0)
        drop_ref[b] = jnp.zeros((LANES, s), F32)
        for c in range(2):
            in_half = (lane_s < HEAD_DIM) if c == 0 else (lane_s >= HEAD_DIM)
            qh = jnp.where(in_half, q_all, jnp.zeros_like(q_all))
            gate = lax.dot_general(kmean3, jnp.concatenate([qh, qh, qh], axis=1), nt,
                                   preferred_element_type=F32)
            rank = jnp.zeros((nblk, s), jnp.int32)
            for n in range(nblk - 1):
                gn = gate[n:n + 1, :]
                beats = jnp.where(gn > gate, 1, jnp.where(gn == gate, jnp.where(n < blk, 1, 0), 0))
                rank = rank + jnp.where(n < qblk, beats, 0)
            drop = jnp.where(blk < qblk, jnp.where(rank >= MOBA_TOPK, NEG, 0.0), 0.0)
            drop_ref[b, c * nblk:(c + 1) * nblk, :] = drop

    if mode == "diff":
        lp = lam_ref[...]
        lam = (jnp.exp(jnp.sum(lp[0:1] * lp[1:2], axis=1, keepdims=True))
               - jnp.exp(jnp.sum(lp[2:3] * lp[3:4], axis=1, keepdims=True)) + lam_init)

    units = [(b, i, c) for b in range(n_seq) for i in _block_order(nblk) for c in range(2)]
    sel_cache = {}

    def scores(u):
        b, i, c = units[u]
        nk = (i + 1) * TK
        q = q_ref[b, i * TQ:(i + 1) * TQ, :]
        qc = jnp.where(halves[c], q, jnp.zeros_like(q))
        if moba:
            if (b, i) not in sel_cache:
                sel_cache[b, i] = drop_ref[b, :, i * TQ:(i + 1) * TQ].T.astype(BF16)
            mine = (lane >= c * nblk) & (lane < (c + 1) * nblk)
            sel = jnp.where(mine, sel_cache[b, i], jnp.zeros_like(q))
            qc = jnp.concatenate([qc, sel], axis=1)
        sc_ref[u % 2, :, 0:nk] = (jnp.dot(qc, kt_ref[b, :, 0:nk], preferred_element_type=F32)
                                  + b_ref[c if mode == "pair" else 0, :, s - nk:])

    def softmax(u):
        nk = (units[u][1] + 1) * TK
        m = jnp.max(sc_ref[u % 2, :, 0:nk], axis=1, keepdims=True)
        p = jnp.exp2(sc_ref[u % 2, :, 0:nk] - m)
        p_ref[u % 2, :, 0:nk] = p.astype(BF16)

    outs = []

    def weighted_values(u):
        b, i, c = units[u]
        nk = (i + 1) * TK
        o = jnp.dot(p_ref[u % 2, :, 0:nk], vone_ref[b, 0:nk, :], preferred_element_type=F32)
        outs.append(o[:, :LANES] / o[:, LANES:])
        if c == 1:
            if mode == "pair":
                res = jnp.where(halves[0], outs[0], outs[1])
            else:
                res = _rms(outs[0] - lam * outs[1], sg_ref[...]) * (1.0 - lam_init)
            o_ref[b, i * TQ:(i + 1) * TQ, :] = res.astype(o_ref.dtype)
            outs.clear()

    scores(0)
    for u in range(len(units)):
        if u + 1 < len(units):
            scores(u + 1)
        softmax(u)
        if u > 0:
            weighted_values(u - 1)
    weighted_values(len(units) - 1)


def attention(qkv, col_q, col_k, col_v, n_blocks, bias_vecs, head0, mode, moba=False,
              lam_params=None, subln_g=None, lam_init=0.0):
    b, s, _ = qkv.shape
    nb = 2 if mode == "pair" else 1
    nsq = ATTN_SEQS
    seq = lambda col: pl.BlockSpec((nsq, s, LANES), lambda p, bb: (bb, 0, col + p))
    in_specs = [seq(col_q), seq(col_k), seq(col_v),
                pl.BlockSpec((nb, 1, s + TQ), lambda p, bb: (head0 // nb + p, 0, 0))]
    args = [qkv, qkv, qkv, bias_vecs]
    if mode == "diff":
        in_specs += [pl.BlockSpec((4, HEAD_DIM), lambda p, bb: (0, 0)),
                     pl.BlockSpec((1, LANES), lambda p, bb: (0, 0))]
        args += [lam_params.astype(F32), subln_g.reshape(1, LANES)]
    scratch = [pltpu.VMEM((nsq, 2 * LANES if moba else LANES, s), BF16),
               pltpu.VMEM((2, TQ, s), F32), pltpu.VMEM((2, TQ, s), BF16),
               pltpu.VMEM((nsq, s, 2 * LANES), BF16), pltpu.VMEM((nb, TQ, s), F32)]
    if moba:
        scratch += [pltpu.VMEM((nsq, LANES, s), F32)]
    return pl.pallas_call(
        functools.partial(_attn_kernel, mode=mode, moba=moba, lam_init=lam_init, s=s),
        grid=(n_blocks, b // nsq),
        in_specs=in_specs,
        out_specs=pl.BlockSpec((nsq, s, LANES), lambda p, bb: (bb, 0, p)),
        out_shape=jax.ShapeDtypeStruct((b, s, n_blocks * LANES), BF16),
        scratch_shapes=scratch,
        compiler_params=_params(2, VMEM_LIMIT),
        name="attn_" + mode + ("_moba" if moba else ""),
    )(*args)


def _mla_up_kernel(lat_ref, qg_ref, kvg_ref, wa_ref, wb_ref, wk_ref, wv_ref, cq_ref, sq_ref,
                   ck_ref, sk_ref, q_ref, k_ref, v_ref, *, per_seq):
    lat = lat_ref[...]
    tm = lat.shape[0]
    r0 = pl.multiple_of((pl.program_id(0) % per_seq) * tm, tm)
    cqn = _rms(lat[:, :MLA_Q_RANK], qg_ref[...]).astype(BF16)
    ckvn = _rms(lat[:, MLA_Q_RANK:MLA_Q_RANK + MLA_KV_RANK], kvg_ref[...]).astype(BF16)
    qa = jnp.dot(cqn, wa_ref[...], preferred_element_type=F32)
    qb = jnp.dot(cqn, wb_ref[...], preferred_element_type=F32)
    kn = jnp.dot(ckvn, wk_ref[...], preferred_element_type=F32).astype(BF16)
    v_ref[...] = jnp.dot(ckvn, wv_ref[...], preferred_element_type=F32).astype(BF16)
    x = lat[:, MLA_Q_RANK + MLA_KV_RANK:]
    kr = (x * ck_ref[pl.ds(r0, tm), :]
          + pltpu.roll(x, MLA_ROPE_DIM, 1) * sk_ref[pl.ds(r0, tm), :]).astype(BF16)
    cq = cq_ref[pl.ds(r0, tm), :]
    sq = sq_ref[pl.ds(r0, tm), :]
    hw = 2 * LANES
    for h in range(MLA_HEADS):
        q_ref[:, h * hw:(h + 1) * hw] = (qa[:, h * hw:(h + 1) * hw] * cq
                                         + qb[:, h * hw:(h + 1) * hw] * sq).astype(BF16)
        k_ref[:, h * hw:h * hw + LANES] = kn[:, h * LANES:(h + 1) * LANES]
        k_ref[:, h * hw + LANES:(h + 1) * hw] = kr


def mla_up(latent, q_norm, kv_norm, wa, wb, wk, wv, rope_tabs, s, tm=1024):
    m = latent.shape[0]
    cq, sq, ck, sk = rope_tabs
    per_seq = s // tm
    tab = lambda w: pl.BlockSpec((s, w), lambda i: (0, 0))
    full = lambda a: pl.BlockSpec(a.shape, lambda i: (0, 0))
    hw = 2 * LANES
    rows = lambda w: pl.BlockSpec((tm, w), lambda i: (i, 0))
    return pl.pallas_call(
        functools.partial(_mla_up_kernel, per_seq=per_seq),
        grid=(m // tm,),
        in_specs=[rows(latent.shape[1]),
                  pl.BlockSpec((1, MLA_Q_RANK), lambda i: (0, 0)),
                  pl.BlockSpec((1, MLA_KV_RANK), lambda i: (0, 0)),
                  full(wa), full(wb), full(wk), full(wv), tab(hw), tab(hw), tab(LANES), tab(LANES)],
        out_specs=[rows(MLA_HEADS * hw), rows(MLA_HEADS * hw), rows(MLA_HEADS * LANES)],
        out_shape=[jax.ShapeDtypeStruct((m, MLA_HEADS * hw), BF16),
                   jax.ShapeDtypeStruct((m, MLA_HEADS * hw), BF16),
                   jax.ShapeDtypeStruct((m, MLA_HEADS * LANES), BF16)],
        compiler_params=_params(1, VMEM_LIMIT),
        name="mla_up",
    )(latent, q_norm.reshape(1, -1), kv_norm.reshape(1, -1), wa, wb, wk, wv, cq, sq, ck, sk)


def _mla_attn_kernel(q_ref, k_ref, v_ref, o_ref, kt_ref, sc_ref, p_ref, vone_ref, *, s):
    scale = (MLA_NOPE_DIM + MLA_ROPE_DIM) ** -0.5 * LOG2E
    n_seq = q_ref.shape[0]
    for b in range(n_seq):
        _store_transposed(kt_ref, b, k_ref[b])
        vone_ref[b, :, :LANES] = v_ref[b]
        vone_ref[b, :, LANES:] = jnp.ones((s, LANES), BF16)
    row = lax.broadcasted_iota(jnp.int32, (TQ, TK), 0)
    col = lax.broadcasted_iota(jnp.int32, (TQ, TK), 1)
    causal = row >= col
    nblk = s // TQ

    order = [(b, i) for b in range(n_seq) for i in _block_order(nblk)]

    def scores(u):
        b, i = order[u]
        nk = (i + 1) * TK
        q = q_ref[b, i * TQ:(i + 1) * TQ, :]
        sc = jnp.dot(q, kt_ref[b, :, 0:nk], preferred_element_type=F32) * scale
        if i > 0:
            sc_ref[u % 2, :, 0:nk - TK] = sc[:, :nk - TK]
        sc_ref[u % 2, :, nk - TK:nk] = jnp.where(causal, sc[:, nk - TK:], NEG)

    def softmax(u):
        nk = (order[u][1] + 1) * TK
        m = jnp.max(sc_ref[u % 2, :, 0:nk], axis=1, keepdims=True)
        p = jnp.exp2(sc_ref[u % 2, :, 0:nk] - m)
        p_ref[u % 2, :, 0:nk] = p.astype(BF16)

    def weighted_values(u):
        b, i = order[u]
        nk = (i + 1) * TK
        o = jnp.dot(p_ref[u % 2, :, 0:nk], vone_ref[b, 0:nk, :], preferred_element_type=F32)
        o_ref[b, i * TQ:(i + 1) * TQ, :] = (o[:, :LANES] / o[:, LANES:]).astype(o_ref.dtype)

    scores(0)
    for u in range(len(order)):
        if u + 1 < len(order):
            scores(u + 1)
        softmax(u)
        if u > 0:
            weighted_values(u - 1)
    weighted_values(len(order) - 1)


def mla_attention(q, k, v):
    b, s, _ = q.shape
    hw = 2 * LANES
    nsq = ATTN_SEQS
    return pl.pallas_call(
        functools.partial(_mla_attn_kernel, s=s),
        grid=(MLA_HEADS, b // nsq),
        in_specs=[pl.BlockSpec((nsq, s, hw), lambda h, bb: (bb, 0, h)),
                  pl.BlockSpec((nsq, s, hw), lambda h, bb: (bb, 0, h)),
                  pl.BlockSpec((nsq, s, LANES), lambda h, bb: (bb, 0, h))],
        out_specs=pl.BlockSpec((nsq, s, LANES), lambda h, bb: (bb, 0, h)),
        out_shape=jax.ShapeDtypeStruct((b, s, MLA_HEADS * LANES), BF16),
        scratch_shapes=[pltpu.VMEM((nsq, hw, s), BF16),
                        pltpu.VMEM((2, TQ, s), F32), pltpu.VMEM((2, TQ, s), BF16),
                        pltpu.VMEM((nsq, s, 2 * LANES), BF16)],
        compiler_params=_params(2, VMEM_LIMIT),
        name="mla_attn",
    )(q, k, v)


def _rope_tables(s):
    half = MLA_ROPE_DIM // 2
    freq = ROPE_THETA ** (-jnp.arange(half, dtype=F32) / half)
    ang = jnp.arange(s, dtype=F32)[:, None] * freq[None, :]
    cos, sin = jnp.cos(ang), jnp.sin(ang)
    cos2 = jnp.concatenate([cos, cos], axis=1)
    sin2 = jnp.concatenate([-sin, sin], axis=1)
    z64 = jnp.zeros((s, MLA_ROPE_DIM), F32)
    cq = jnp.concatenate([jnp.ones((s, MLA_NOPE_DIM), F32), cos2, z64], axis=1)
    sq = jnp.concatenate([jnp.zeros((s, MLA_NOPE_DIM), F32), sin2, z64], axis=1)
    ck = jnp.concatenate([cos2, z64], axis=1)
    sk = jnp.concatenate([sin2, z64], axis=1)
    return cq, sq, ck, sk


def _swap_halves(w):
    half = w.shape[-1] // 2
    return jnp.concatenate([w[..., half:], w[..., :half]], axis=-1)


def _mla_q_weights(w_uq):
    r = w_uq.shape[0]
    w = w_uq.reshape(r, MLA_HEADS, MLA_NOPE_DIM + MLA_ROPE_DIM)
    nope, rope = w[..., :MLA_NOPE_DIM], w[..., MLA_NOPE_DIM:]
    z64 = jnp.zeros((r, MLA_HEADS, MLA_ROPE_DIM), w.dtype)
    z128 = jnp.zeros((r, MLA_HEADS, MLA_NOPE_DIM), w.dtype)
    wa = jnp.concatenate([nope, rope, z64], axis=-1).reshape(r, -1)
    wb = jnp.concatenate([z128, _swap_halves(rope), z64], axis=-1).reshape(r, -1)
    return wa.astype(BF16), wb.astype(BF16)


def kernel(x, rel_bias, even_norm1, even_w_in, diff_lambda, diff_subln, even_w_out,
           odd_norm1, odd_w_in, mla_q_norm, mla_w_uq, mla_kv_norm, mla_w_ukv, odd_w_out,
           ffn_norm, ffn_w_in, ffn_conv_w, ffn_conv_b, ffn_w_out, final_norm):
    b, s, d = x.shape
    m = b * s
    bias_vecs = bias_vectors(rel_bias, s)
    rope_tabs = _rope_tables(s)
    qscale = HEAD_DIM ** -0.5 * LOG2E
    even_w_out_bf16 = even_w_out.astype(BF16)
    odd_w_out_bf16 = odd_w_out.astype(BF16)
    h = x.reshape(m, d)
    for layer in range(DEPTH):
        li = layer // 2
        if layer % 2 == 0:
            lam_init = 0.8 - 0.6 * math.exp(-0.3 * layer)
            mw, dw = MOBA_HEADS * HEAD_DIM, DIFF_HEADS * 2 * HEAD_DIM
            colscale = np.ones((3 * mw + 3 * dw,), np.float32)
            colscale[:mw] = qscale
            colscale[3 * mw:3 * mw + dw] = qscale
            w = (even_w_in[li] * jnp.asarray(colscale)).astype(BF16)
            qkv, w_ug = in_proj(h, even_norm1[li], w, [w.shape[1]], [BF16], ffn_w_in, layer)
            qkv = qkv.reshape(b, s, -1)
            nb = mw // LANES
            o_a = attention(qkv, 0, nb, 2 * nb, nb, bias_vecs, 0, "pair", moba=True)
            o_b = attention(qkv, 3 * nb, 4 * nb, 5 * nb, DIFF_HEADS, bias_vecs, MOBA_HEADS, "diff",
                            lam_params=diff_lambda[li], subln_g=diff_subln[li], lam_init=lam_init)
            wo = even_w_out_bf16
        else:
            w = odd_w_in[li]
            dw = DIL_HEADS * HEAD_DIM
            lat0 = 3 * dw
            colscale = np.ones((w.shape[1],), np.float32)
            colscale[:dw] = qscale
            kr_cols = w[:, lat0 + MLA_Q_RANK + MLA_KV_RANK:]
            w_all = jnp.concatenate([w * jnp.asarray(colscale), _swap_halves(kr_cols)],
                                    axis=1).astype(BF16)
            qkv, latent, w_ug = in_proj(h, odd_norm1[li], w_all, [lat0, w_all.shape[1] - lat0],
                                        [BF16, F32], ffn_w_in, layer)
            qkv = qkv.reshape(b, s, -1)
            nb = dw // LANES
            o_a = attention(qkv, 0, nb, 2 * nb, nb, bias_vecs, MOBA_HEADS + DIFF_HEADS, "pair")
            wa, wb = _mla_q_weights(mla_w_uq[li])
            wkv = mla_w_ukv[li].reshape(MLA_KV_RANK, MLA_HEADS, MLA_NOPE_DIM + MLA_V_DIM)
            wk = wkv[..., :MLA_NOPE_DIM].reshape(MLA_KV_RANK, -1).astype(BF16)
            wv = wkv[..., MLA_NOPE_DIM:].reshape(MLA_KV_RANK, -1).astype(BF16)
            q_m, k_m, v_m = mla_up(latent, mla_q_norm[li], mla_kv_norm[li], wa, wb, wk, wv,
                                   rope_tabs, s)
            o_b = mla_attention(q_m.reshape(b, s, -1), k_m.reshape(b, s, -1),
                                v_m.reshape(b, s, -1))
            wo = odd_w_out_bf16
        act, h3 = attn_out_ffn_in(o_a, o_b, wo, li, h.reshape(b, s, d), ffn_norm[layer],
                                  w_ug, ffn_conv_w[layer], ffn_conv_b[layer])
        h = proj_residual([act.reshape(m, D_FF)], ffn_w_out, layer, h3.reshape(m, d),
                          final_g=final_norm if layer == DEPTH - 1 else None)
    return h.reshape(b, s, d)
```

```python
import functools
import math

import numpy as np
import jax
import jax.numpy as jnp
from jax import lax
from jax.experimental import pallas as pl
from jax.experimental.pallas import tpu as pltpu

DEPTH = 4
HEAD_DIM = 64
MOBA_HEADS = 8
MOBA_BLOCK = 256
MOBA_TOPK = 3
DIFF_HEADS = 4
DIL_HEADS = 8
DIL_CONFIGS = ((128, 1), (512, 4), (2048, 16))
MLA_HEADS = 4
MLA_Q_RANK = 256
MLA_KV_RANK = 128
MLA_NOPE_DIM = 128
MLA_ROPE_DIM = 64
MLA_V_DIM = 128
ROPE_THETA = 10000.0
REL_BUCKETS = 32
REL_MAX_DIST = 1024
D_FF = 2816
EPS = 1e-6
NEG = -1e30
LOG2E = math.log2(math.e)

LANES = 128
TQ = MOBA_BLOCK
TK = 256
ATTN_SEQS = 2
FF_STEP_ROWS = 1024
FF_CHUNK = 256
FF_ROWS = 512
VMEM_LIMIT = 56 * 1024 * 1024

F32 = jnp.float32
BF16 = jnp.bfloat16


def _params(n_axes, vmem=None):
    return pltpu.CompilerParams(dimension_semantics=("arbitrary",) * n_axes,
                                vmem_limit_bytes=vmem)


def _rms(x, g):
    ms = jnp.mean(x * x, axis=-1, keepdims=True)
    return x * lax.rsqrt(ms + EPS) * g


def _in_proj_kernel(x_ref, g_ref, w_ref, cast_ref, *o_refs, row_chunk):
    *o_refs, cast_out_ref = o_refs
    cast_out_ref[...] = cast_ref[...].astype(BF16)
    tm = x_ref.shape[0]
    g = g_ref[...]
    for r in range(tm // row_chunk):
        rows = slice(r * row_chunk, (r + 1) * row_chunk)
        xn = _rms(x_ref[rows, :], g).astype(BF16)
        acc = jnp.dot(xn, w_ref[...], preferred_element_type=F32)
        col = 0
        for o_ref in o_refs:
            width = o_ref.shape[1]
            o_ref[rows, :] = acc[:, col:col + width].astype(o_ref.dtype)
            col += width


def in_proj(x, g, w, out_widths, out_dtypes, cast_stack, cast_layer, tm=1024, row_chunk=512):
    m, d = x.shape
    steps = m // tm
    rows, cols = cast_stack.shape[1:]
    slab = rows // steps
    return pl.pallas_call(
        functools.partial(_in_proj_kernel, row_chunk=row_chunk),
        grid=(steps,),
        in_specs=[pl.BlockSpec((tm, d), lambda i: (i, 0)),
                  pl.BlockSpec((1, d), lambda i: (0, 0)),
                  pl.BlockSpec(w.shape, lambda i: (0, 0)),
                  pl.BlockSpec((None, slab, cols), lambda i: (cast_layer, i, 0))],
        out_specs=[pl.BlockSpec((tm, n), lambda i: (i, 0)) for n in out_widths]
        + [pl.BlockSpec((slab, cols), lambda i: (i, 0))],
        out_shape=[jax.ShapeDtypeStruct((m, n), dt) for n, dt in zip(out_widths, out_dtypes)]
        + [jax.ShapeDtypeStruct((rows, cols), BF16)],
        compiler_params=_params(1, VMEM_LIMIT),
        name="in_proj",
    )(x, g.reshape(1, d), w, cast_stack)


def _proj_res_kernel(*refs, n_in, final_norm, row_chunk):
    a_refs = refs[:n_in]
    w_ref, res_ref = refs[n_in], refs[n_in + 1]
    o_ref, wb_ref = refs[-2], refs[-1]

    @pl.when(pl.program_id(0) == 0)
    def _():
        wb_ref[...] = w_ref[...].astype(BF16)

    for r in range(res_ref.shape[0] // row_chunk):
        rows = slice(r * row_chunk, (r + 1) * row_chunk)
        acc = res_ref[rows, :]
        row = 0
        for a_ref in a_refs:
            k = a_ref.shape[1]
            acc = acc + jnp.dot(a_ref[rows, :], wb_ref[row:row + k, :],
                                preferred_element_type=F32)
            row += k
        if final_norm:
            acc = _rms(acc, refs[n_in + 2][...])
        o_ref[rows, :] = acc


def proj_residual(acts, w_stack, layer, res, final_g=None, tm=1024, row_chunk=512):
    m, d = res.shape
    n_in = len(acts)
    in_specs = [pl.BlockSpec((tm, a.shape[1]), lambda i: (i, 0)) for a in acts]
    in_specs += [pl.BlockSpec((None,) + w_stack.shape[1:], lambda i: (layer, 0, 0),
                              pipeline_mode=pl.Buffered(1)),
                 pl.BlockSpec((tm, d), lambda i: (i, 0))]
    args = list(acts) + [w_stack, res]
    if final_g is not None:
        in_specs += [pl.BlockSpec((1, d), lambda i: (0, 0))]
        args += [final_g.reshape(1, d)]
    return pl.pallas_call(
        functools.partial(_proj_res_kernel, n_in=n_in, final_norm=final_g is not None,
                          row_chunk=row_chunk),
        grid=(m // tm,),
        in_specs=in_specs,
        out_specs=pl.BlockSpec((tm, d), lambda i: (i, 0)),
        out_shape=jax.ShapeDtypeStruct((m, d), F32),
        scratch_shapes=[pltpu.VMEM(w_stack.shape[1:], BF16)],
        compiler_params=_params(1, VMEM_LIMIT),
        name="proj_residual",
    )(*args)


def _ffn_in_kernel(oa_ref, ob_ref, wo_ref, res_ref, g_ref, w_ref, cw_ref, cb_ref,
                   act_ref, h_ref, xn_ref, h1_ref, h2_ref):
    rows = res_ref.shape[1]
    ka = oa_ref.shape[2]
    first = pl.program_id(1) == 0
    for r in range(rows // FF_ROWS):
        rs = slice(r * FF_ROWS, (r + 1) * FF_ROWS)
        hh = (res_ref[0, rs, :]
              + jnp.dot(oa_ref[0, rs, :], wo_ref[:ka, :], preferred_element_type=F32)
              + jnp.dot(ob_ref[0, rs, :], wo_ref[ka:, :], preferred_element_type=F32))
        h_ref[0, rs, :] = hh
        xn_ref[rs, :] = _rms(hh, g_ref[...]).astype(BF16)
    sub = lax.broadcasted_iota(jnp.int32, (8, FF_CHUNK), 0)
    zeros = jnp.zeros((8, FF_CHUNK), F32)
    for c in range(D_FF // FF_CHUNK):
        cols = slice(c * FF_CHUNK, (c + 1) * FF_CHUNK)
        wu = w_ref[:, cols]
        wg = w_ref[:, D_FF + c * FF_CHUNK:D_FF + (c + 1) * FF_CHUNK]
        cw = cw_ref[:, cols]
        cb = cb_ref[:, cols]
        prev1 = jnp.where(first, zeros, h1_ref[c])
        prev2 = jnp.where(first, zeros, h2_ref[c])
        for r in range(rows // FF_ROWS):
            rs = slice(r * FF_ROWS, (r + 1) * FF_ROWS)
            xn = xn_ref[rs, :]
            u = jnp.dot(xn, wu, preferred_element_type=F32)
            gt = jnp.dot(xn, wg, preferred_element_type=F32)
            r1 = pltpu.roll(gt, 1, 0)
            r2 = pltpu.roll(gt, 2, 0)
            g1 = jnp.concatenate([jnp.where(sub >= 1, r1[:8], prev1), r1[8:]], axis=0)
            g2 = jnp.concatenate([jnp.where(sub >= 2, r2[:8], prev2), r2[8:]], axis=0)
            prev1, prev2 = r1[:8], r2[:8]
            z = cw[2:3] * gt + cw[1:2] * g1 + cw[0:1] * g2 + cb
            gelu = 0.5 * z * (1.0 + lax.erf(z * math.sqrt(0.5)))
            act_ref[0, rs, cols] = (gelu * u).astype(BF16)
        h1_ref[c] = prev1
        h2_ref[c] = prev2


def attn_out_ffn_in(o_a, o_b, wo_stack, wo_layer, h3, g, w_ug, conv_w, conv_b):
    b, s, d = h3.shape
    nc = D_FF // FF_CHUNK
    once = dict(pipeline_mode=pl.Buffered(1))
    rows3 = lambda w: pl.BlockSpec((1, FF_STEP_ROWS, w), lambda i, j: (i, j, 0))
    return pl.pallas_call(
        _ffn_in_kernel,
        grid=(b, s // FF_STEP_ROWS),
        in_specs=[rows3(o_a.shape[2]), rows3(o_b.shape[2]),
                  pl.BlockSpec((None,) + wo_stack.shape[1:], lambda i, j: (wo_layer, 0, 0), **once),
                  rows3(d),
                  pl.BlockSpec((1, d), lambda i, j: (0, 0)),
                  pl.BlockSpec((d, 2 * D_FF), lambda i, j: (0, 0), **once),
                  pl.BlockSpec((3, D_FF), lambda i, j: (0, 0)),
                  pl.BlockSpec((1, D_FF), lambda i, j: (0, 0))],
        out_specs=[rows3(D_FF), rows3(d)],
        out_shape=[jax.ShapeDtypeStruct((b, s, D_FF), BF16),
                   jax.ShapeDtypeStruct((b, s, d), F32)],
        scratch_shapes=[pltpu.VMEM((FF_STEP_ROWS, d), BF16),
                        pltpu.VMEM((nc, 8, FF_CHUNK), F32), pltpu.VMEM((nc, 8, FF_CHUNK), F32)],
        compiler_params=_params(2, VMEM_LIMIT),
        name="attn_out_ffn_in",
    )(o_a, o_b, wo_stack, h3, g.reshape(1, d), w_ug, conv_w, conv_b.reshape(1, D_FF))


def _bucket_of_distance(s):
    max_exact = REL_BUCKETS // 2
    n_large = REL_BUCKETS - max_exact
    thresholds = []
    for k in range(1, n_large):
        t = max_exact * (REL_MAX_DIST / max_exact) ** (k / n_large)
        ti = int(round(t))
        thresholds.append(ti if abs(t - ti) < 1e-9 else int(math.ceil(t)))
    d = np.arange(s)
    large = max_exact + sum((d >= t).astype(np.int64) for t in thresholds)
    return np.where(d < max_exact, d, np.minimum(large, REL_BUCKETS - 1)).astype(np.int32)


def _dilated_log_multiplicity(s):
    d = np.arange(s)
    count = np.zeros(s, np.int64)
    for window, dil in DIL_CONFIGS:
        count += ((d % dil == 0) & (d // dil <= window // dil)).astype(np.int64)
    return np.where(count > 0, np.log(np.maximum(count, 1)), NEG).astype(np.float32)


def _fill_bias_strip(strip_ref, vec_ref, n, s):
    for h in range(n):
        tile = jnp.broadcast_to(vec_ref[h], (TQ, s + TQ))
        tile = pltpu.roll(tile, 0, 1, stride=1, stride_axis=0)
        strip_ref[h] = tile[:, TQ:]


def bias_vectors(rel_bias, s):
    bucket = _bucket_of_distance(s)
    per_dist = jnp.take(rel_bias.T.astype(F32), jnp.asarray(bucket), axis=1)
    logmult = jnp.asarray(_dilated_log_multiplicity(s))
    dil0 = MOBA_HEADS + DIFF_HEADS
    covered = logmult > 0.5 * NEG
    dil_rows = jnp.where(covered[None, :], per_dist[dil0:] + logmult[None, :], NEG)
    per_dist = jnp.concatenate([per_dist[:dil0], dil_rows], axis=0) * LOG2E
    nh = per_dist.shape[0]
    w = jnp.concatenate([jnp.full((nh, 1), NEG, F32), per_dist[:, ::-1],
                         jnp.full((nh, TQ - 1), NEG, F32)], axis=1)
    return w.reshape(nh, 1, s + TQ)


def _store_transposed(kt_ref, idx, k):
    s, width = k.shape
    for j in range(s // TK):
        kt_ref[idx, 0:width, j * TK:(j + 1) * TK] = k[j * TK:(j + 1) * TK, :].T


def _block_order(n):
    return list(range(1, n, 2)) + list(range((n - 1) // 2 * 2, -1, -2))


def _attn_kernel(*refs, mode, moba, lam_init, s):
    if mode == "diff":
        (q_ref, k_ref, v_ref, bvec_ref, lam_ref, sg_ref, o_ref,
         kt_ref, sc_ref, p_ref, vone_ref, b_ref) = refs
    elif moba:
        (q_ref, k_ref, v_ref, bvec_ref, o_ref,
         kt_ref, sc_ref, p_ref, vone_ref, b_ref, drop_ref) = refs
    else:
        q_ref, k_ref, v_ref, bvec_ref, o_ref, kt_ref, sc_ref, p_ref, vone_ref, b_ref = refs
    n_seq = q_ref.shape[0]

    @pl.when(pl.program_id(1) == 0)
    def _():
        _fill_bias_strip(b_ref, bvec_ref, 2 if mode == "pair" else 1, s)

    nblk = s // TK
    lane = lax.broadcasted_iota(jnp.int32, (TQ, LANES), 1)
    halves = (lane < HEAD_DIM, lane >= HEAD_DIM)
    nt = (((1,), (1,)), ((), ()))

    for b in range(n_seq):
        vone_ref[b, :, :LANES] = v_ref[b]
        vone_ref[b, :, LANES:] = jnp.ones((s, LANES), BF16)
        _store_transposed(kt_ref, b, k_ref[b])
        if not moba:
            continue
        k_all = k_ref[b]
        q_all = q_ref[b]
        sel_row = lax.broadcasted_iota(jnp.int32, (LANES, s), 0)
        key_blk = lax.broadcasted_iota(jnp.int32, (LANES, s), 1) // TK
        kt_ref[b, LANES:, :] = jnp.where((sel_row < 2 * nblk) & (sel_row % nblk == key_blk),
                                         1.0, 0.0).astype(BF16)
        lane_s = lax.broadcasted_iota(jnp.int32, (s, LANES), 1)
        kmean = jnp.mean(k_all.astype(F32).reshape(nblk, TK, LANES), axis=1)
        k_hi = kmean.astype(BF16)
        rem = kmean - k_hi.astype(F32)
        k_mid = rem.astype(BF16)
        k_lo = (rem - k_mid.astype(F32)).astype(BF16)
        kmean3 = jnp.concatenate([k_hi, k_mid, k_lo], axis=1)
        qblk = lax.broadcasted_iota(jnp.int32, (nblk, s), 1) // TQ
        blk = lax.broadcasted_iota(jnp.int32, (nblk, s), 0)
        drop_ref[b] = jnp.zeros((LANES, s), F32)
        for c in range(2):
            in_half = (lane_s < HEAD_DIM) if c == 0 else (lane_s >= HEAD_DIM)
            qh = jnp.where(in_half, q_all, jnp.zeros_like(q_all))
            gate = lax.dot_general(kmean3, jnp.concatenate([qh, qh, qh], axis=1), nt,
                                   preferred_element_type=F32)
            rank = jnp.zeros((nblk, s), jnp.int32)
            for n in range(nblk - 1):
                gn = gate[n:n + 1, :]
                beats = jnp.where(gn > gate, 1, jnp.where(gn == gate, jnp.where(n < blk, 1, 0), 0))
                rank = rank + jnp.where(n < qblk, beats, 0)
            drop = jnp.where(blk < qblk, jnp.where(rank >= MOBA_TOPK, NEG, 0.0), 0.0)
            drop_ref[b, c * nblk:(c + 1) * nblk, :] = drop

    if mode == "diff":
        lp = lam_ref[...]
        lam = (jnp.exp(jnp.sum(lp[0:1] * lp[1:2], axis=1, keepdims=True))
               - jnp.exp(jnp.sum(lp[2:3] * lp[3:4], axis=1, keepdims=True)) + lam_init)

    units = [(b, i, c) for b in range(n_seq) for i in _block_order(nblk) for c in range(2)]
    sel_cache = {}

    def scores(u):
        b, i, c = units[u]
        nk = (i + 1) * TK
        q = q_ref[b, i * TQ:(i + 1) * TQ, :]
        qc = jnp.where(halves[c], q, jnp.zeros_like(q))
        if moba:
            if (b, i) not in sel_cache:
                sel_cache[b, i] = drop_ref[b, :, i * TQ:(i + 1) * TQ].T.astype(BF16)
            mine = (lane >= c * nblk) & (lane < (c + 1) * nblk)
            sel = jnp.where(mine, sel_cache[b, i], jnp.zeros_like(q))
            qc = jnp.concatenate([qc, sel], axis=1)
        sc_ref[u % 2, :, 0:nk] = (jnp.dot(qc, kt_ref[b, :, 0:nk], preferred_element_type=F32)
                                  + b_ref[c if mode == "pair" else 0, :, s - nk:])

    def softmax(u):
        nk = (units[u][1] + 1) * TK
        m = jnp.max(sc_ref[u % 2, :, 0:nk], axis=1, keepdims=True)
        p = jnp.exp2(sc_ref[u % 2, :, 0:nk] - m)
        p_ref[u % 2, :, 0:nk] = p.astype(BF16)

    outs = []

    def weighted_values(u):
        b, i, c = units[u]
        nk = (i + 1) * TK
        o = jnp.dot(p_ref[u % 2, :, 0:nk], vone_ref[b, 0:nk, :], preferred_element_type=F32)
        outs.append(o[:, :LANES] / o[:, LANES:])
        if c == 1:
            if mode == "pair":
                res = jnp.where(halves[0], outs[0], outs[1])
            else:
                res = _rms(outs[0] - lam * outs[1], sg_ref[...]) * (1.0 - lam_init)
            o_ref[b, i * TQ:(i + 1) * TQ, :] = res.astype(o_ref.dtype)
            outs.clear()

    scores(0)
    for u in range(len(units)):
        if u + 1 < len(units):
            scores(u + 1)
        softmax(u)
        if u > 0:
            weighted_values(u - 1)
    weighted_values(len(units) - 1)


def attention(qkv, col_q, col_k, col_v, n_blocks, bias_vecs, head0, mode, moba=False,
              lam_params=None, subln_g=None, lam_init=0.0):
    b, s, _ = qkv.shape
    nb = 2 if mode == "pair" else 1
    nsq = ATTN_SEQS
    seq = lambda col: pl.BlockSpec((nsq, s, LANES), lambda p, bb: (bb, 0, col + p))
    in_specs = [seq(col_q), seq(col_k), seq(col_v),
                pl.BlockSpec((nb, 1, s + TQ), lambda p, bb: (head0 // nb + p, 0, 0))]
    args = [qkv, qkv, qkv, bias_vecs]
    if mode == "diff":
        in_specs += [pl.BlockSpec((4, HEAD_DIM), lambda p, bb: (0, 0)),
                     pl.BlockSpec((1, LANES), lambda p, bb: (0, 0))]
        args += [lam_params.astype(F32), subln_g.reshape(1, LANES)]
    scratch = [pltpu.VMEM((nsq, 2 * LANES if moba else LANES, s), BF16),
               pltpu.VMEM((2, TQ, s), F32), pltpu.VMEM((2, TQ, s), BF16),
               pltpu.VMEM((nsq, s, 2 * LANES), BF16), pltpu.VMEM((nb, TQ, s), F32)]
    if moba:
        scratch += [pltpu.VMEM((nsq, LANES, s), F32)]
    return pl.pallas_call(
        functools.partial(_attn_kernel, mode=mode, moba=moba, lam_init=lam_init, s=s),
        grid=(n_blocks, b // nsq),
        in_specs=in_specs,
        out_specs=pl.BlockSpec((nsq, s, LANES), lambda p, bb: (bb, 0, p)),
        out_shape=jax.ShapeDtypeStruct((b, s, n_blocks * LANES), BF16),
        scratch_shapes=scratch,
        compiler_params=_params(2, VMEM_LIMIT),
        name="attn_" + mode + ("_moba" if moba else ""),
    )(*args)


def _mla_up_kernel(lat_ref, qg_ref, kvg_ref, wa_ref, wb_ref, wk_ref, wv_ref, cq_ref, sq_ref,
                   ck_ref, sk_ref, q_ref, k_ref, v_ref, *, per_seq):
    lat = lat_ref[...]
    tm = lat.shape[0]
    r0 = pl.multiple_of((pl.program_id(0) % per_seq) * tm, tm)
    cqn = _rms(lat[:, :MLA_Q_RANK], qg_ref[...]).astype(BF16)
    ckvn = _rms(lat[:, MLA_Q_RANK:MLA_Q_RANK + MLA_KV_RANK], kvg_ref[...]).astype(BF16)
    qa = jnp.dot(cqn, wa_ref[...], preferred_element_type=F32)
    qb = jnp.dot(cqn, wb_ref[...], preferred_element_type=F32)
    kn = jnp.dot(ckvn, wk_ref[...], preferred_element_type=F32).astype(BF16)
    v_ref[...] = jnp.dot(ckvn, wv_ref[...], preferred_element_type=F32).astype(BF16)
    x = lat[:, MLA_Q_RANK + MLA_KV_RANK:]
    kr = (x * ck_ref[pl.ds(r0, tm), :]
          + pltpu.roll(x, MLA_ROPE_DIM, 1) * sk_ref[pl.ds(r0, tm), :]).astype(BF16)
    cq = cq_ref[pl.ds(r0, tm), :]
    sq = sq_ref[pl.ds(r0, tm), :]
    hw = 2 * LANES
    for h in range(MLA_HEADS):
        q_ref[:, h * hw:(h + 1) * hw] = (qa[:, h * hw:(h + 1) * hw] * cq
                                         + qb[:, h * hw:(h + 1) * hw] * sq).astype(BF16)
        k_ref[:, h * hw:h * hw + LANES] = kn[:, h * LANES:(h + 1) * LANES]
        k_ref[:, h * hw + LANES:(h + 1) * hw] = kr


def mla_up(latent, q_norm, kv_norm, wa, wb, wk, wv, rope_tabs, s, tm=1024):
    m = latent.shape[0]
    cq, sq, ck, sk = rope_tabs
    per_seq = s // tm
    tab = lambda w: pl.BlockSpec((s, w), lambda i: (0, 0))
    full = lambda a: pl.BlockSpec(a.shape, lambda i: (0, 0))
    hw = 2 * LANES
    rows = lambda w: pl.BlockSpec((tm, w), lambda i: (i, 0))
    return pl.pallas_call(
        functools.partial(_mla_up_kernel, per_seq=per_seq),
        grid=(m // tm,),
        in_specs=[rows(latent.shape[1]),
                  pl.BlockSpec((1, MLA_Q_RANK), lambda i: (0, 0)),
                  pl.BlockSpec((1, MLA_KV_RANK), lambda i: (0, 0)),
                  full(wa), full(wb), full(wk), full(wv), tab(hw), tab(hw), tab(LANES), tab(LANES)],
        out_specs=[rows(MLA_HEADS * hw), rows(MLA_HEADS * hw), rows(MLA_HEADS * LANES)],
        out_shape=[jax.ShapeDtypeStruct((m, MLA_HEADS * hw), BF16),
                   jax.ShapeDtypeStruct((m, MLA_HEADS * hw), BF16),
                   jax.ShapeDtypeStruct((m, MLA_HEADS * LANES), BF16)],
        compiler_params=_params(1, VMEM_LIMIT),
        name="mla_up",
    )(latent, q_norm.reshape(1, -1), kv_norm.reshape(1, -1), wa, wb, wk, wv, cq, sq, ck, sk)


def _mla_attn_kernel(q_ref, k_ref, v_ref, o_ref, kt_ref, sc_ref, p_ref, vone_ref, *, s):
    scale = (MLA_NOPE_DIM + MLA_ROPE_DIM) ** -0.5 * LOG2E
    n_seq = q_ref.shape[0]
    for b in range(n_seq):
        _store_transposed(kt_ref, b, k_ref[b])
        vone_ref[b, :, :LANES] = v_ref[b]
        vone_ref[b, :, LANES:] = jnp.ones((s, LANES), BF16)
    row = lax.broadcasted_iota(jnp.int32, (TQ, TK), 0)
    col = lax.broadcasted_iota(jnp.int32, (TQ, TK), 1)
    causal = row >= col
    nblk = s // TQ

    order = [(b, i) for b in range(n_seq) for i in _block_order(nblk)]

    def scores(u):
        b, i = order[u]
        nk = (i + 1) * TK
        q = q_ref[b, i * TQ:(i + 1) * TQ, :]
        sc = jnp.dot(q, kt_ref[b, :, 0:nk], preferred_element_type=F32) * scale
        if i > 0:
            sc_ref[u % 2, :, 0:nk - TK] = sc[:, :nk - TK]
        sc_ref[u % 2, :, nk - TK:nk] = jnp.where(causal, sc[:, nk - TK:], NEG)

    def softmax(u):
        nk = (order[u][1] + 1) * TK
        m = jnp.max(sc_ref[u % 2, :, 0:nk], axis=1, keepdims=True)
        p = jnp.exp2(sc_ref[u % 2, :, 0:nk] - m)
        p_ref[u % 2, :, 0:nk] = p.astype(BF16)

    def weighted_values(u):
        b, i = order[u]
        nk = (i + 1) * TK
        o = jnp.dot(p_ref[u % 2, :, 0:nk], vone_ref[b, 0:nk, :], preferred_element_type=F32)
        o_ref[b, i * TQ:(i + 1) * TQ, :] = (o[:, :LANES] / o[:, LANES:]).astype(o_ref.dtype)

    scores(0)
    for u in range(len(order)):
        if u + 1 < len(order):
            scores(u + 1)
        softmax(u)
        if u > 0:
            weighted_values(u - 1)
    weighted_values(len(order) - 1)


def mla_attention(q, k, v):
    b, s, _ = q.shape
    hw = 2 * LANES
    nsq = ATTN_SEQS
    return pl.pallas_call(
        functools.partial(_mla_attn_kernel, s=s),
        grid=(MLA_HEADS, b // nsq),
        in_specs=[pl.BlockSpec((nsq, s, hw), lambda h, bb: (bb, 0, h)),
                  pl.BlockSpec((nsq, s, hw), lambda h, bb: (bb, 0, h)),
                  pl.BlockSpec((nsq, s, LANES), lambda h, bb: (bb, 0, h))],
        out_specs=pl.BlockSpec((nsq, s, LANES), lambda h, bb: (bb, 0, h)),
        out_shape=jax.ShapeDtypeStruct((b, s, MLA_HEADS * LANES), BF16),
        scratch_shapes=[pltpu.VMEM((nsq, hw, s), BF16),
                        pltpu.VMEM((2, TQ, s), F32), pltpu.VMEM((2, TQ, s), BF16),
                        pltpu.VMEM((nsq, s, 2 * LANES), BF16)],
        compiler_params=_params(2, VMEM_LIMIT),
        name="mla_attn",
    )(q, k, v)


def _rope_tables(s):
    half = MLA_ROPE_DIM // 2
    freq = ROPE_THETA ** (-jnp.arange(half, dtype=F32) / half)
    ang = jnp.arange(s, dtype=F32)[:, None] * freq[None, :]
    cos, sin = jnp.cos(ang), jnp.sin(ang)
    cos2 = jnp.concatenate([cos, cos], axis=1)
    sin2 = jnp.concatenate([-sin, sin], axis=1)
    z64 = jnp.zeros((s, MLA_ROPE_DIM), F32)
    cq = jnp.concatenate([jnp.ones((s, MLA_NOPE_DIM), F32), cos2, z64], axis=1)
    sq = jnp.concatenate([jnp.zeros((s, MLA_NOPE_DIM), F32), sin2, z64], axis=1)
    ck = jnp.concatenate([cos2, z64], axis=1)
    sk = jnp.concatenate([sin2, z64], axis=1)
    return cq, sq, ck, sk


def _swap_halves(w):
    half = w.shape[-1] // 2
    return jnp.concatenate([w[..., half:], w[..., :half]], axis=-1)


def _mla_q_weights(w_uq):
    r = w_uq.shape[0]
    w = w_uq.reshape(r, MLA_HEADS, MLA_NOPE_DIM + MLA_ROPE_DIM)
    nope, rope = w[..., :MLA_NOPE_DIM], w[..., MLA_NOPE_DIM:]
    z64 = jnp.zeros((r, MLA_HEADS, MLA_ROPE_DIM), w.dtype)
    z128 = jnp.zeros((r, MLA_HEADS, MLA_NOPE_DIM), w.dtype)
    wa = jnp.concatenate([nope, rope, z64], axis=-1).reshape(r, -1)
    wb = jnp.concatenate([z128, _swap_halves(rope), z64], axis=-1).reshape(r, -1)
    return wa.astype(BF16), wb.astype(BF16)


def kernel(x, rel_bias, even_norm1, even_w_in, diff_lambda, diff_subln, even_w_out,
           odd_norm1, odd_w_in, mla_q_norm, mla_w_uq, mla_kv_norm, mla_w_ukv, odd_w_out,
           ffn_norm, ffn_w_in, ffn_conv_w, ffn_conv_b, ffn_w_out, final_norm):
    b, s, d = x.shape
    m = b * s
    bias_vecs = bias_vectors(rel_bias, s)
    rope_tabs = _rope_tables(s)
    qscale = HEAD_DIM ** -0.5 * LOG2E
    even_w_out_bf16 = even_w_out.astype(BF16)
    odd_w_out_bf16 = odd_w_out.astype(BF16)
    h = x.reshape(m, d)
    for layer in range(DEPTH):
        li = layer // 2
        if layer % 2 == 0:
            lam_init = 0.8 - 0.6 * math.exp(-0.3 * layer)
            mw, dw = MOBA_HEADS * HEAD_DIM, DIFF_HEADS * 2 * HEAD_DIM
            colscale = np.ones((3 * mw + 3 * dw,), np.float32)
            colscale[:mw] = qscale
            colscale[3 * mw:3 * mw + dw] = qscale
            w = (even_w_in[li] * jnp.asarray(colscale)).astype(BF16)
            qkv, w_ug = in_proj(h, even_norm1[li], w, [w.shape[1]], [BF16], ffn_w_in, layer)
            qkv = qkv.reshape(b, s, -1)
            nb = mw // LANES
            o_a = attention(qkv, 0, nb, 2 * nb, nb, bias_vecs, 0, "pair", moba=True)
            o_b = attention(qkv, 3 * nb, 4 * nb, 5 * nb, DIFF_HEADS, bias_vecs, MOBA_HEADS, "diff",
                            lam_params=diff_lambda[li], subln_g=diff_subln[li], lam_init=lam_init)
            wo = even_w_out_bf16
        else:
            w = odd_w_in[li]
            dw = DIL_HEADS * HEAD_DIM
            lat0 = 3 * dw
            colscale = np.ones((w.shape[1],), np.float32)
            colscale[:dw] = qscale
            kr_cols = w[:, lat0 + MLA_Q_RANK + MLA_KV_RANK:]
            w_all = jnp.concatenate([w * jnp.asarray(colscale), _swap_halves(kr_cols)],
                                    axis=1).astype(BF16)
            qkv, latent, w_ug = in_proj(h, odd_norm1[li], w_all, [lat0, w_all.shape[1] - lat0],
                                        [BF16, F32], ffn_w_in, layer)
            qkv = qkv.reshape(b, s, -1)
            nb = dw // LANES
            o_a = attention(qkv, 0, nb, 2 * nb, nb, bias_vecs, MOBA_HEADS + DIFF_HEADS, "pair")
            wa, wb = _mla_q_weights(mla_w_uq[li])
            wkv = mla_w_ukv[li].reshape(MLA_KV_RANK, MLA_HEADS, MLA_NOPE_DIM + MLA_V_DIM)
            wk = wkv[..., :MLA_NOPE_DIM].reshape(MLA_KV_RANK, -1).astype(BF16)
            wv = wkv[..., MLA_NOPE_DIM:].reshape(MLA_KV_RANK, -1).astype(BF16)
            q_m, k_m, v_m = mla_up(latent, mla_q_norm[li], mla_kv_norm[li], wa, wb, wk, wv,
                                   rope_tabs, s)
            o_b = mla_attention(q_m.reshape(b, s, -1), k_m.reshape(b, s, -1),
                                v_m.reshape(b, s, -1))
            wo = odd_w_out_bf16
        act, h3 = attn_out_ffn_in(o_a, o_b, wo, li, h.reshape(b, s, d), ffn_norm[layer],
                                  w_ug, ffn_conv_w[layer], ffn_conv_b[layer])
        h = proj_residual([act.reshape(m, D_FF)], ffn_w_out, layer, h3.reshape(m, d),
                          final_g=final_norm if layer == DEPTH - 1 else None)
    return h.reshape(b, s, d)
```

```python
import functools
import math

import numpy as np
import jax
import jax.numpy as jnp
from jax import lax
from jax.experimental import pallas as pl
from jax.experimental.pallas import tpu as pltpu

DEPTH = 4
HEAD_DIM = 64
MOBA_HEADS = 8
MOBA_BLOCK = 256
MOBA_TOPK = 3
DIFF_HEADS = 4
DIL_HEADS = 8
DIL_CONFIGS = ((128, 1), (512, 4), (2048, 16))
MLA_HEADS = 4
MLA_Q_RANK = 256
MLA_KV_RANK = 128
MLA_NOPE_DIM = 128
MLA_ROPE_DIM = 64
MLA_V_DIM = 128
ROPE_THETA = 10000.0
REL_BUCKETS = 32
REL_MAX_DIST = 1024
D_FF = 2816
EPS = 1e-6
NEG = -1e30
LOG2E = math.log2(math.e)

LANES = 128
TQ = MOBA_BLOCK
TK = 256
ATTN_SEQS = 2
FF_STEP_ROWS = 1024
FF_CHUNK = 256
FF_ROWS = 512
VMEM_LIMIT = 56 * 1024 * 1024

F32 = jnp.float32
BF16 = jnp.bfloat16


def _params(n_axes, vmem=None):
    return pltpu.CompilerParams(dimension_semantics=("arbitrary",) * n_axes,
                                vmem_limit_bytes=vmem)


def _rms(x, g):
    ms = jnp.mean(x * x, axis=-1, keepdims=True)
    return x * lax.rsqrt(ms + EPS) * g


def _in_proj_kernel(x_ref, g_ref, w_ref, cast_ref, *o_refs, row_chunk):
    *o_refs, cast_out_ref = o_refs
    cast_out_ref[...] = cast_ref[...].astype(BF16)
    tm = x_ref.shape[0]
    g = g_ref[...]
    for r in range(tm // row_chunk):
        rows = slice(r * row_chunk, (r + 1) * row_chunk)
        xn = _rms(x_ref[rows, :], g).astype(BF16)
        acc = jnp.dot(xn, w_ref[...], preferred_element_type=F32)
        col = 0
        for o_ref in o_refs:
            width = o_ref.shape[1]
            o_ref[rows, :] = acc[:, col:col + width].astype(o_ref.dtype)
            col += width


def in_proj(x, g, w, out_widths, out_dtypes, cast_stack, cast_layer, tm=1024, row_chunk=512):
    m, d = x.shape
    steps = m // tm
    rows, cols = cast_stack.shape[1:]
    slab = rows // steps
    return pl.pallas_call(
        functools.partial(_in_proj_kernel, row_chunk=row_chunk),
        grid=(steps,),
        in_specs=[pl.BlockSpec((tm, d), lambda i: (i, 0)),
                  pl.BlockSpec((1, d), lambda i: (0, 0)),
                  pl.BlockSpec(w.shape, lambda i: (0, 0)),
                  pl.BlockSpec((None, slab, cols), lambda i: (cast_layer, i, 0))],
        out_specs=[pl.BlockSpec((tm, n), lambda i: (i, 0)) for n in out_widths]
        + [pl.BlockSpec((slab, cols), lambda i: (i, 0))],
        out_shape=[jax.ShapeDtypeStruct((m, n), dt) for n, dt in zip(out_widths, out_dtypes)]
        + [jax.ShapeDtypeStruct((rows, cols), BF16)],
        compiler_params=_params(1, VMEM_LIMIT),
        name="in_proj",
    )(x, g.reshape(1, d), w, cast_stack)


def _proj_res_kernel(*refs, n_in, final_norm, row_chunk):
    a_refs = refs[:n_in]
    w_ref, res_ref = refs[n_in], refs[n_in + 1]
    o_ref, wb_ref = refs[-2], refs[-1]

    @pl.when(pl.program_id(0) == 0)
    def _():
        wb_ref[...] = w_ref[...].astype(BF16)

    for r in range(res_ref.shape[0] // row_chunk):
        rows = slice(r * row_chunk, (r + 1) * row_chunk)
        acc = res_ref[rows, :]
        row = 0
        for a_ref in a_refs:
            k = a_ref.shape[1]
            acc = acc + jnp.dot(a_ref[rows, :], wb_ref[row:row + k, :],
                                preferred_element_type=F32)
            row += k
        if final_norm:
            acc = _rms(acc, refs[n_in + 2][...])
        o_ref[rows, :] = acc


def proj_residual(acts, w_stack, layer, res, final_g=None, tm=1024, row_chunk=512):
    m, d = res.shape
    n_in = len(acts)
    in_specs = [pl.BlockSpec((tm, a.shape[1]), lambda i: (i, 0)) for a in acts]
    in_specs += [pl.BlockSpec((None,) + w_stack.shape[1:], lambda i: (layer, 0, 0),
                              pipeline_mode=pl.Buffered(1)),
                 pl.BlockSpec((tm, d), lambda i: (i, 0))]
    args = list(acts) + [w_stack, res]
    if final_g is not None:
        in_specs += [pl.BlockSpec((1, d), lambda i: (0, 0))]
        args += [final_g.reshape(1, d)]
    return pl.pallas_call(
        functools.partial(_proj_res_kernel, n_in=n_in, final_norm=final_g is not None,
                          row_chunk=row_chunk),
        grid=(m // tm,),
        in_specs=in_specs,
        out_specs=pl.BlockSpec((tm, d), lambda i: (i, 0)),
        out_shape=jax.ShapeDtypeStruct((m, d), F32),
        scratch_shapes=[pltpu.VMEM(w_stack.shape[1:], BF16)],
        compiler_params=_params(1, VMEM_LIMIT),
        name="proj_residual",
    )(*args)


def _ffn_in_kernel(oa_ref, ob_ref, wo_ref, res_ref, g_ref, w_ref, cw_ref, cb_ref,
                   act_ref, h_ref, xn_ref, h1_ref, h2_ref):
    rows = res_ref.shape[1]
    ka = oa_ref.shape[2]
    first = pl.program_id(1) == 0
    for r in range(rows // FF_ROWS):
        rs = slice(r * FF_ROWS, (r + 1) * FF_ROWS)
        hh = (res_ref[0, rs, :]
              + jnp.dot(oa_ref[0, rs, :], wo_ref[:ka, :], preferred_element_type=F32)
              + jnp.dot(ob_ref[0, rs, :], wo_ref[ka:, :], preferred_element_type=F32))
        h_ref[0, rs, :] = hh
        xn_ref[rs, :] = _rms(hh, g_ref[...]).astype(BF16)
    sub = lax.broadcasted_iota(jnp.int32, (8, FF_CHUNK), 0)
    zeros = jnp.zeros((8, FF_CHUNK), F32)
    for c in range(D_FF // FF_CHUNK):
        cols = slice(c * FF_CHUNK, (c + 1) * FF_CHUNK)
        wu = w_ref[:, cols]
        wg = w_ref[:, D_FF + c * FF_CHUNK:D_FF + (c + 1) * FF_CHUNK]
        cw = cw_ref[:, cols]
        cb = cb_ref[:, cols]
        prev1 = jnp.where(first, zeros, h1_ref[c])
        prev2 = jnp.where(first, zeros, h2_ref[c])
        for r in range(rows // FF_ROWS):
            rs = slice(r * FF_ROWS, (r + 1) * FF_ROWS)
            xn = xn_ref[rs, :]
            u = jnp.dot(xn, wu, preferred_element_type=F32)
            gt = jnp.dot(xn, wg, preferred_element_type=F32)
            r1 = pltpu.roll(gt, 1, 0)
            r2 = pltpu.roll(gt, 2, 0)
            g1 = jnp.concatenate([jnp.where(sub >= 1, r1[:8], prev1), r1[8:]], axis=0)
            g2 = jnp.concatenate([jnp.where(sub >= 2, r2[:8], prev2), r2[8:]], axis=0)
            prev1, prev2 = r1[:8], r2[:8]
            z = cw[2:3] * gt + cw[1:2] * g1 + cw[0:1] * g2 + cb
            gelu = 0.5 * z * (1.0 + lax.erf(z * math.sqrt(0.5)))
            act_ref[0, rs, cols] = (gelu * u).astype(BF16)
        h1_ref[c] = prev1
        h2_ref[c] = prev2


def attn_out_ffn_in(o_a, o_b, wo_stack, wo_layer, h3, g, w_ug, conv_w, conv_b):
    b, s, d = h3.shape
    nc = D_FF // FF_CHUNK
    once = dict(pipeline_mode=pl.Buffered(1))
    rows3 = lambda w: pl.BlockSpec((1, FF_STEP_ROWS, w), lambda i, j: (i, j, 0))
    return pl.pallas_call(
        _ffn_in_kernel,
        grid=(b, s // FF_STEP_ROWS),
        in_specs=[rows3(o_a.shape[2]), rows3(o_b.shape[2]),
                  pl.BlockSpec((None,) + wo_stack.shape[1:], lambda i, j: (wo_layer, 0, 0), **once),
                  rows3(d),
                  pl.BlockSpec((1, d), lambda i, j: (0, 0)),
                  pl.BlockSpec((d, 2 * D_FF), lambda i, j: (0, 0), **once),
                  pl.BlockSpec((3, D_FF), lambda i, j: (0, 0)),
                  pl.BlockSpec((1, D_FF), lambda i, j: (0, 0))],
        out_specs=[rows3(D_FF), rows3(d)],
        out_shape=[jax.ShapeDtypeStruct((b, s, D_FF), BF16),
                   jax.ShapeDtypeStruct((b, s, d), F32)],
        scratch_shapes=[pltpu.VMEM((FF_STEP_ROWS, d), BF16),
                        pltpu.VMEM((nc, 8, FF_CHUNK), F32), pltpu.VMEM((nc, 8, FF_CHUNK), F32)],
        compiler_params=_params(2, VMEM_LIMIT),
        name="attn_out_ffn_in",
    )(o_a, o_b, wo_stack, h3, g.reshape(1, d), w_ug, conv_w, conv_b.reshape(1, D_FF))


def _bucket_of_distance(s):
    max_exact = REL_BUCKETS // 2
    n_large = REL_BUCKETS - max_exact
    thresholds = []
    for k in range(1, n_large):
        t = max_exact * (REL_MAX_DIST / max_exact) ** (k / n_large)
        ti = int(round(t))
        thresholds.append(ti if abs(t - ti) < 1e-9 else int(math.ceil(t)))
    d = np.arange(s)
    large = max_exact + sum((d >= t).astype(np.int64) for t in thresholds)
    return np.where(d < max_exact, d, np.minimum(large, REL_BUCKETS - 1)).astype(np.int32)


def _dilated_log_multiplicity(s):
    d = np.arange(s)
    count = np.zeros(s, np.int64)
    for window, dil in DIL_CONFIGS:
        count += ((d % dil == 0) & (d // dil <= window // dil)).astype(np.int64)
    return np.where(count > 0, np.log(np.maximum(count, 1)), NEG).astype(np.float32)


def _fill_bias_strip(strip_ref, vec_ref, n, s):
    for h in range(n):
        tile = jnp.broadcast_to(vec_ref[h], (TQ, s + TQ))
        tile = pltpu.roll(tile, 0, 1, stride=1, stride_axis=0)
        strip_ref[h] = tile[:, TQ:]


def bias_vectors(rel_bias, s):
    bucket = _bucket_of_distance(s)
    per_dist = jnp.take(rel_bias.T.astype(F32), jnp.asarray(bucket), axis=1)
    logmult = jnp.asarray(_dilated_log_multiplicity(s))
    dil0 = MOBA_HEADS + DIFF_HEADS
    covered = logmult > 0.5 * NEG
    dil_rows = jnp.where(covered[None, :], per_dist[dil0:] + logmult[None, :], NEG)
    per_dist = jnp.concatenate([per_dist[:dil0], dil_rows], axis=0) * LOG2E
    nh = per_dist.shape[0]
    w = jnp.concatenate([jnp.full((nh, 1), NEG, F32), per_dist[:, ::-1],
                         jnp.full((nh, TQ - 1), NEG, F32)], axis=1)
    return w.reshape(nh, 1, s + TQ)


def _store_transposed(kt_ref, idx, k, row0=0):
    s, width = k.shape
    for j in range(s // TK):
        kt_ref[idx, row0:row0 + width, j * TK:(j + 1) * TK] = k[j * TK:(j + 1) * TK, :].T


def _block_order(n):
    return list(range(1, n, 2)) + list(range((n - 1) // 2 * 2, -1, -2))


def _attn_kernel(*refs, mode, moba, lam_init, s):
    if mode == "diff":
        (q_ref, k_ref, v_ref, bvec_ref, lam_ref, sg_ref, o_ref,
         kt_ref, sc_ref, p_ref, vone_ref, b_ref) = refs
    elif moba:
        (q_ref, k_ref, v_ref, bvec_ref, o_ref,
         kt_ref, sc_ref, p_ref, vone_ref, b_ref, drop_ref) = refs
    else:
        q_ref, k_ref, v_ref, bvec_ref, o_ref, kt_ref, sc_ref, p_ref, vone_ref, b_ref = refs
    n_seq = q_ref.shape[0]

    @pl.when(pl.program_id(1) == 0)
    def _():
        _fill_bias_strip(b_ref, bvec_ref, 2 if mode == "pair" else 1, s)

    nblk = s // TK
    lane = lax.broadcasted_iota(jnp.int32, (TQ, LANES), 1)
    halves = (lane < HEAD_DIM, lane >= HEAD_DIM)
    nt = (((1,), (1,)), ((), ()))

    for b in range(n_seq):
        vone_ref[b, :, :LANES] = v_ref[b]
        vone_ref[b, :, LANES:] = jnp.ones((s, LANES), BF16)
        _store_transposed(kt_ref, b, k_ref[b])
        if not moba:
            continue
        k_all = k_ref[b]
        q_all = q_ref[b]
        sel_row = lax.broadcasted_iota(jnp.int32, (LANES, s), 0)
        key_blk = lax.broadcasted_iota(jnp.int32, (LANES, s), 1) // TK
        kt_ref[b, LANES:, :] = jnp.where((sel_row < 2 * nblk) & (sel_row % nblk == key_blk),
                                         1.0, 0.0).astype(BF16)
        lane_s = lax.broadcasted_iota(jnp.int32, (s, LANES), 1)
        kmean = jnp.mean(k_all.astype(F32).reshape(nblk, TK, LANES), axis=1)
        k_hi = kmean.astype(BF16)
        rem = kmean - k_hi.astype(F32)
        k_mid = rem.astype(BF16)
        k_lo = (rem - k_mid.astype(F32)).astype(BF16)
        kmean3 = jnp.concatenate([k_hi, k_mid, k_lo], axis=1)
        qblk = lax.broadcasted_iota(jnp.int32, (nblk, s), 1) // TQ
        blk = lax.broadcasted_iota(jnp.int32, (nblk, s), 0)
        drop_ref[b] = jnp.zeros((LANES, s), F32)
        for c in range(2):
            in_half = (lane_s < HEAD_DIM) if c == 0 else (lane_s >= HEAD_DIM)
            qh = jnp.where(in_half, q_all, jnp.zeros_like(q_all))
            gate = lax.dot_general(kmean3, jnp.concatenate([qh, qh, qh], axis=1), nt,
                                   preferred_element_type=F32)
            rank = jnp.zeros((nblk, s), jnp.int32)
            for n in range(nblk - 1):
                gn = gate[n:n + 1, :]
                beats = jnp.where(gn > gate, 1, jnp.where(gn == gate, jnp.where(n < blk, 1, 0), 0))
                rank = rank + jnp.where(n < qblk, beats, 0)
            drop = jnp.where(blk < qblk, jnp.where(rank >= MOBA_TOPK, NEG, 0.0), 0.0)
            drop_ref[b, c * nblk:(c + 1) * nblk, :] = drop

    if mode == "diff":
        lp = lam_ref[...]
        lam = (jnp.exp(jnp.sum(lp[0:1] * lp[1:2], axis=1, keepdims=True))
               - jnp.exp(jnp.sum(lp[2:3] * lp[3:4], axis=1, keepdims=True)) + lam_init)

    units = [(b, i, c) for b in range(n_seq) for i in _block_order(nblk) for c in range(2)]
    sel_cache = {}

    def scores(u):
        b, i, c = units[u]
        nk = (i + 1) * TK
        q = q_ref[b, i * TQ:(i + 1) * TQ, :]
        qc = jnp.where(halves[c], q, jnp.zeros_like(q))
        if moba:
            if (b, i) not in sel_cache:
                sel_cache[b, i] = drop_ref[b, :, i * TQ:(i + 1) * TQ].T.astype(BF16)
            mine = (lane >= c * nblk) & (lane < (c + 1) * nblk)
            sel = jnp.where(mine, sel_cache[b, i], jnp.zeros_like(q))
            qc = jnp.concatenate([qc, sel], axis=1)
        sc_ref[u % 2, :, 0:nk] = (jnp.dot(qc, kt_ref[b, :, 0:nk], preferred_element_type=F32)
                                  + b_ref[c if mode == "pair" else 0, :, s - nk:])

    def softmax(u):
        nk = (units[u][1] + 1) * TK
        m = jnp.max(sc_ref[u % 2, :, 0:nk], axis=1, keepdims=True)
        p = jnp.exp2(sc_ref[u % 2, :, 0:nk] - m)
        p_ref[u % 2, :, 0:nk] = p.astype(BF16)

    outs = []

    def weighted_values(u):
        b, i, c = units[u]
        nk = (i + 1) * TK
        o = jnp.dot(p_ref[u % 2, :, 0:nk], vone_ref[b, 0:nk, :], preferred_element_type=F32)
        outs.append(o[:, :LANES] / o[:, LANES:])
        if c == 1:
            if mode == "pair":
                res = jnp.where(halves[0], outs[0], outs[1])
            else:
                res = _rms(outs[0] - lam * outs[1], sg_ref[...]) * (1.0 - lam_init)
            o_ref[b, i * TQ:(i + 1) * TQ, :] = res.astype(o_ref.dtype)
            outs.clear()

    scores(0)
    for u in range(len(units)):
        if u + 1 < len(units):
            scores(u + 1)
        softmax(u)
        if u > 0:
            weighted_values(u - 1)
    weighted_values(len(units) - 1)


def attention(qkv, col_q, col_k, col_v, n_blocks, bias_vecs, head0, mode, moba=False,
              lam_params=None, subln_g=None, lam_init=0.0):
    b, s, _ = qkv.shape
    nb = 2 if mode == "pair" else 1
    nsq = ATTN_SEQS
    seq = lambda col: pl.BlockSpec((nsq, s, LANES), lambda p, bb: (bb, 0, col + p))
    in_specs = [seq(col_q), seq(col_k), seq(col_v),
                pl.BlockSpec((nb, 1, s + TQ), lambda p, bb: (head0 // nb + p, 0, 0))]
    args = [qkv, qkv, qkv, bias_vecs]
    if mode == "diff":
        in_specs += [pl.BlockSpec((4, HEAD_DIM), lambda p, bb: (0, 0)),
                     pl.BlockSpec((1, LANES), lambda p, bb: (0, 0))]
        args += [lam_params.astype(F32), subln_g.reshape(1, LANES)]
    scratch = [pltpu.VMEM((nsq, 2 * LANES if moba else LANES, s), BF16),
               pltpu.VMEM((2, TQ, s), F32), pltpu.VMEM((2, TQ, s), BF16),
               pltpu.VMEM((nsq, s, 2 * LANES), BF16), pltpu.VMEM((nb, TQ, s), F32)]
    if moba:
        scratch += [pltpu.VMEM((nsq, LANES, s), F32)]
    return pl.pallas_call(
        functools.partial(_attn_kernel, mode=mode, moba=moba, lam_init=lam_init, s=s),
        grid=(n_blocks, b // nsq),
        in_specs=in_specs,
        out_specs=pl.BlockSpec((nsq, s, LANES), lambda p, bb: (bb, 0, p)),
        out_shape=jax.ShapeDtypeStruct((b, s, n_blocks * LANES), BF16),
        scratch_shapes=scratch,
        compiler_params=_params(2, VMEM_LIMIT),
        name="attn_" + mode + ("_moba" if moba else ""),
    )(*args)


def _mla_up_kernel(lat_ref, qg_ref, kvg_ref, wa_ref, wb_ref, wk_ref, wv_ref, cq_ref, sq_ref,
                   ck_ref, sk_ref, q_ref, kn_ref, kr_ref, v_ref, *, per_seq):
    lat = lat_ref[...]
    tm = lat.shape[0]
    r0 = pl.multiple_of((pl.program_id(0) % per_seq) * tm, tm)
    cqn = _rms(lat[:, :MLA_Q_RANK], qg_ref[...]).astype(BF16)
    ckvn = _rms(lat[:, MLA_Q_RANK:MLA_Q_RANK + MLA_KV_RANK], kvg_ref[...]).astype(BF16)
    qa = jnp.dot(cqn, wa_ref[...], preferred_element_type=F32)
    qb = jnp.dot(cqn, wb_ref[...], preferred_element_type=F32)
    kn_ref[...] = jnp.dot(ckvn, wk_ref[...], preferred_element_type=F32).astype(BF16)
    v_ref[...] = jnp.dot(ckvn, wv_ref[...], preferred_element_type=F32).astype(BF16)
    x = lat[:, MLA_Q_RANK + MLA_KV_RANK:]
    kr_ref[...] = (x * ck_ref[pl.ds(r0, tm), :]
                   + pltpu.roll(x, MLA_ROPE_DIM, 1) * sk_ref[pl.ds(r0, tm), :]).astype(BF16)
    cq = cq_ref[pl.ds(r0, tm), :]
    sq = sq_ref[pl.ds(r0, tm), :]
    hw = 2 * LANES
    for h in range(MLA_HEADS):
        q_ref[:, h * hw:(h + 1) * hw] = (qa[:, h * hw:(h + 1) * hw] * cq
                                         + qb[:, h * hw:(h + 1) * hw] * sq).astype(BF16)


def mla_up(latent, q_norm, kv_norm, wa, wb, wk, wv, rope_tabs, s, tm=1024):
    m = latent.shape[0]
    cq, sq, ck, sk = rope_tabs
    per_seq = s // tm
    tab = lambda w: pl.BlockSpec((s, w), lambda i: (0, 0))
    full = lambda a: pl.BlockSpec(a.shape, lambda i: (0, 0))
    hw = 2 * LANES
    rows = lambda w: pl.BlockSpec((tm, w), lambda i: (i, 0))
    return pl.pallas_call(
        functools.partial(_mla_up_kernel, per_seq=per_seq),
        grid=(m // tm,),
        in_specs=[rows(latent.shape[1]),
                  pl.BlockSpec((1, MLA_Q_RANK), lambda i: (0, 0)),
                  pl.BlockSpec((1, MLA_KV_RANK), lambda i: (0, 0)),
                  full(wa), full(wb), full(wk), full(wv), tab(hw), tab(hw), tab(LANES), tab(LANES)],
        out_specs=[rows(MLA_HEADS * hw), rows(MLA_HEADS * LANES), rows(LANES),
                   rows(MLA_HEADS * LANES)],
        out_shape=[jax.ShapeDtypeStruct((m, MLA_HEADS * hw), BF16),
                   jax.ShapeDtypeStruct((m, MLA_HEADS * LANES), BF16),
                   jax.ShapeDtypeStruct((m, LANES), BF16),
                   jax.ShapeDtypeStruct((m, MLA_HEADS * LANES), BF16)],
        compiler_params=_params(1, VMEM_LIMIT),
        name="mla_up",
    )(latent, q_norm.reshape(1, -1), kv_norm.reshape(1, -1), wa, wb, wk, wv, cq, sq, ck, sk)


def _mla_attn_kernel(q_ref, kn_ref, kr_ref, v_ref, o_ref, kt_ref, sc_ref, p_ref, vone_ref, *, s):
    scale = (MLA_NOPE_DIM + MLA_ROPE_DIM) ** -0.5 * LOG2E
    n_seq = q_ref.shape[0]
    for b in range(n_seq):
        _store_transposed(kt_ref, b, kn_ref[b])
        _store_transposed(kt_ref, b, kr_ref[b], row0=LANES)
        vone_ref[b, :, :LANES] = v_ref[b]
        vone_ref[b, :, LANES:] = jnp.ones((s, LANES), BF16)
    row = lax.broadcasted_iota(jnp.int32, (TQ, TK), 0)
    col = lax.broadcasted_iota(jnp.int32, (TQ, TK), 1)
    causal = row >= col
    nblk = s // TQ

    order = [(b, i) for b in range(n_seq) for i in _block_order(nblk)]

    def scores(u):
        b, i = order[u]
        nk = (i + 1) * TK
        q = q_ref[b, i * TQ:(i + 1) * TQ, :]
        sc = jnp.dot(q, kt_ref[b, :, 0:nk], preferred_element_type=F32) * scale
        if i > 0:
            sc_ref[u % 2, :, 0:nk - TK] = sc[:, :nk - TK]
        sc_ref[u % 2, :, nk - TK:nk] = jnp.where(causal, sc[:, nk - TK:], NEG)

    def softmax(u):
        nk = (order[u][1] + 1) * TK
        m = jnp.max(sc_ref[u % 2, :, 0:nk], axis=1, keepdims=True)
        p = jnp.exp2(sc_ref[u % 2, :, 0:nk] - m)
        p_ref[u % 2, :, 0:nk] = p.astype(BF16)

    def weighted_values(u):
        b, i = order[u]
        nk = (i + 1) * TK
        o = jnp.dot(p_ref[u % 2, :, 0:nk], vone_ref[b, 0:nk, :], preferred_element_type=F32)
        o_ref[b, i * TQ:(i + 1) * TQ, :] = (o[:, :LANES] / o[:, LANES:]).astype(o_ref.dtype)

    scores(0)
    for u in range(len(order)):
        if u + 1 < len(order):
            scores(u + 1)
        softmax(u)
        if u > 0:
            weighted_values(u - 1)
    weighted_values(len(order) - 1)


def mla_attention(q, kn, kr, v):
    b, s, _ = q.shape
    hw = 2 * LANES
    nsq = ATTN_SEQS
    return pl.pallas_call(
        functools.partial(_mla_attn_kernel, s=s),
        grid=(MLA_HEADS, b // nsq),
        in_specs=[pl.BlockSpec((nsq, s, hw), lambda h, bb: (bb, 0, h)),
                  pl.BlockSpec((nsq, s, LANES), lambda h, bb: (bb, 0, h)),
                  pl.BlockSpec((nsq, s, LANES), lambda h, bb: (bb, 0, 0)),
                  pl.BlockSpec((nsq, s, LANES), lambda h, bb: (bb, 0, h))],
        out_specs=pl.BlockSpec((nsq, s, LANES), lambda h, bb: (bb, 0, h)),
        out_shape=jax.ShapeDtypeStruct((b, s, MLA_HEADS * LANES), BF16),
        scratch_shapes=[pltpu.VMEM((nsq, hw, s), BF16),
                        pltpu.VMEM((2, TQ, s), F32), pltpu.VMEM((2, TQ, s), BF16),
                        pltpu.VMEM((nsq, s, 2 * LANES), BF16)],
        compiler_params=_params(2, VMEM_LIMIT),
        name="mla_attn",
    )(q, kn, kr, v)


def _rope_tables(s):
    half = MLA_ROPE_DIM // 2
    freq = ROPE_THETA ** (-jnp.arange(half, dtype=F32) / half)
    ang = jnp.arange(s, dtype=F32)[:, None] * freq[None, :]
    cos, sin = jnp.cos(ang), jnp.sin(ang)
    cos2 = jnp.concatenate([cos, cos], axis=1)
    sin2 = jnp.concatenate([-sin, sin], axis=1)
    z64 = jnp.zeros((s, MLA_ROPE_DIM), F32)
    cq = jnp.concatenate([jnp.ones((s, MLA_NOPE_DIM), F32), cos2, z64], axis=1)
    sq = jnp.concatenate([jnp.zeros((s, MLA_NOPE_DIM), F32), sin2, z64], axis=1)
    ck = jnp.concatenate([cos2, z64], axis=1)
    sk = jnp.concatenate([sin2, z64], axis=1)
    return cq, sq, ck, sk


def _swap_halves(w):
    half = w.shape[-1] // 2
    return jnp.concatenate([w[..., half:], w[..., :half]], axis=-1)


def _mla_q_weights(w_uq):
    r = w_uq.shape[0]
    w = w_uq.reshape(r, MLA_HEADS, MLA_NOPE_DIM + MLA_ROPE_DIM)
    nope, rope = w[..., :MLA_NOPE_DIM], w[..., MLA_NOPE_DIM:]
    z64 = jnp.zeros((r, MLA_HEADS, MLA_ROPE_DIM), w.dtype)
    z128 = jnp.zeros((r, MLA_HEADS, MLA_NOPE_DIM), w.dtype)
    wa = jnp.concatenate([nope, rope, z64], axis=-1).reshape(r, -1)
    wb = jnp.concatenate([z128, _swap_halves(rope), z64], axis=-1).reshape(r, -1)
    return wa.astype(BF16), wb.astype(BF16)


def kernel(x, rel_bias, even_norm1, even_w_in, diff_lambda, diff_subln, even_w_out,
           odd_norm1, odd_w_in, mla_q_norm, mla_w_uq, mla_kv_norm, mla_w_ukv, odd_w_out,
           ffn_norm, ffn_w_in, ffn_conv_w, ffn_conv_b, ffn_w_out, final_norm):
    b, s, d = x.shape
    m = b * s
    bias_vecs = bias_vectors(rel_bias, s)
    rope_tabs = _rope_tables(s)
    qscale = HEAD_DIM ** -0.5 * LOG2E
    even_w_out_bf16 = even_w_out.astype(BF16)
    odd_w_out_bf16 = odd_w_out.astype(BF16)
    h = x.reshape(m, d)
    for layer in range(DEPTH):
        li = layer // 2
        if layer % 2 == 0:
            lam_init = 0.8 - 0.6 * math.exp(-0.3 * layer)
            mw, dw = MOBA_HEADS * HEAD_DIM, DIFF_HEADS * 2 * HEAD_DIM
            colscale = np.ones((3 * mw + 3 * dw,), np.float32)
            colscale[:mw] = qscale
            colscale[3 * mw:3 * mw + dw] = qscale
            w = (even_w_in[li] * jnp.asarray(colscale)).astype(BF16)
            qkv, w_ug = in_proj(h, even_norm1[li], w, [w.shape[1]], [BF16], ffn_w_in, layer)
            qkv = qkv.reshape(b, s, -1)
            nb = mw // LANES
            o_a = attention(qkv, 0, nb, 2 * nb, nb, bias_vecs, 0, "pair", moba=True)
            o_b = attention(qkv, 3 * nb, 4 * nb, 5 * nb, DIFF_HEADS, bias_vecs, MOBA_HEADS, "diff",
                            lam_params=diff_lambda[li], subln_g=diff_subln[li], lam_init=lam_init)
            wo = even_w_out_bf16
        else:
            w = odd_w_in[li]
            dw = DIL_HEADS * HEAD_DIM
            lat0 = 3 * dw
            colscale = np.ones((w.shape[1],), np.float32)
            colscale[:dw] = qscale
            kr_cols = w[:, lat0 + MLA_Q_RANK + MLA_KV_RANK:]
            w_all = jnp.concatenate([w * jnp.asarray(colscale), _swap_halves(kr_cols)],
                                    axis=1).astype(BF16)
            qkv, latent, w_ug = in_proj(h, odd_norm1[li], w_all, [lat0, w_all.shape[1] - lat0],
                                        [BF16, F32], ffn_w_in, layer)
            qkv = qkv.reshape(b, s, -1)
            nb = dw // LANES
            o_a = attention(qkv, 0, nb, 2 * nb, nb, bias_vecs, MOBA_HEADS + DIFF_HEADS, "pair")
            wa, wb = _mla_q_weights(mla_w_uq[li])
            wkv = mla_w_ukv[li].reshape(MLA_KV_RANK, MLA_HEADS, MLA_NOPE_DIM + MLA_V_DIM)
            wk = wkv[..., :MLA_NOPE_DIM].reshape(MLA_KV_RANK, -1).astype(BF16)
            wv = wkv[..., MLA_NOPE_DIM:].reshape(MLA_KV_RANK, -1).astype(BF16)
            q_m, kn_m, kr_m, v_m = mla_up(latent, mla_q_norm[li], mla_kv_norm[li], wa, wb, wk, wv,
                                          rope_tabs, s)
            o_b = mla_attention(q_m.reshape(b, s, -1), kn_m.reshape(b, s, -1),
                                kr_m.reshape(b, s, -1), v_m.reshape(b, s, -1))
            wo = odd_w_out_bf16
        act, h3 = attn_out_ffn_in(o_a, o_b, wo, li, h.reshape(b, s, d), ffn_norm[layer],
                                  w_ug, ffn_conv_w[layer], ffn_conv_b[layer])
        h = proj_residual([act.reshape(m, D_FF)], ffn_w_out, layer, h3.reshape(m, d),
                          final_g=final_norm if layer == DEPTH - 1 else None)
    return h.reshape(b, s, d)
```
